```python
import jax, jax.numpy as jnp
from jax import lax
import numpy as np


D_MODEL = 1024
BATCH = 16
SEQ = 4096
DEPTH = 1

GRID_W = 64
CTX_LEN = 256
N_HEADS = 8
N_KV_HEADS = 2
HEAD_DIM = 64
WINDOW = 128
BLOCK = 128
ROPE_BASE = 10000.0
M_HEADS = 4
M_DIM = 128
CHUNK = 128
CONV_W = 3
ATT_WIDTH = N_HEADS * HEAD_DIM
KV_WIDTH = N_KV_HEADS * HEAD_DIM
M_WIDTH = M_HEADS * M_DIM
MIX_WIDTH = ATT_WIDTH + M_WIDTH
IN_SPLITS = (ATT_WIDTH, KV_WIDTH, KV_WIDTH, M_WIDTH, M_WIDTH, M_WIDTH, M_WIDTH, 4 * M_HEADS)
IN_COLS = sum(IN_SPLITS)
N_EXPERTS = 16
EXPERT_FF = 512
CAPACITY = 2
EPS = 1e-6
NEG = -1e30

kernel_name = 'hybrid_swa_mlstm_ecmoe_dit_block'

f32 = jnp.float32


def rmsnorm(x, w):
    xf = x.astype(f32)
    y = xf * lax.rsqrt(jnp.mean(xf * xf, -1, keepdims=True) + EPS)
    return (y * w.astype(f32)).astype(x.dtype)


def modulate(h, shift, scale):
    return h * (1 + scale) + shift


def axial_rope_tables(n):
    rows = n // GRID_W
    row, col = jnp.meshgrid(jnp.arange(rows), jnp.arange(GRID_W), indexing='ij')
    n_freq = HEAD_DIM // 4
    freqs = ROPE_BASE ** (-jnp.arange(n_freq, dtype=f32) / n_freq)
    ang = jnp.concatenate([row.reshape(-1, 1).astype(f32) * freqs,
                           col.reshape(-1, 1).astype(f32) * freqs], -1)
    return jnp.cos(ang), jnp.sin(ang)


def apply_rope(x, cos, sin):
    shape = (1, cos.shape[0]) + (1,) * (x.ndim - 3) + (cos.shape[1],)
    cos = cos.reshape(shape)
    sin = sin.reshape(shape)
    x1, x2 = jnp.split(x.astype(f32), 2, -1)
    return jnp.concatenate([x1 * cos - x2 * sin, x2 * cos + x1 * sin], -1).astype(x.dtype)


def short_conv(u, w):
    n = u.shape[1]
    r = CONV_W // 2
    up = jnp.pad(u, ((0, 0), (r, r), (0, 0)))
    out = up[:, 0:n] * w[0]
    for j in range(1, CONV_W):
        out = out + up[:, j:j + n] * w[j]
    return out


def mixer_inputs(h, w_in, b_gates, conv_qk, q_norm_w, k_norm_w):
    B, N, _ = h.shape
    offs = np.cumsum(IN_SPLITS)[:-1].tolist()
    aq, ak, av, mq, mk, mv, mo, gates = jnp.split(h @ w_in, offs, -1)
    aq = rmsnorm(aq.reshape(B, N, N_KV_HEADS, N_HEADS // N_KV_HEADS, HEAD_DIM), q_norm_w)
    ak = rmsnorm(ak.reshape(B, N, N_KV_HEADS, HEAD_DIM), k_norm_w)
    av = av.reshape(B, N, N_KV_HEADS, HEAD_DIM)
    mqk = jax.nn.silu(short_conv(jnp.concatenate([mq, mk], -1), conv_qk))
    mq, mk = jnp.split(mqk, 2, -1)

    def heads(a):
        return a.reshape(B, N, M_HEADS, M_DIM).transpose(0, 2, 1, 3).astype(f32)

    g = (gates + b_gates).astype(f32).reshape(B, N, 4, M_HEADS).transpose(2, 0, 3, 1)
    li_f, f_f, li_b, f_b = g[0], g[1], g[2], g[3]
    return (aq, ak, av, heads(mq) * (M_DIM ** -0.5), heads(mk), heads(mv), mo,
            li_f, jax.nn.log_sigmoid(f_f), li_b, jax.nn.log_sigmoid(f_b))


def latent_attention(q, k, v, k_ctx, v_ctx, sink):
    B, N = q.shape[:2]
    L = k_ctx.shape[1]
    nb = N // BLOCK
    span = BLOCK + 2 * WINDOW
    pad = ((0, 0), (WINDOW, WINDOW), (0, 0), (0, 0))
    idx = jnp.arange(nb)[:, None] * BLOCK + jnp.arange(span)[None]
    k_blk = jnp.moveaxis(jnp.pad(k, pad)[:, idx], 1, 0)
    v_blk = jnp.moveaxis(jnp.pad(v, pad)[:, idx], 1, 0)
    q_blk = jnp.moveaxis(q.reshape((B, nb, BLOCK) + q.shape[2:]), 1, 0)
    scale = HEAD_DIM ** -0.5
    sink_f = sink.astype(f32)

    def one_block(args):
        qb, kb, vb, n = args
        qpos = n * BLOCK + jnp.arange(BLOCK)
        kpos = n * BLOCK - WINDOW + jnp.arange(span)
        ok = (kpos[None] >= 0) & (kpos[None] < N) & (jnp.abs(qpos[:, None] - kpos[None]) <= WINDOW)
        s_loc = jnp.einsum('bqkgd,bskd->bkgqs', qb, kb, preferred_element_type=f32) * scale
        s_loc = jnp.where(ok, s_loc, NEG)
        s_ctx = jnp.einsum('bqkgd,bckd->bkgqc', qb, k_ctx, preferred_element_type=f32) * scale
        s_sink = jnp.broadcast_to(sink_f[None, :, :, None, None], s_loc.shape[:-1] + (1,))
        p = jax.nn.softmax(jnp.concatenate([s_loc, s_ctx, s_sink], -1), -1)
        p_loc = p[..., :span].astype(vb.dtype)
        p_ctx = p[..., span:span + L].astype(vb.dtype)
        return (jnp.einsum('bkgqs,bskd->bqkgd', p_loc, vb)
                + jnp.einsum('bkgqc,bckd->bqkgd', p_ctx, v_ctx))

    o = lax.map(one_block, (q_blk, k_blk, v_blk, jnp.arange(nb)))
    return jnp.moveaxis(o, 0, 1).reshape(B, N, ATT_WIDTH)


def context_attention(q, k, v, sink):
    B, L = q.shape[:2]
    s = jnp.einsum('bqkgd,bckd->bkgqc', q, k, preferred_element_type=f32) * (HEAD_DIM ** -0.5)
    s_sink = jnp.broadcast_to(sink.astype(f32)[None, :, :, None, None], s.shape[:-1] + (1,))
    p = jax.nn.softmax(jnp.concatenate([s, s_sink], -1), -1)[..., :L].astype(v.dtype)
    return jnp.einsum('bkgqc,bckd->bqkgd', p, v).reshape(B, L, ATT_WIDTH)


def zero_state(batch):
    return (jnp.zeros((batch, M_HEADS, M_DIM, M_DIM), f32),
            jnp.zeros((batch, M_HEADS, M_DIM), f32),
            jnp.zeros((batch, M_HEADS), f32))


def mlstm_state_update(state, k, v, li, lf):
    C, n, m = state
    b = jnp.cumsum(lf, -1)
    b_last = b[..., -1]
    w = b_last[..., None] - b + li
    m_new = jnp.maximum(b_last + m, w.max(-1))
    a = jnp.exp(b_last + m - m_new)
    ws = jnp.exp(w - m_new[..., None])
    C_new = a[..., None, None] * C + jnp.einsum('bhs,bhsd,bhsv->bhdv', ws, k, v)
    n_new = a[..., None] * n + jnp.einsum('bhs,bhsd->bhd', ws, k)
    return (C_new, n_new, m_new)


def mlstm_chunk(state, xs):
    q, k, v, li, lf = xs
    C, n, m = state
    b = jnp.cumsum(lf, -1)
    within = jnp.tril(jnp.ones((CHUNK, CHUNK), bool))
    d = jnp.where(within, b[..., :, None] - b[..., None, :] + li[..., None, :], -jnp.inf)
    inter = b + m[..., None]
    m_t = jnp.maximum(inter, d.max(-1))
    wts = jnp.exp(d - m_t[..., None]) * jnp.einsum('bhtd,bhsd->bhts', q, k)
    decay = jnp.exp(inter - m_t)
    num = jnp.einsum('bhts,bhsv->bhtv', wts, v) + decay[..., None] * jnp.einsum('bhtd,bhdv->bhtv', q, C)
    den = wts.sum(-1) + decay * jnp.einsum('bhtd,bhd->bht', q, n)
    h = num / jnp.maximum(jnp.abs(den), jnp.exp(-m_t))[..., None]
    return mlstm_state_update(state, k, v, li, lf), h


def mlstm_scan(q, k, v, li, lf, state):
    B, H, N, _ = q.shape
    nc = N // CHUNK

    def chunks(a):
        return jnp.moveaxis(a.reshape((B, H, nc, CHUNK) + a.shape[3:]), 2, 0)

    state, h = lax.scan(mlstm_chunk, state, (chunks(q), chunks(k), chunks(v), chunks(li), chunks(lf)))
    return jnp.moveaxis(h, 0, 2).reshape(B, H, N, M_DIM), state


def flip(a):
    return jnp.flip(a, 2)


def bidir_mlstm(mq, mk, mv, li_f, lf_f, li_b, lf_b, st_f, st_b):
    h_f, end_f = mlstm_scan(mq, mk, mv, li_f, lf_f, st_f)
    h_b, end_b = mlstm_scan(flip(mq), flip(mk), flip(mv), flip(li_b), flip(lf_b), st_b)
    return h_f + flip(h_b), end_f, end_b


def mlstm_out(h_sum, mo, norm_w):
    B, H, N, _ = h_sum.shape
    h = jnp.moveaxis(h_sum, 1, 2)
    h = h * lax.rsqrt(jnp.mean(h * h, -1, keepdims=True) + EPS) * norm_w.astype(f32).reshape(M_HEADS, M_DIM)
    return (h.reshape(B, N, M_WIDTH) * jax.nn.sigmoid(mo.astype(f32))).astype(mo.dtype)


def expert_choice_ffn(h, w_router, w_gate, w_up, w_down):
    B, N, _ = h.shape
    cap = CAPACITY * N // N_EXPERTS
    aff = jax.nn.softmax((h @ w_router).astype(f32), -1)
    gate, idx = lax.top_k(jnp.swapaxes(aff, 1, 2), cap)
    bidx = jnp.arange(B)[:, None, None]
    xg = h[bidx, idx]
    g = jnp.einsum('becd,edf->becf', xg, w_gate)
    u = jnp.einsum('becd,edf->becf', xg, w_up)
    y = jnp.einsum('becf,efd->becd', jax.nn.silu(g) * u, w_down) * gate[..., None].astype(h.dtype)
    return jnp.zeros_like(h).at[bidx, idx].add(y)


def layer(x, ctx, mod_x, mod_c, norm1_w, norm2_w, w_in, b_gates, conv_qk, q_norm_w, k_norm_w,
          sink, mlstm_norm_w, w_out, w_router, w_gate, w_up, w_down, update_ctx):
    B = x.shape[0]
    sh1, sc1, g1, sh2, sc2, g2 = jnp.split(mod_x[:, None, :], 6, -1)
    csh1, csc1, cg1, csh2, csc2, cg2 = jnp.split(mod_c, 6, -1)
    sink_r = sink.reshape(N_KV_HEADS, N_HEADS // N_KV_HEADS)

    hx = modulate(rmsnorm(x, norm1_w), sh1, sc1)
    hc = modulate(rmsnorm(ctx, norm1_w), csh1, csc1)
    aq, ak, av, mq, mk, mv, mo, li_f, lf_f, li_b, lf_b = mixer_inputs(hx, w_in, b_gates, conv_qk, q_norm_w, k_norm_w)
    caq, cak, cav, cmq, cmk, cmv, cmo, cli_f, clf_f, cli_b, clf_b = mixer_inputs(hc, w_in, b_gates, conv_qk, q_norm_w, k_norm_w)

    cos, sin = axial_rope_tables(x.shape[1])
    att = latent_attention(apply_rope(aq, cos, sin), apply_rope(ak, cos, sin), av, cak, cav, sink_r)

    if update_ctx:
        hc_sum, st_f, st_b = bidir_mlstm(cmq, cmk, cmv, cli_f, clf_f, cli_b, clf_b, zero_state(B), zero_state(B))
        c_mix = jnp.concatenate([context_attention(caq, cak, cav, sink_r), mlstm_out(hc_sum, cmo, mlstm_norm_w)], -1)
        ctx_new = ctx + cg1 * (c_mix @ w_out)
        hc2 = modulate(rmsnorm(ctx_new, norm2_w), csh2, csc2)
        ctx_new = ctx_new + cg2 * expert_choice_ffn(hc2, w_router, w_gate, w_up, w_down)
    else:
        st_f = mlstm_state_update(zero_state(B), cmk, cmv, cli_f, clf_f)
        st_b = mlstm_state_update(zero_state(B), flip(cmk), flip(cmv), flip(cli_b), flip(clf_b))
        ctx_new = ctx

    h_sum, _, _ = bidir_mlstm(mq, mk, mv, li_f, lf_f, li_b, lf_b, st_f, st_b)
    mix = jnp.concatenate([att, mlstm_out(h_sum, mo, mlstm_norm_w)], -1)
    x = x + g1 * (mix @ w_out)

    hx2 = modulate(rmsnorm(x, norm2_w), sh2, sc2)
    x = x + g2 * expert_choice_ffn(hx2, w_router, w_gate, w_up, w_down)
    return x, ctx_new


def setup_inputs(seed: int = 0) -> dict:
    key = jax.random.key(seed)
    ks = jax.random.split(key, 24)
    D = D_MODEL

    def nrm(k, shape, scale):
        return jax.random.normal(k, shape, f32) * scale

    f_bias = 3.0 + 0.5 * jax.random.normal(ks[20], (DEPTH, 2, M_HEADS), f32)
    i_bias = 0.1 * jax.random.normal(ks[21], (DEPTH, 2, M_HEADS), f32)
    b_gates = jnp.stack([i_bias[:, 0], f_bias[:, 0], i_bias[:, 1], f_bias[:, 1]], 1).reshape(DEPTH, 4 * M_HEADS)
    return {
        'x': nrm(ks[0], (BATCH, SEQ, D), 1.0),
        'c': nrm(ks[1], (BATCH, D), 1.0),
        'ctx': nrm(ks[2], (BATCH, CTX_LEN, D), 1.0),
        'c_ctx': nrm(ks[3], (D,), 1.0),
        'w_mod': nrm(ks[4], (DEPTH, D, 6 * D), 0.5 * D ** -0.5),
        'b_mod': nrm(ks[5], (DEPTH, 6 * D), 0.02),
        'norm1_w': 1.0 + nrm(ks[6], (DEPTH, D), 0.02),
        'norm2_w': 1.0 + nrm(ks[7], (DEPTH, D), 0.02),
        'w_in': nrm(ks[8], (DEPTH, D, IN_COLS), D ** -0.5),
        'b_gates': b_gates,
        'conv_qk': nrm(ks[9], (DEPTH, CONV_W, 2 * M_WIDTH), CONV_W ** -0.5),
        'q_norm_w': 1.0 + nrm(ks[10], (DEPTH, HEAD_DIM), 0.02),
        'k_norm_w': 1.0 + nrm(ks[11], (DEPTH, HEAD_DIM), 0.02),
        'sink': nrm(ks[12], (DEPTH, N_HEADS), 0.5),
        'mlstm_norm_w': 1.0 + nrm(ks[13], (DEPTH, M_WIDTH), 0.02),
        'w_out': nrm(ks[14], (DEPTH, MIX_WIDTH, D), MIX_WIDTH ** -0.5),
        'w_router': nrm(ks[15], (DEPTH, D, N_EXPERTS), D ** -0.5),
        'w_gate': nrm(ks[16], (DEPTH, N_EXPERTS, D, EXPERT_FF), D ** -0.5),
        'w_up': nrm(ks[17], (DEPTH, N_EXPERTS, D, EXPERT_FF), D ** -0.5),
        'w_down': nrm(ks[18], (DEPTH, N_EXPERTS, EXPERT_FF, D), EXPERT_FF ** -0.5),
    }


def reference(x, c, ctx, c_ctx, w_mod, b_mod, norm1_w, norm2_w, w_in, b_gates, conv_qk, q_norm_w,
              k_norm_w, sink, mlstm_norm_w, w_out, w_router, w_gate, w_up, w_down):
    for l in range(DEPTH):
        mod_x = jax.nn.silu(c) @ w_mod[l] + b_mod[l]
        mod_c = jax.nn.silu(c_ctx) @ w_mod[l] + b_mod[l]
        x, ctx = layer(x, ctx, mod_x, mod_c, norm1_w[l], norm2_w[l], w_in[l], b_gates[l], conv_qk[l],
                       q_norm_w[l], k_norm_w[l], sink[l], mlstm_norm_w[l], w_out[l], w_router[l],
                       w_gate[l], w_up[l], w_down[l], l < DEPTH - 1)
    return x
```

```python
import functools

import jax
import jax.numpy as jnp
from jax import lax
from jax.experimental import pallas as pl
from jax.experimental.pallas import tpu as pltpu

f32 = jnp.float32
bf16 = jnp.bfloat16

GRID_W = 64
N_HEADS = 8
N_KV_HEADS = 2
HEAD_DIM = 64
WINDOW = 128
BLOCK = 128
ROPE_BASE = 10000.0
M_HEADS = 4
M_DIM = 128
CHUNK = 128
CONV_W = 3
ATT_WIDTH = N_HEADS * HEAD_DIM
KV_WIDTH = N_KV_HEADS * HEAD_DIM
M_WIDTH = M_HEADS * M_DIM
N_EXPERTS = 16
CAPACITY = 2
EPS = 1e-6
NEG = -1e30

LANES = 128
SUBLANES = 8
VMEM_LIMIT_BYTES = 56 * 1024 * 1024

K2_WIDTH = 2 * KV_WIDTH
C_Q = 0
C_K = C_Q + ATT_WIDTH
C_V = C_K + K2_WIDTH
C_MQK = C_V + K2_WIDTH
C_MV = C_MQK + 2 * M_WIDTH
C_MO = C_MV + M_WIDTH
C_G = C_MO + M_WIDTH
W_COLS = C_G + LANES
N_GATES = 4 * M_HEADS


def _dot(a, b):
    return jnp.dot(a, b, preferred_element_type=f32)


def _dot_nt(a, b):
    return lax.dot_general(a, b, (((1,), (1,)), ((), ())), preferred_element_type=f32)


def _dot_tn(a, b):
    return lax.dot_general(a, b, (((0,), (0,)), ((), ())), preferred_element_type=f32)


def _split3(x):
    h = x.astype(bf16)
    r = x - h.astype(f32)
    m = r.astype(bf16)
    l = (r - m.astype(f32)).astype(bf16)
    return h, m, l


def _log_sigmoid(x):
    return jnp.minimum(x, 0.0) - jnp.log1p(jnp.exp(-jnp.abs(x)))


def _silu(x):
    return x / (1.0 + jnp.exp(-x))


def _mod_kernel(c_ref, w_ref, b_ref, o_ref):
    h, m, l = _split3(_silu(c_ref[...]))
    w = w_ref[...]
    wh = w.astype(bf16)
    wl = (w - wh.astype(f32)).astype(bf16)
    acc = _dot(h, wh) + _dot(m, wh) + _dot(h, wl) + _dot(l, wh) + _dot(m, wl)
    o_ref[...] = acc + b_ref[...]


def _modulation(c_all, w_mod, b_mod):
    rows, d = c_all.shape
    cols = w_mod.shape[1]
    tn = 512
    return pl.pallas_call(
        _mod_kernel,
        grid=(cols // tn,),
        in_specs=[pl.BlockSpec((rows, d), lambda j: (0, 0)),
                  pl.BlockSpec((d, tn), lambda j: (0, j)),
                  pl.BlockSpec((1, tn), lambda j: (0, j))],
        out_specs=pl.BlockSpec((rows, tn), lambda j: (0, j)),
        out_shape=jax.ShapeDtypeStruct((rows, cols), f32),
        name="adaln_mod",
    )(c_all, w_mod, b_mod.reshape(1, cols))


def _inproj_kernel(xp_ref, x_ref, xn_ref, shift_ref, scale_ref, nw_ref, w_ref, cos_ref, sin_ref, qkw_ref,
                   conv_ref, bg_ref,
                   aq_ref, k2_ref, v2_ref, mq_ref, mk_ref, mv_ref, mo_ref, g_ref, conv_scr, *, tm, nt):
    i = pl.program_id(1)
    nw = nw_ref[...]
    sc = 1.0 + scale_ref[0]
    sh = shift_ref[0]

    def prep(xv):
        ms = jnp.mean(xv * xv, axis=-1, keepdims=True)
        return (xv * lax.rsqrt(ms + EPS) * nw) * sc + sh

    hm = prep(x_ref[0])
    lhs = hm.astype(bf16)
    lhs_halo = jnp.concatenate([prep(xp_ref[0]), hm, prep(xn_ref[0])], axis=0).astype(bf16)

    lane = lax.broadcasted_iota(jnp.int32, (1, LANES), 1)
    lo = lane < HEAD_DIM
    first_half = (lane % HEAD_DIM) < (HEAD_DIM // 2)
    cos = cos_ref[...]
    sin = sin_ref[...]
    n_qk = (ATT_WIDTH + K2_WIDTH) // LANES
    for g in range(n_qk):
        c0 = g * LANES
        v = _dot(lhs, w_ref[:, c0:c0 + LANES])
        sq = v * v
        s_all = jnp.sum(sq, axis=-1, keepdims=True)
        s_lo = jnp.sum(jnp.where(lo, sq, 0.0), axis=-1, keepdims=True)
        ms = jnp.where(lo, s_lo, s_all - s_lo) * (1.0 / HEAD_DIM)
        nv = v * lax.rsqrt(ms + EPS) * qkw_ref[:, c0:c0 + LANES]
        swapped = jnp.where(first_half, pltpu.roll(nv, LANES - HEAD_DIM // 2, 1), pltpu.roll(nv, HEAD_DIM // 2, 1))
        r = nv * cos + swapped * sin
        if c0 < ATT_WIDTH:
            aq_ref[0, :, c0:c0 + LANES] = (r * (HEAD_DIM ** -0.5)).astype(bf16)
        else:
            k2_ref[0, :, c0 - ATT_WIDTH:c0 - ATT_WIDTH + LANES] = r.astype(bf16)

    v2_ref[0] = _dot(lhs, w_ref[:, C_V:C_V + K2_WIDTH]).astype(bf16)
    mv_ref[0] = _dot(lhs, w_ref[:, C_MV:C_MV + M_WIDTH]).astype(bf16)
    mo_ref[0] = _dot(lhs, w_ref[:, C_MO:C_MO + M_WIDTH]).astype(bf16)

    g = _dot(lhs, w_ref[:, C_G:C_G + LANES]) + bg_ref[...]
    is_forget = ((lane // M_HEADS) % 2) == 1
    g_ref[0] = jnp.where(is_forget, _log_sigmoid(g), g)[:, :N_GATES]

    conv_scr[...] = _dot(lhs_halo, w_ref[:, C_MQK:C_MQK + 2 * M_WIDTH])
    row = lax.broadcasted_iota(jnp.int32, (tm, 1), 0)
    prev = conv_scr[SUBLANES - 1:SUBLANES - 1 + tm, :]
    prev = jnp.where((row == 0) & (i == 0), 0.0, prev)
    nxt = conv_scr[SUBLANES + 1:SUBLANES + 1 + tm, :]
    nxt = jnp.where((row == tm - 1) & (i == nt - 1), 0.0, nxt)
    cur = conv_scr[SUBLANES:SUBLANES + tm, :]
    u = prev * conv_ref[0:1, :] + cur * conv_ref[1:2, :] + nxt * conv_ref[2:3, :]
    u = _silu(u)
    mq_ref[0] = (u[:, :M_WIDTH] * (M_DIM ** -0.5)).astype(bf16)
    mk_ref[0] = u[:, M_WIDTH:].astype(bf16)


def _in_projection(x, shift, scale, norm_w, w_all, cos_t, sin_t, qk_w, conv_qk, bg, *, tm):
    B, N, D = x.shape
    nt = N // tm
    hb = tm // SUBLANES
    nblk8 = N // SUBLANES
    kern = functools.partial(_inproj_kernel, tm=tm, nt=nt)
    row_map = lambda b, i: (b, i, 0)
    const2 = lambda b, i: (0, 0)
    outs = pl.pallas_call(
        kern,
        grid=(B, nt),
        in_specs=[
            pl.BlockSpec((1, SUBLANES, D), lambda b, i: (b, jnp.maximum(i * hb - 1, 0), 0)),
            pl.BlockSpec((1, tm, D), row_map),
            pl.BlockSpec((1, SUBLANES, D), lambda b, i: (b, jnp.minimum((i + 1) * hb, nblk8 - 1), 0)),
            pl.BlockSpec((1, 1, D), lambda b, i: (b, 0, 0)),
            pl.BlockSpec((1, 1, D), lambda b, i: (b, 0, 0)),
            pl.BlockSpec((1, D), const2),
            pl.BlockSpec((D, W_COLS), const2),
            pl.BlockSpec((tm, LANES), lambda b, i: (i, 0)),
            pl.BlockSpec((tm, LANES), lambda b, i: (i, 0)),
            pl.BlockSpec((1, ATT_WIDTH + K2_WIDTH), const2),
            pl.BlockSpec((CONV_W, 2 * M_WIDTH), const2),
            pl.BlockSpec((1, LANES), const2),
        ],
        out_specs=[
            pl.BlockSpec((1, tm, ATT_WIDTH), row_map),
            pl.BlockSpec((1, tm, K2_WIDTH), row_map),
            pl.BlockSpec((1, tm, K2_WIDTH), row_map),
            pl.BlockSpec((1, tm, M_WIDTH), row_map),
            pl.BlockSpec((1, tm, M_WIDTH), row_map),
            pl.BlockSpec((1, tm, M_WIDTH), row_map),
            pl.BlockSpec((1, tm, M_WIDTH), row_map),
            pl.BlockSpec((1, tm, N_GATES), row_map),
        ],
        out_shape=[
            jax.ShapeDtypeStruct((B, N, ATT_WIDTH), bf16),
            jax.ShapeDtypeStruct((B, N, K2_WIDTH), bf16),
            jax.ShapeDtypeStruct((B, N, K2_WIDTH), bf16),
            jax.ShapeDtypeStruct((B, N, M_WIDTH), bf16),
            jax.ShapeDtypeStruct((B, N, M_WIDTH), bf16),
            jax.ShapeDtypeStruct((B, N, M_WIDTH), bf16),
            jax.ShapeDtypeStruct((B, N, M_WIDTH), bf16),
            jax.ShapeDtypeStruct((B, N, N_GATES), f32),
        ],
        scratch_shapes=[pltpu.VMEM((tm + 2 * SUBLANES, 2 * M_WIDTH), f32)],
        compiler_params=pltpu.CompilerParams(
            dimension_semantics=("parallel", "parallel"), vmem_limit_bytes=VMEM_LIMIT_BYTES),
        name="in_projection",
    )(x, x, x, shift, scale, norm_w, w_all, cos_t, sin_t, qk_w, conv_qk, bg)
    return outs


def _attn_kernel(sink_ref, q_ref, kp_ref, kc_ref, kn_ref, vp_ref, vc_ref, vn_ref, ck_ref, cv_ref, o_ref, *, nb):
    n = pl.program_id(1)
    span = BLOCK + 2 * WINDOW
    grp = N_HEADS // N_KV_HEADS
    lane = lax.broadcasted_iota(jnp.int32, (1, LANES), 1)
    lo = lane < HEAD_DIM
    qi = lax.broadcasted_iota(jnp.int32, (BLOCK, span), 0)
    kj = lax.broadcasted_iota(jnp.int32, (BLOCK, span), 1)
    rel = kj - qi
    valid = (rel >= 0) & (rel <= 2 * WINDOW)
    valid = valid & ((kj >= WINDOW) | (n > 0)) & ((kj < WINDOW + BLOCK) | (n < nb - 1))
    valid2 = jnp.concatenate([valid, valid], axis=0)
    row2 = lax.broadcasted_iota(jnp.int32, (2 * BLOCK, 1), 0)
    zero = jnp.zeros((), bf16)
    for kh in range(N_KV_HEADS):
        ks = slice(kh * LANES, (kh + 1) * LANES)
        kk = jnp.concatenate([kp_ref[0, :, ks], kc_ref[0, :, ks], kn_ref[0, :, ks]], axis=0)
        vv = jnp.concatenate([vp_ref[0, :, ks], vc_ref[0, :, ks], vn_ref[0, :, ks]], axis=0)
        ckk = ck_ref[0, :, ks]
        cvv = cv_ref[0, :, ks]
        q0 = kh * grp * HEAD_DIM
        q2 = jnp.concatenate([q_ref[0, :, q0:q0 + LANES], q_ref[0, :, q0 + LANES:q0 + 2 * LANES]], axis=0)
        outs = []
        for half in range(2):
            keep = lo if half == 0 else jnp.logical_not(lo)
            kl = jnp.where(keep, kk, zero)
            ckl = jnp.where(keep, ckk, zero)
            s_loc = jnp.where(valid2, _dot_nt(q2, kl), NEG)
            s_ctx = _dot_nt(q2, ckl)
            sk = jnp.where(row2 < BLOCK, sink_ref[kh * grp + half], sink_ref[kh * grp + 2 + half])
            m = jnp.maximum(jnp.maximum(jnp.max(s_loc, axis=-1, keepdims=True),
                                        jnp.max(s_ctx, axis=-1, keepdims=True)), sk)
            p_loc = jnp.exp(s_loc - m)
            p_ctx = jnp.exp(s_ctx - m)
            den = (jnp.sum(p_loc, axis=-1, keepdims=True) + jnp.sum(p_ctx, axis=-1, keepdims=True)
                   + jnp.exp(sk - m))
            o2 = _dot(p_loc.astype(bf16), vv) + _dot(p_ctx.astype(bf16), cvv)
            outs.append(o2 / den)
        ok = jnp.where(lo, outs[0], outs[1]).astype(o_ref.dtype)
        o_ref[0, :, q0:q0 + LANES] = ok[:BLOCK]
        o_ref[0, :, q0 + LANES:q0 + 2 * LANES] = ok[BLOCK:]


def _attention(sink, q, k2, v2, ck2, cv2):
    B, N, _ = q.shape
    L = ck2.shape[1]
    nb = N // BLOCK
    kern = functools.partial(_attn_kernel, nb=nb)
    prev_map = lambda b, n, s: (b, jnp.maximum(n - 1, 0), 0)
    cur_map = lambda b, n, s: (b, n, 0)
    next_map = lambda b, n, s: (b, jnp.minimum(n + 1, nb - 1), 0)
    ctx_map = lambda b, n, s: (b, 0, 0)
    kv_spec = lambda m: pl.BlockSpec((1, BLOCK, K2_WIDTH), m)
    return pl.pallas_call(
        kern,
        grid_spec=pltpu.PrefetchScalarGridSpec(
            num_scalar_prefetch=1,
            grid=(B, nb),
            in_specs=[pl.BlockSpec((1, BLOCK, ATT_WIDTH), cur_map),
                      kv_spec(prev_map), kv_spec(cur_map), kv_spec(next_map),
                      kv_spec(prev_map), kv_spec(cur_map), kv_spec(next_map),
                      pl.BlockSpec((1, L, K2_WIDTH), ctx_map),
                      pl.BlockSpec((1, L, K2_WIDTH), ctx_map)],
            out_specs=pl.BlockSpec((1, BLOCK, ATT_WIDTH), cur_map),
        ),
        out_shape=jax.ShapeDtypeStruct((B, N, ATT_WIDTH), bf16),
        compiler_params=pltpu.CompilerParams(dimension_semantics=("parallel", "parallel")),
        name="window_attention",
    )(sink, q, k2, k2, k2, v2, v2, v2, ck2, cv2)


def _tri_cumsum_cols(tri, x):
    h, m, l = _split3(x)
    return _dot(tri, h) + _dot(tri, m) + _dot(tri, l)


def _tri_cumsum_rows(x, tri_t):
    h, m, l = _split3(x)
    return _dot(h, tri_t) + _dot(m, tri_t) + _dot(l, tri_t)


def _tri(t):
    r = lax.broadcasted_iota(jnp.int32, (t, t), 0)
    c = lax.broadcasted_iota(jnp.int32, (t, t), 1)
    return jnp.where(c <= r, 1.0, 0.0).astype(bf16), jnp.where(r <= c, 1.0, 0.0).astype(bf16)


def _ctx_state_kernel(k_ref, v_ref, li_ref, lf_ref, c_ref, n_ref, m_ref):
    L = k_ref.shape[1]
    tri, _ = _tri(L)
    li = li_ref[0]
    lf = lf_ref[0]
    lane = lax.broadcasted_iota(jnp.int32, (1, 2 * M_HEADS), 1)
    fwd = lane < M_HEADS
    cs = _tri_cumsum_cols(tri, lf)
    tot = cs[L - 1:L, :]
    b = jnp.where(fwd, cs, tot - cs + lf)
    w = tot - b + li
    m_new = jnp.maximum(tot, jnp.max(w, axis=0, keepdims=True))
    ws = jnp.exp(w - m_new)
    m_ref[0] = m_new
    for c in range(2 * M_HEADS):
        hs = slice((c % M_HEADS) * M_DIM, (c % M_HEADS + 1) * M_DIM)
        ks = k_ref[0, :, hs].astype(f32) * ws[:, c:c + 1]
        c_ref[0, c] = _dot_tn(ks.astype(bf16), v_ref[0, :, hs])
        n_ref[0, c] = jnp.sum(ks, axis=0, keepdims=True)


def _ctx_states(cmk, cmv, li_c, lf_c):
    B, L, _ = cmk.shape
    S = 2 * M_HEADS
    bmap = lambda b: (b, 0, 0)
    return pl.pallas_call(
        _ctx_state_kernel,
        grid=(B,),
        in_specs=[pl.BlockSpec((1, L, M_WIDTH), bmap), pl.BlockSpec((1, L, M_WIDTH), bmap),
                  pl.BlockSpec((1, L, S), bmap), pl.BlockSpec((1, L, S), bmap)],
        out_specs=[pl.BlockSpec((1, S, M_DIM, M_DIM), lambda b: (b, 0, 0, 0)),
                   pl.BlockSpec((1, S, 1, M_DIM), lambda b: (b, 0, 0, 0)),
                   pl.BlockSpec((1, 1, S), bmap)],
        out_shape=[jax.ShapeDtypeStruct((B, S, M_DIM, M_DIM), f32),
                   jax.ShapeDtypeStruct((B, S, 1, M_DIM), f32),
                   jax.ShapeDtypeStruct((B, 1, S), f32)],
        compiler_params=pltpu.CompilerParams(dimension_semantics=("parallel",)),
        name="mlstm_ctx_state",
    )(cmk, cmv, li_c, lf_c)


def _mlstm_kernel(qf_ref, kf_ref, vf_ref, qb_ref, kb_ref, vb_ref,
                  licf_ref, lfcf_ref, licb_ref, lfcb_ref, lirf_ref, lfrf_ref, lirb_ref, lfrb_ref,
                  c0_ref, n0_ref, m0_ref, hf_ref, hb_ref, c_scr, n_scr, m_scr):
    j = pl.program_id(1)
    S = 2 * M_HEADS
    T = CHUNK

    @pl.when(j == 0)
    def _():
        c_scr[...] = c0_ref[0]
        n_scr[...] = n0_ref[0]
        m_scr[...] = m0_ref[0]

    tri, tri_t = _tri(T)
    lane = lax.broadcasted_iota(jnp.int32, (1, S), 1)
    fwd_c = lane < M_HEADS
    srow = lax.broadcasted_iota(jnp.int32, (S, 1), 0)
    fwd_r = srow < M_HEADS
    li_c = jnp.where(fwd_c, licf_ref[0], licb_ref[0])
    lf_c = jnp.where(fwd_c, lfcf_ref[0], lfcb_ref[0])
    li_r = jnp.where(fwd_r, lirf_ref[0], lirb_ref[0])
    lf_r = jnp.where(fwd_r, lfrf_ref[0], lfrb_ref[0])

    cs_c = _tri_cumsum_cols(tri, lf_c)
    tot = cs_c[T - 1:T, :]
    b_c = jnp.where(fwd_c, cs_c, tot - cs_c + lf_c)
    cs_r = _tri_cumsum_rows(lf_r, tri_t)
    tot_r = cs_r[:, T - 1:T]
    b_r = jnp.where(fwd_r, cs_r, tot_r - cs_r + lf_r)
    e_r = li_r - b_r

    m_prev = m_scr[...]
    inter = b_c + m_prev
    w = tot - b_c + li_c
    m_new = jnp.maximum(tot + m_prev, jnp.max(w, axis=0, keepdims=True))
    a = jnp.exp(tot + m_prev - m_new)
    ws = jnp.exp(w - m_new)
    m_scr[...] = m_new

    ti = lax.broadcasted_iota(jnp.int32, (T, T), 0)
    si = lax.broadcasted_iota(jnp.int32, (T, T), 1)
    for c in range(S):
        is_fwd = c < M_HEADS
        hs = slice((c % M_HEADS) * M_DIM, (c % M_HEADS + 1) * M_DIM)
        q = (qf_ref if is_fwd else qb_ref)[0, :, hs]
        k = (kf_ref if is_fwd else kb_ref)[0, :, hs]
        v = (vf_ref if is_fwd else vb_ref)[0, :, hs]
        within = (si <= ti) if is_fwd else (si >= ti)
        d = jnp.where(within, b_c[:, c:c + 1] + e_r[c:c + 1, :], -jnp.inf)
        inter_c = inter[:, c:c + 1]
        m_t = jnp.maximum(inter_c, jnp.max(d, axis=-1, keepdims=True))
        wts = jnp.exp(d - m_t) * _dot_nt(q, k)
        decay = jnp.exp(inter_c - m_t)
        cmat = c_scr[c]
        nvec = n_scr[c]
        num = _dot(wts.astype(bf16), v) + decay * _dot(q, cmat.astype(bf16))
        den = jnp.sum(wts, axis=-1, keepdims=True) + decay * jnp.sum(q.astype(f32) * nvec, axis=-1, keepdims=True)
        h = num / jnp.maximum(jnp.abs(den), jnp.exp(-m_t))
        (hf_ref if is_fwd else hb_ref)[0, :, hs] = h
        ks = k.astype(f32) * ws[:, c:c + 1]
        a_c = a[:, c:c + 1]
        c_scr[c] = a_c * cmat + _dot_tn(ks.astype(bf16), v)
        n_scr[c] = a_c * nvec + jnp.sum(ks, axis=0, keepdims=True)


def _mlstm(mq, mk, mv, li_c, lf_c, li_r, lf_r, c0, n0, m0):
    B, N, _ = mq.shape
    nc = N // CHUNK
    S = 2 * M_HEADS
    fmap = lambda b, j: (b, j, 0)
    bmap = lambda b, j: (b, nc - 1 - j, 0)
    frow = lambda b, j: (b, 0, j)
    brow = lambda b, j: (b, 0, nc - 1 - j)
    seq = lambda m: pl.BlockSpec((1, CHUNK, M_WIDTH), m)
    col = lambda m: pl.BlockSpec((1, CHUNK, S), m)
    rowspec = lambda m: pl.BlockSpec((1, S, CHUNK), m)
    return pl.pallas_call(
        _mlstm_kernel,
        grid=(B, nc),
        in_specs=[seq(fmap), seq(fmap), seq(fmap), seq(bmap), seq(bmap), seq(bmap),
                  col(fmap), col(fmap), col(bmap), col(bmap),
                  rowspec(frow), rowspec(frow), rowspec(brow), rowspec(brow),
                  pl.BlockSpec((1, S, M_DIM, M_DIM), lambda b, j: (b, 0, 0, 0)),
                  pl.BlockSpec((1, S, 1, M_DIM), lambda b, j: (b, 0, 0, 0)),
                  pl.BlockSpec((1, 1, S), lambda b, j: (b, 0, 0))],
        out_specs=[seq(fmap), seq(bmap)],
        out_shape=[jax.ShapeDtypeStruct((B, N, M_WIDTH), f32), jax.ShapeDtypeStruct((B, N, M_WIDTH), f32)],
        scratch_shapes=[pltpu.VMEM((S, M_DIM, M_DIM), f32), pltpu.VMEM((S, 1, M_DIM), f32),
                        pltpu.VMEM((1, S), f32)],
        compiler_params=pltpu.CompilerParams(dimension_semantics=("parallel", "arbitrary")),
        name="mlstm_scan",
    )(mq, mk, mv, mq, mk, mv, li_c, lf_c, li_c, lf_c, li_r, lf_r, li_r, lf_r, c0, n0, m0)


def _outproj_kernel(x_ref, att_ref, hf_ref, hb_ref, mo_ref, mnw_ref, wo_ref, g1_ref, n2w_ref, sh2_ref, sc2_ref,
                    wrh_ref, wrl_ref, x1_ref, h2_ref, aff_ref):
    parts = []
    for h in range(M_HEADS):
        hs = slice(h * M_DIM, (h + 1) * M_DIM)
        s = hf_ref[0, :, hs] + hb_ref[0, :, hs]
        ms = jnp.mean(s * s, axis=-1, keepdims=True)
        hn = s * lax.rsqrt(ms + EPS) * mnw_ref[:, hs]
        gate = 1.0 / (1.0 + jnp.exp(-mo_ref[0, :, hs].astype(f32)))
        parts.append((hn * gate).astype(bf16))
    ml = jnp.concatenate(parts, axis=-1)
    proj = _dot(att_ref[0], wo_ref[0:ATT_WIDTH, :]) + _dot(ml, wo_ref[ATT_WIDTH:, :])
    x1 = x_ref[0] + g1_ref[0] * proj
    x1_ref[0] = x1
    ms = jnp.mean(x1 * x1, axis=-1, keepdims=True)
    h2 = (x1 * lax.rsqrt(ms + EPS) * n2w_ref[...]) * (1.0 + sc2_ref[0]) + sh2_ref[0]
    h2_ref[0] = h2.astype(bf16)
    hh = h2.astype(bf16)
    hl = (h2 - hh.astype(f32)).astype(bf16)
    wrh = wrh_ref[...]
    logits = _dot_nt(wrh, hh) + _dot_nt(wrh, hl) + _dot_nt(wrl_ref[...], hh)
    mx = jnp.max(logits, axis=0, keepdims=True)
    e = jnp.exp(logits - mx)
    aff_ref[0] = e / jnp.sum(e, axis=0, keepdims=True)


def _out_projection(x, att, hf, hb, mo, mnw, w_out, g1, n2w, sh2, sc2, wr_hi, wr_lo, *, tm):
    B, N, D = x.shape
    E = wr_hi.shape[0]
    row_map = lambda b, i: (b, i, 0)
    const2 = lambda b, i: (0, 0)
    bvec = lambda b, i: (b, 0, 0)
    return pl.pallas_call(
        _outproj_kernel,
        grid=(B, N // tm),
        in_specs=[pl.BlockSpec((1, tm, D), row_map),
                  pl.BlockSpec((1, tm, ATT_WIDTH), row_map),
                  pl.BlockSpec((1, tm, M_WIDTH), row_map),
                  pl.BlockSpec((1, tm, M_WIDTH), row_map),
                  pl.BlockSpec((1, tm, M_WIDTH), row_map),
                  pl.BlockSpec((1, M_WIDTH), const2),
                  pl.BlockSpec((ATT_WIDTH + M_WIDTH, D), const2),
                  pl.BlockSpec((1, 1, D), bvec),
                  pl.BlockSpec((1, D), const2),
                  pl.BlockSpec((1, 1, D), bvec),
                  pl.BlockSpec((1, 1, D), bvec),
                  pl.BlockSpec((E, D), const2),
                  pl.BlockSpec((E, D), const2)],
        out_specs=[pl.BlockSpec((1, tm, D), row_map),
                   pl.BlockSpec((1, tm, D), row_map),
                   pl.BlockSpec((1, E, tm), lambda b, i: (b, 0, i))],
        out_shape=[jax.ShapeDtypeStruct((B, N, D), f32),
                   jax.ShapeDtypeStruct((B, N, D), bf16),
                   jax.ShapeDtypeStruct((B, E, N), f32)],
        compiler_params=pltpu.CompilerParams(
            dimension_semantics=("parallel", "parallel"), vmem_limit_bytes=VMEM_LIMIT_BYTES),
        name="out_projection",
    )(x, att, hf, hb, mo, mnw, w_out, g1, n2w, sh2, sc2, wr_hi, wr_lo)


def _ffn_kernel(xg_ref, gate_ref, wg_ref, wu_ref, wd_ref, y_ref):
    xg = xg_ref[0, 0]
    g = _dot(xg, wg_ref[0])
    u = _dot(xg, wu_ref[0])
    hmid = (_silu(g) * u).astype(bf16)
    y_ref[0, 0] = _dot(hmid, wd_ref[0]) * gate_ref[0, 0]


def _expert_ffn(xg, gate, wg, wu, wd):
    B, E, cap, D = xg.shape
    FF = wg.shape[2]
    tok = lambda e, b: (b, e, 0, 0)
    wmap = lambda e, b: (e, 0, 0)
    return pl.pallas_call(
        _ffn_kernel,
        grid=(E, B),
        in_specs=[pl.BlockSpec((1, 1, cap, D), tok),
                  pl.BlockSpec((1, 1, cap, 1), tok),
                  pl.BlockSpec((1, D, FF), wmap),
                  pl.BlockSpec((1, D, FF), wmap),
                  pl.BlockSpec((1, FF, D), wmap)],
        out_specs=pl.BlockSpec((1, 1, cap, D), tok),
        out_shape=jax.ShapeDtypeStruct((B, E, cap, D), f32),
        compiler_params=pltpu.CompilerParams(
            dimension_semantics=("parallel", "parallel"), vmem_limit_bytes=VMEM_LIMIT_BYTES),
        name="expert_ffn",
    )(xg, gate, wg, wu, wd)


def _rope_tables(n):
    rows = n // GRID_W
    row, col = jnp.meshgrid(jnp.arange(rows), jnp.arange(GRID_W), indexing='ij')
    n_freq = HEAD_DIM // 4
    freqs = ROPE_BASE ** (-jnp.arange(n_freq, dtype=f32) / n_freq)
    ang = jnp.concatenate([row.reshape(-1, 1).astype(f32) * freqs, col.reshape(-1, 1).astype(f32) * freqs], -1)
    cos, sin = jnp.cos(ang), jnp.sin(ang)
    reps = LANES // HEAD_DIM
    return (jnp.tile(jnp.concatenate([cos, cos], -1), (1, reps)),
            jnp.tile(jnp.concatenate([-sin, sin], -1), (1, reps)))


def _pack_w_in(w_in):
    D = w_in.shape[0]
    o = 0
    aq = w_in[:, o:o + ATT_WIDTH]; o += ATT_WIDTH
    ak = w_in[:, o:o + KV_WIDTH]; o += KV_WIDTH
    av = w_in[:, o:o + KV_WIDTH]; o += KV_WIDTH
    mq = w_in[:, o:o + M_WIDTH]; o += M_WIDTH
    mk = w_in[:, o:o + M_WIDTH]; o += M_WIDTH
    mv = w_in[:, o:o + M_WIDTH]; o += M_WIDTH
    mo = w_in[:, o:o + M_WIDTH]; o += M_WIDTH
    gates = w_in[:, o:o + N_GATES]

    def dup(a):
        a = a.reshape(D, N_KV_HEADS, 1, HEAD_DIM)
        return jnp.broadcast_to(a, (D, N_KV_HEADS, 2, HEAD_DIM)).reshape(D, K2_WIDTH)

    pad = jnp.zeros((D, LANES - N_GATES), w_in.dtype)
    return jnp.concatenate([aq, dup(ak), dup(av), mq, mk, mv, mo, gates, pad], -1).astype(bf16)


def _gate_layouts(g):
    H = M_HEADS
    li_c = jnp.concatenate([g[..., 0:H], g[..., 2 * H:3 * H]], -1)
    lf_c = jnp.concatenate([g[..., H:2 * H], g[..., 3 * H:4 * H]], -1)
    return li_c, lf_c, jnp.swapaxes(li_c, 1, 2), jnp.swapaxes(lf_c, 1, 2)


def _layer(x, ctx, mod_x, mod_c, norm1_w, norm2_w, w_in, b_gates, conv_qk, q_norm_w, k_norm_w, sink,
           mlstm_norm_w, w_out, w_router, w_gate, w_up, w_down):
    B, N, D = x.shape
    L = ctx.shape[1]
    tm = min(512, N)
    sh1, sc1, g1, sh2, sc2, g2 = [m[:, None, :] for m in jnp.split(mod_x, 6, -1)]
    csh1, csc1 = [jnp.broadcast_to(m[None, None, :], (B, 1, D)) for m in jnp.split(mod_c, 6, -1)[:2]]

    w_all = _pack_w_in(w_in)
    qk_w = jnp.concatenate([jnp.tile(q_norm_w, N_HEADS), jnp.tile(k_norm_w, 2 * N_KV_HEADS)])[None, :]
    bg = jnp.concatenate([b_gates, jnp.zeros((LANES - N_GATES,), f32)])[None, :]
    nw1 = norm1_w[None, :]
    cos_t, sin_t = _rope_tables(N)
    ones_t, zeros_t = jnp.ones((L, LANES), f32), jnp.zeros((L, LANES), f32)

    aq, k2, v2, mq, mk, mv, mo, gts = _in_projection(x, sh1, sc1, nw1, w_all, cos_t, sin_t, qk_w, conv_qk, bg, tm=tm)
    _, ck2, cv2, _, cmk, cmv, _, cgts = _in_projection(ctx, csh1, csc1, nw1, w_all, ones_t, zeros_t, qk_w, conv_qk,
                                                       bg, tm=L)

    att = _attention(sink, aq, k2, v2, ck2, cv2)

    cli_c, clf_c, _, _ = _gate_layouts(cgts)
    c0, n0, m0 = _ctx_states(cmk, cmv, cli_c, clf_c)
    li_c, lf_c, li_r, lf_r = _gate_layouts(gts)
    hf, hb = _mlstm(mq, mk, mv, li_c, lf_c, li_r, lf_r, c0, n0, m0)

    wr_t = w_router.T
    wr_hi = wr_t.astype(bf16)
    wr_lo = (wr_t - wr_hi.astype(f32)).astype(bf16)
    x1, h2, aff_t = _out_projection(x, att, hf, hb, mo, mlstm_norm_w[None, :], w_out.astype(bf16), g1,
                                    norm2_w[None, :], sh2, sc2, wr_hi, wr_lo, tm=tm)

    cap = CAPACITY * N // N_EXPERTS
    gate, idx = lax.top_k(aff_t, cap)
    bidx = jnp.arange(B)[:, None, None]
    xg = h2[bidx, idx]
    y = _expert_ffn(xg, gate[..., None], w_gate.astype(bf16), w_up.astype(bf16), w_down.astype(bf16))
    ffn = jnp.zeros((B, N, D), f32).at[bidx, idx].add(y)
    return x1 + g2 * ffn


def kernel(x, c, ctx, c_ctx, w_mod, b_mod, norm1_w, norm2_w, w_in, b_gates, conv_qk, q_norm_w, k_norm_w, sink,
           mlstm_norm_w, w_out, w_router, w_gate, w_up, w_down):
    depth = w_mod.shape[0]
    assert depth == 1, "only the final-layer (no context update) form of the block is implemented"
    B = x.shape[0]
    pad_rows = (-(B + 1)) % SUBLANES
    c_all = jnp.concatenate([c, c_ctx[None, :], jnp.zeros((pad_rows, c.shape[1]), c.dtype)], 0)
    mod = _modulation(c_all, w_mod[0], b_mod[0])
    return _layer(x, ctx, mod[:B], mod[B], norm1_w[0], norm2_w[0], w_in[0], b_gates[0], conv_qk[0], q_norm_w[0],
                  k_norm_w[0], sink[0], mlstm_norm_w[0], w_out[0], w_router[0], w_gate[0], w_up[0], w_down[0])
```

```python
import functools

import jax
import jax.numpy as jnp
from jax import lax
from jax.experimental import pallas as pl
from jax.experimental.pallas import tpu as pltpu

f32 = jnp.float32
bf16 = jnp.bfloat16

GRID_W = 64
N_HEADS = 8
N_KV_HEADS = 2
HEAD_DIM = 64
WINDOW = 128
BLOCK = 128
ROPE_BASE = 10000.0
M_HEADS = 4
M_DIM = 128
CHUNK = 128
CONV_W = 3
ATT_WIDTH = N_HEADS * HEAD_DIM
KV_WIDTH = N_KV_HEADS * HEAD_DIM
M_WIDTH = M_HEADS * M_DIM
N_EXPERTS = 16
CAPACITY = 2
EPS = 1e-6
NEG = -1e30

LANES = 128
SUBLANES = 8
VMEM_LIMIT_BYTES = 56 * 1024 * 1024

K2_WIDTH = 2 * KV_WIDTH
C_Q = 0
C_K = C_Q + ATT_WIDTH
C_V = C_K + K2_WIDTH
C_MQK = C_V + K2_WIDTH
C_MV = C_MQK + 2 * M_WIDTH
C_MO = C_MV + M_WIDTH
C_G = C_MO + M_WIDTH
W_COLS = C_G + LANES
N_GATES = 4 * M_HEADS


def _dot(a, b):
    return jnp.dot(a, b, preferred_element_type=f32)


def _dot_nt(a, b):
    return lax.dot_general(a, b, (((1,), (1,)), ((), ())), preferred_element_type=f32)


def _dot_tn(a, b):
    return lax.dot_general(a, b, (((0,), (0,)), ((), ())), preferred_element_type=f32)


def _split3(x):
    h = x.astype(bf16)
    r = x - h.astype(f32)
    m = r.astype(bf16)
    l = (r - m.astype(f32)).astype(bf16)
    return h, m, l


def _log_sigmoid(x):
    return jnp.minimum(x, 0.0) - jnp.log1p(jnp.exp(-jnp.abs(x)))


def _silu(x):
    return x / (1.0 + jnp.exp(-x))


def _mod_kernel(c_ref, w_ref, b_ref, o_ref):
    h, m, l = _split3(_silu(c_ref[...]))
    w = w_ref[...]
    wh = w.astype(bf16)
    wl = (w - wh.astype(f32)).astype(bf16)
    acc = _dot(h, wh) + _dot(m, wh) + _dot(h, wl) + _dot(l, wh) + _dot(m, wl)
    o_ref[...] = acc + b_ref[...]


def _modulation(c_all, w_mod, b_mod):
    rows, d = c_all.shape
    cols = w_mod.shape[1]
    tn = 512
    return pl.pallas_call(
        _mod_kernel,
        grid=(cols // tn,),
        in_specs=[pl.BlockSpec((rows, d), lambda j: (0, 0)),
                  pl.BlockSpec((d, tn), lambda j: (0, j)),
                  pl.BlockSpec((1, tn), lambda j: (0, j))],
        out_specs=pl.BlockSpec((rows, tn), lambda j: (0, j)),
        out_shape=jax.ShapeDtypeStruct((rows, cols), f32),
        name="adaln_mod",
    )(c_all, w_mod, b_mod.reshape(1, cols))


def _inproj_kernel(xp_ref, x_ref, xn_ref, shift_ref, scale_ref, nw_ref, w_ref, cos_ref, sin_ref, qkw_ref,
                   conv_ref, bg_ref,
                   aq_ref, k2_ref, v2_ref, mq_ref, mk_ref, mv_ref, mo_ref, g_ref, conv_scr, *, tm, nt):
    i = pl.program_id(1)
    nw = nw_ref[...]
    sc = 1.0 + scale_ref[0]
    sh = shift_ref[0]

    def prep(xv):
        ms = jnp.mean(xv * xv, axis=-1, keepdims=True)
        return (xv * lax.rsqrt(ms + EPS) * nw) * sc + sh

    hm = prep(x_ref[0])
    lhs = hm.astype(bf16)
    lhs_halo = jnp.concatenate([prep(xp_ref[0]), hm, prep(xn_ref[0])], axis=0).astype(bf16)

    lane = lax.broadcasted_iota(jnp.int32, (1, LANES), 1)
    lo = lane < HEAD_DIM
    first_half = (lane % HEAD_DIM) < (HEAD_DIM // 2)
    cos = cos_ref[...]
    sin = sin_ref[...]
    n_qk = (ATT_WIDTH + K2_WIDTH) // LANES
    for g in range(n_qk):
        c0 = g * LANES
        v = _dot(lhs, w_ref[:, c0:c0 + LANES])
        sq = v * v
        s_all = jnp.sum(sq, axis=-1, keepdims=True)
        s_lo = jnp.sum(jnp.where(lo, sq, 0.0), axis=-1, keepdims=True)
        ms = jnp.where(lo, s_lo, s_all - s_lo) * (1.0 / HEAD_DIM)
        nv = v * lax.rsqrt(ms + EPS) * qkw_ref[:, c0:c0 + LANES]
        swapped = jnp.where(first_half, pltpu.roll(nv, LANES - HEAD_DIM // 2, 1), pltpu.roll(nv, HEAD_DIM // 2, 1))
        r = nv * cos + swapped * sin
        if c0 < ATT_WIDTH:
            aq_ref[0, :, c0:c0 + LANES] = (r * (HEAD_DIM ** -0.5)).astype(bf16)
        else:
            k2_ref[0, :, c0 - ATT_WIDTH:c0 - ATT_WIDTH + LANES] = r.astype(bf16)

    v2_ref[0] = _dot(lhs, w_ref[:, C_V:C_V + K2_WIDTH]).astype(bf16)
    mv_ref[0] = _dot(lhs, w_ref[:, C_MV:C_MV + M_WIDTH]).astype(bf16)
    mo_ref[0] = _dot(lhs, w_ref[:, C_MO:C_MO + M_WIDTH]).astype(bf16)

    g = _dot(lhs, w_ref[:, C_G:C_G + LANES]) + bg_ref[...]
    is_forget = ((lane // M_HEADS) % 2) == 1
    g_ref[0] = jnp.where(is_forget, _log_sigmoid(g), g)[:, :N_GATES]

    conv_scr[...] = _dot(lhs_halo, w_ref[:, C_MQK:C_MQK + 2 * M_WIDTH])
    row = lax.broadcasted_iota(jnp.int32, (tm, 1), 0)
    prev = conv_scr[SUBLANES - 1:SUBLANES - 1 + tm, :]
    prev = jnp.where((row == 0) & (i == 0), 0.0, prev)
    nxt = conv_scr[SUBLANES + 1:SUBLANES + 1 + tm, :]
    nxt = jnp.where((row == tm - 1) & (i == nt - 1), 0.0, nxt)
    cur = conv_scr[SUBLANES:SUBLANES + tm, :]
    u = prev * conv_ref[0:1, :] + cur * conv_ref[1:2, :] + nxt * conv_ref[2:3, :]
    u = _silu(u)
    mq_ref[0] = (u[:, :M_WIDTH] * (M_DIM ** -0.5)).astype(bf16)
    mk_ref[0] = u[:, M_WIDTH:].astype(bf16)


def _in_projection(x, shift, scale, norm_w, w_all, cos_t, sin_t, qk_w, conv_qk, bg, *, tm):
    B, N, D = x.shape
    nt = N // tm
    hb = tm // SUBLANES
    nblk8 = N // SUBLANES
    kern = functools.partial(_inproj_kernel, tm=tm, nt=nt)
    row_map = lambda b, i: (b, i, 0)
    const2 = lambda b, i: (0, 0)
    outs = pl.pallas_call(
        kern,
        grid=(B, nt),
        in_specs=[
            pl.BlockSpec((1, SUBLANES, D), lambda b, i: (b, jnp.maximum(i * hb - 1, 0), 0)),
            pl.BlockSpec((1, tm, D), row_map),
            pl.BlockSpec((1, SUBLANES, D), lambda b, i: (b, jnp.minimum((i + 1) * hb, nblk8 - 1), 0)),
            pl.BlockSpec((1, 1, D), lambda b, i: (b, 0, 0)),
            pl.BlockSpec((1, 1, D), lambda b, i: (b, 0, 0)),
            pl.BlockSpec((1, D), const2),
            pl.BlockSpec((D, W_COLS), const2),
            pl.BlockSpec((tm, LANES), lambda b, i: (i, 0)),
            pl.BlockSpec((tm, LANES), lambda b, i: (i, 0)),
            pl.BlockSpec((1, ATT_WIDTH + K2_WIDTH), const2),
            pl.BlockSpec((CONV_W, 2 * M_WIDTH), const2),
            pl.BlockSpec((1, LANES), const2),
        ],
        out_specs=[
            pl.BlockSpec((1, tm, ATT_WIDTH), row_map),
            pl.BlockSpec((1, tm, K2_WIDTH), row_map),
            pl.BlockSpec((1, tm, K2_WIDTH), row_map),
            pl.BlockSpec((1, tm, M_WIDTH), row_map),
            pl.BlockSpec((1, tm, M_WIDTH), row_map),
            pl.BlockSpec((1, tm, M_WIDTH), row_map),
            pl.BlockSpec((1, tm, M_WIDTH), row_map),
            pl.BlockSpec((1, tm, N_GATES), row_map),
        ],
        out_shape=[
            jax.ShapeDtypeStruct((B, N, ATT_WIDTH), bf16),
            jax.ShapeDtypeStruct((B, N, K2_WIDTH), bf16),
            jax.ShapeDtypeStruct((B, N, K2_WIDTH), bf16),
            jax.ShapeDtypeStruct((B, N, M_WIDTH), bf16),
            jax.ShapeDtypeStruct((B, N, M_WIDTH), bf16),
            jax.ShapeDtypeStruct((B, N, M_WIDTH), bf16),
            jax.ShapeDtypeStruct((B, N, M_WIDTH), bf16),
            jax.ShapeDtypeStruct((B, N, N_GATES), f32),
        ],
        scratch_shapes=[pltpu.VMEM((tm + 2 * SUBLANES, 2 * M_WIDTH), f32)],
        compiler_params=pltpu.CompilerParams(
            dimension_semantics=("parallel", "parallel"), vmem_limit_bytes=VMEM_LIMIT_BYTES),
        name="in_projection",
    )(x, x, x, shift, scale, norm_w, w_all, cos_t, sin_t, qk_w, conv_qk, bg)
    return outs


def _attn_kernel(sink_ref, q_ref, kp_ref, kc_ref, kn_ref, vp_ref, vc_ref, vn_ref, ck_ref, cv_ref, o_ref, *, nb):
    n = pl.program_id(1)
    span = BLOCK + 2 * WINDOW
    grp = N_HEADS // N_KV_HEADS
    lane = lax.broadcasted_iota(jnp.int32, (1, LANES), 1)
    lo = lane < HEAD_DIM
    qi = lax.broadcasted_iota(jnp.int32, (BLOCK, span), 0)
    kj = lax.broadcasted_iota(jnp.int32, (BLOCK, span), 1)
    rel = kj - qi
    valid = (rel >= 0) & (rel <= 2 * WINDOW)
    valid = valid & ((kj >= WINDOW) | (n > 0)) & ((kj < WINDOW + BLOCK) | (n < nb - 1))
    valid2 = jnp.concatenate([valid, valid], axis=0)
    row2 = lax.broadcasted_iota(jnp.int32, (2 * BLOCK, 1), 0)
    zero = jnp.zeros((), bf16)
    for kh in range(N_KV_HEADS):
        ks = slice(kh * LANES, (kh + 1) * LANES)
        kk = jnp.concatenate([kp_ref[0, :, ks], kc_ref[0, :, ks], kn_ref[0, :, ks]], axis=0)
        vv = jnp.concatenate([vp_ref[0, :, ks], vc_ref[0, :, ks], vn_ref[0, :, ks]], axis=0)
        ckk = ck_ref[0, :, ks]
        cvv = cv_ref[0, :, ks]
        q0 = kh * grp * HEAD_DIM
        q2 = jnp.concatenate([q_ref[0, :, q0:q0 + LANES], q_ref[0, :, q0 + LANES:q0 + 2 * LANES]], axis=0)
        outs = []
        for half in range(2):
            keep = lo if half == 0 else jnp.logical_not(lo)
            kl = jnp.where(keep, kk, zero)
            ckl = jnp.where(keep, ckk, zero)
            s_loc = jnp.where(valid2, _dot_nt(q2, kl), NEG)
            s_ctx = _dot_nt(q2, ckl)
            sk = jnp.where(row2 < BLOCK, sink_ref[kh * grp + half], sink_ref[kh * grp + 2 + half])
            m = jnp.maximum(jnp.maximum(jnp.max(s_loc, axis=-1, keepdims=True),
                                        jnp.max(s_ctx, axis=-1, keepdims=True)), sk)
            p_loc = jnp.exp(s_loc - m)
            p_ctx = jnp.exp(s_ctx - m)
            den = (jnp.sum(p_loc, axis=-1, keepdims=True) + jnp.sum(p_ctx, axis=-1, keepdims=True)
                   + jnp.exp(sk - m))
            o2 = _dot(p_loc.astype(bf16), vv) + _dot(p_ctx.astype(bf16), cvv)
            outs.append(o2 / den)
        ok = jnp.where(lo, outs[0], outs[1]).astype(o_ref.dtype)
        o_ref[0, :, q0:q0 + LANES] = ok[:BLOCK]
        o_ref[0, :, q0 + LANES:q0 + 2 * LANES] = ok[BLOCK:]


def _attention(sink, q, k2, v2, ck2, cv2):
    B, N, _ = q.shape
    L = ck2.shape[1]
    nb = N // BLOCK
    kern = functools.partial(_attn_kernel, nb=nb)
    prev_map = lambda b, n, s: (b, jnp.maximum(n - 1, 0), 0)
    cur_map = lambda b, n, s: (b, n, 0)
    next_map = lambda b, n, s: (b, jnp.minimum(n + 1, nb - 1), 0)
    ctx_map = lambda b, n, s: (b, 0, 0)
    kv_spec = lambda m: pl.BlockSpec((1, BLOCK, K2_WIDTH), m)
    return pl.pallas_call(
        kern,
        grid_spec=pltpu.PrefetchScalarGridSpec(
            num_scalar_prefetch=1,
            grid=(B, nb),
            in_specs=[pl.BlockSpec((1, BLOCK, ATT_WIDTH), cur_map),
                      kv_spec(prev_map), kv_spec(cur_map), kv_spec(next_map),
                      kv_spec(prev_map), kv_spec(cur_map), kv_spec(next_map),
                      pl.BlockSpec((1, L, K2_WIDTH), ctx_map),
                      pl.BlockSpec((1, L, K2_WIDTH), ctx_map)],
            out_specs=pl.BlockSpec((1, BLOCK, ATT_WIDTH), cur_map),
        ),
        out_shape=jax.ShapeDtypeStruct((B, N, ATT_WIDTH), bf16),
        compiler_params=pltpu.CompilerParams(dimension_semantics=("parallel", "parallel")),
        name="window_attention",
    )(sink, q, k2, k2, k2, v2, v2, v2, ck2, cv2)


def _tri_cumsum_cols(tri, x):
    h, m, l = _split3(x)
    return _dot(tri, h) + _dot(tri, m) + _dot(tri, l)


def _tri_cumsum_rows(x, tri_t):
    h, m, l = _split3(x)
    return _dot(h, tri_t) + _dot(m, tri_t) + _dot(l, tri_t)


def _tri(t):
    r = lax.broadcasted_iota(jnp.int32, (t, t), 0)
    c = lax.broadcasted_iota(jnp.int32, (t, t), 1)
    return jnp.where(c <= r, 1.0, 0.0).astype(bf16), jnp.where(r <= c, 1.0, 0.0).astype(bf16)


def _ctx_state_kernel(k_ref, v_ref, li_ref, lf_ref, c_ref, n_ref, m_ref):
    L = k_ref.shape[1]
    tri, _ = _tri(L)
    li = li_ref[0]
    lf = lf_ref[0]
    lane = lax.broadcasted_iota(jnp.int32, (1, 2 * M_HEADS), 1)
    fwd = lane < M_HEADS
    cs = _tri_cumsum_cols(tri, lf)
    tot = cs[L - 1:L, :]
    b = jnp.where(fwd, cs, tot - cs + lf)
    w = tot - b + li
    m_new = jnp.maximum(tot, jnp.max(w, axis=0, keepdims=True))
    ws = jnp.exp(w - m_new)
    m_ref[0] = m_new
    for c in range(2 * M_HEADS):
        hs = slice((c % M_HEADS) * M_DIM, (c % M_HEADS + 1) * M_DIM)
        ks = k_ref[0, :, hs].astype(f32) * ws[:, c:c + 1]
        c_ref[0, c] = _dot_tn(ks.astype(bf16), v_ref[0, :, hs])
        n_ref[0, c] = jnp.sum(ks, axis=0, keepdims=True)


def _ctx_states(cmk, cmv, li_c, lf_c):
    B, L, _ = cmk.shape
    S = 2 * M_HEADS
    bmap = lambda b: (b, 0, 0)
    return pl.pallas_call(
        _ctx_state_kernel,
        grid=(B,),
        in_specs=[pl.BlockSpec((1, L, M_WIDTH), bmap), pl.BlockSpec((1, L, M_WIDTH), bmap),
                  pl.BlockSpec((1, L, S), bmap), pl.BlockSpec((1, L, S), bmap)],
        out_specs=[pl.BlockSpec((1, S, M_DIM, M_DIM), lambda b: (b, 0, 0, 0)),
                   pl.BlockSpec((1, S, 1, M_DIM), lambda b: (b, 0, 0, 0)),
                   pl.BlockSpec((1, 1, S), bmap)],
        out_shape=[jax.ShapeDtypeStruct((B, S, M_DIM, M_DIM), f32),
                   jax.ShapeDtypeStruct((B, S, 1, M_DIM), f32),
                   jax.ShapeDtypeStruct((B, 1, S), f32)],
        compiler_params=pltpu.CompilerParams(dimension_semantics=("parallel",)),
        name="mlstm_ctx_state",
    )(cmk, cmv, li_c, lf_c)


def _mlstm_kernel(qf_ref, kf_ref, vf_ref, qb_ref, kb_ref, vb_ref,
                  licf_ref, lfcf_ref, licb_ref, lfcb_ref, lirf_ref, lfrf_ref, lirb_ref, lfrb_ref,
                  c0_ref, n0_ref, m0_ref, hf_ref, hb_ref, c_scr, n_scr, m_scr):
    j = pl.program_id(1)
    S = 2 * M_HEADS
    T = CHUNK

    @pl.when(j == 0)
    def _():
        c_scr[...] = c0_ref[0]
        n_scr[...] = n0_ref[0]
        m_scr[...] = m0_ref[0]

    tri, tri_t = _tri(T)
    lane = lax.broadcasted_iota(jnp.int32, (1, S), 1)
    fwd_c = lane < M_HEADS
    srow = lax.broadcasted_iota(jnp.int32, (S, 1), 0)
    fwd_r = srow < M_HEADS
    li_c = jnp.where(fwd_c, licf_ref[0], licb_ref[0])
    lf_c = jnp.where(fwd_c, lfcf_ref[0], lfcb_ref[0])
    li_r = jnp.where(fwd_r, lirf_ref[0], lirb_ref[0])
    lf_r = jnp.where(fwd_r, lfrf_ref[0], lfrb_ref[0])

    cs_c = _tri_cumsum_cols(tri, lf_c)
    tot = cs_c[T - 1:T, :]
    b_c = jnp.where(fwd_c, cs_c, tot - cs_c + lf_c)
    cs_r = _tri_cumsum_rows(lf_r, tri_t)
    tot_r = cs_r[:, T - 1:T]
    b_r = jnp.where(fwd_r, cs_r, tot_r - cs_r + lf_r)
    e_r = li_r - b_r

    m_prev = m_scr[...]
    inter = b_c + m_prev
    w = tot - b_c + li_c
    m_new = jnp.maximum(tot + m_prev, jnp.max(w, axis=0, keepdims=True))
    a = jnp.exp(tot + m_prev - m_new)
    ws = jnp.exp(w - m_new)
    m_scr[...] = m_new

    ti = lax.broadcasted_iota(jnp.int32, (T, T), 0)
    si = lax.broadcasted_iota(jnp.int32, (T, T), 1)
    for c in range(S):
        is_fwd = c < M_HEADS
        hs = slice((c % M_HEADS) * M_DIM, (c % M_HEADS + 1) * M_DIM)
        q = (qf_ref if is_fwd else qb_ref)[0, :, hs]
        k = (kf_ref if is_fwd else kb_ref)[0, :, hs]
        v = (vf_ref if is_fwd else vb_ref)[0, :, hs]
        within = (si <= ti) if is_fwd else (si >= ti)
        d = jnp.where(within, b_c[:, c:c + 1] + e_r[c:c + 1, :], -jnp.inf)
        inter_c = inter[:, c:c + 1]
        m_t = jnp.maximum(inter_c, jnp.max(d, axis=-1, keepdims=True))
        wts = jnp.exp(d - m_t) * _dot_nt(q, k)
        decay = jnp.exp(inter_c - m_t)
        cmat = c_scr[c]
        nvec = n_scr[c]
        num = _dot(wts.astype(bf16), v) + decay * _dot(q, cmat.astype(bf16))
        den = jnp.sum(wts, axis=-1, keepdims=True) + decay * jnp.sum(q.astype(f32) * nvec, axis=-1, keepdims=True)
        h = num / jnp.maximum(jnp.abs(den), jnp.exp(-m_t))
        (hf_ref if is_fwd else hb_ref)[0, :, hs] = h
        ks = k.astype(f32) * ws[:, c:c + 1]
        a_c = a[:, c:c + 1]
        c_scr[c] = a_c * cmat + _dot_tn(ks.astype(bf16), v)
        n_scr[c] = a_c * nvec + jnp.sum(ks, axis=0, keepdims=True)


def _mlstm(mq, mk, mv, li_c, lf_c, li_r, lf_r, c0, n0, m0):
    B, N, _ = mq.shape
    nc = N // CHUNK
    S = 2 * M_HEADS
    fmap = lambda b, j: (b, j, 0)
    bmap = lambda b, j: (b, nc - 1 - j, 0)
    frow = lambda b, j: (b, 0, j)
    brow = lambda b, j: (b, 0, nc - 1 - j)
    seq = lambda m: pl.BlockSpec((1, CHUNK, M_WIDTH), m)
    col = lambda m: pl.BlockSpec((1, CHUNK, S), m)
    rowspec = lambda m: pl.BlockSpec((1, S, CHUNK), m)
    return pl.pallas_call(
        _mlstm_kernel,
        grid=(B, nc),
        in_specs=[seq(fmap), seq(fmap), seq(fmap), seq(bmap), seq(bmap), seq(bmap),
                  col(fmap), col(fmap), col(bmap), col(bmap),
                  rowspec(frow), rowspec(frow), rowspec(brow), rowspec(brow),
                  pl.BlockSpec((1, S, M_DIM, M_DIM), lambda b, j: (b, 0, 0, 0)),
                  pl.BlockSpec((1, S, 1, M_DIM), lambda b, j: (b, 0, 0, 0)),
                  pl.BlockSpec((1, 1, S), lambda b, j: (b, 0, 0))],
        out_specs=[seq(fmap), seq(bmap)],
        out_shape=[jax.ShapeDtypeStruct((B, N, M_WIDTH), f32), jax.ShapeDtypeStruct((B, N, M_WIDTH), f32)],
        scratch_shapes=[pltpu.VMEM((S, M_DIM, M_DIM), f32), pltpu.VMEM((S, 1, M_DIM), f32),
                        pltpu.VMEM((1, S), f32)],
        compiler_params=pltpu.CompilerParams(dimension_semantics=("parallel", "arbitrary")),
        name="mlstm_scan",
    )(mq, mk, mv, mq, mk, mv, li_c, lf_c, li_c, lf_c, li_r, lf_r, li_r, lf_r, c0, n0, m0)


def _outproj_kernel(x_ref, att_ref, hf_ref, hb_ref, mo_ref, mnw_ref, wo_ref, g1_ref, n2w_ref, sh2_ref, sc2_ref,
                    wrh_ref, wrl_ref, x1_ref, h2_ref, aff_ref):
    parts = []
    for h in range(M_HEADS):
        hs = slice(h * M_DIM, (h + 1) * M_DIM)
        s = hf_ref[0, :, hs] + hb_ref[0, :, hs]
        ms = jnp.mean(s * s, axis=-1, keepdims=True)
        hn = s * lax.rsqrt(ms + EPS) * mnw_ref[:, hs]
        gate = 1.0 / (1.0 + jnp.exp(-mo_ref[0, :, hs].astype(f32)))
        parts.append((hn * gate).astype(bf16))
    ml = jnp.concatenate(parts, axis=-1)
    proj = _dot(att_ref[0], wo_ref[0:ATT_WIDTH, :]) + _dot(ml, wo_ref[ATT_WIDTH:, :])
    x1 = x_ref[0] + g1_ref[0] * proj
    x1_ref[0] = x1
    ms = jnp.mean(x1 * x1, axis=-1, keepdims=True)
    h2 = (x1 * lax.rsqrt(ms + EPS) * n2w_ref[...]) * (1.0 + sc2_ref[0]) + sh2_ref[0]
    h2_ref[0] = h2.astype(bf16)
    hh = h2.astype(bf16)
    hl = (h2 - hh.astype(f32)).astype(bf16)
    wrh = wrh_ref[...]
    logits = _dot_nt(wrh, hh) + _dot_nt(wrh, hl) + _dot_nt(wrl_ref[...], hh)
    mx = jnp.max(logits, axis=0, keepdims=True)
    e = jnp.exp(logits - mx)
    aff_ref[0] = e / jnp.sum(e, axis=0, keepdims=True)


def _out_projection(x, att, hf, hb, mo, mnw, w_out, g1, n2w, sh2, sc2, wr_hi, wr_lo, *, tm):
    B, N, D = x.shape
    E = wr_hi.shape[0]
    row_map = lambda b, i: (b, i, 0)
    const2 = lambda b, i: (0, 0)
    bvec = lambda b, i: (b, 0, 0)
    return pl.pallas_call(
        _outproj_kernel,
        grid=(B, N // tm),
        in_specs=[pl.BlockSpec((1, tm, D), row_map),
                  pl.BlockSpec((1, tm, ATT_WIDTH), row_map),
                  pl.BlockSpec((1, tm, M_WIDTH), row_map),
                  pl.BlockSpec((1, tm, M_WIDTH), row_map),
                  pl.BlockSpec((1, tm, M_WIDTH), row_map),
                  pl.BlockSpec((1, M_WIDTH), const2),
                  pl.BlockSpec((ATT_WIDTH + M_WIDTH, D), const2),
                  pl.BlockSpec((1, 1, D), bvec),
                  pl.BlockSpec((1, D), const2),
                  pl.BlockSpec((1, 1, D), bvec),
                  pl.BlockSpec((1, 1, D), bvec),
                  pl.BlockSpec((E, D), const2),
                  pl.BlockSpec((E, D), const2)],
        out_specs=[pl.BlockSpec((1, tm, D), row_map),
                   pl.BlockSpec((1, tm, D), row_map),
                   pl.BlockSpec((1, E, tm), lambda b, i: (b, 0, i))],
        out_shape=[jax.ShapeDtypeStruct((B, N, D), f32),
                   jax.ShapeDtypeStruct((B, N, D), bf16),
                   jax.ShapeDtypeStruct((B, E, N), f32)],
        compiler_params=pltpu.CompilerParams(
            dimension_semantics=("parallel", "parallel"), vmem_limit_bytes=VMEM_LIMIT_BYTES),
        name="out_projection",
    )(x, att, hf, hb, mo, mnw, w_out, g1, n2w, sh2, sc2, wr_hi, wr_lo)


def _chunk_stride(cap):
    return cap + SUBLANES


def _ffn_kernel(xg_ref, gate_ref, wg_ref, wu_ref, wd_ref, y_ref, *, cap):
    xg = xg_ref[0, 0]
    g = _dot(xg, wg_ref[0])
    u = _dot(xg, wu_ref[0])
    hmid = (_silu(g) * u).astype(bf16)
    y = _dot(hmid, wd_ref[0]) * gate_ref[0, 0]
    stride = _chunk_stride(cap)
    for j in range(y.shape[1] // LANES):
        y_ref[0, 0, j * stride:j * stride + cap, :] = y[:, j * LANES:(j + 1) * LANES]
        y_ref[0, 0, j * stride + cap:(j + 1) * stride, :] = jnp.zeros((stride - cap, LANES), f32)


def _expert_ffn(xg, gate, wg, wu, wd):
    B, E, cap, D = xg.shape
    FF = wg.shape[2]
    rows = (D // LANES) * _chunk_stride(cap)
    tok = lambda e, b: (b, e, 0, 0)
    wmap = lambda e, b: (e, 0, 0)
    return pl.pallas_call(
        functools.partial(_ffn_kernel, cap=cap),
        grid=(E, B),
        in_specs=[pl.BlockSpec((1, 1, cap, D), tok),
                  pl.BlockSpec((1, 1, cap, 1), tok),
                  pl.BlockSpec((1, D, FF), wmap),
                  pl.BlockSpec((1, D, FF), wmap),
                  pl.BlockSpec((1, FF, D), wmap)],
        out_specs=pl.BlockSpec((1, 1, rows, LANES), tok),
        out_shape=jax.ShapeDtypeStruct((B, E, rows, LANES), f32),
        compiler_params=pltpu.CompilerParams(
            dimension_semantics=("parallel", "parallel"), vmem_limit_bytes=VMEM_LIMIT_BYTES),
        name="expert_ffn",
    )(xg, gate, wg, wu, wd)


SCATTER_UNROLL = 8


def _scatter_kernel(idx_ref, y_ref, acc_ref, *, cap):
    e = pl.program_id(1)

    @pl.when(e == 0)
    def _():
        acc_ref[...] = jnp.zeros_like(acc_ref)

    stride = _chunk_stride(cap)
    for i0 in range(0, cap, SCATTER_UNROLL):
        rows = [pl.multiple_of(idx_ref[0, 0, i0 + u] * SUBLANES, SUBLANES) for u in range(SCATTER_UNROLL)]
        vals = [acc_ref[0, pl.ds(rows[u], SUBLANES), :] + y_ref[0, 0, pl.ds(i0 + u, SUBLANES, stride=stride), :]
                for u in range(SCATTER_UNROLL)]
        for u in range(SCATTER_UNROLL):
            acc_ref[0, pl.ds(rows[u], SUBLANES), :] = vals[u]


def _scatter_add(idx, y_cm, n_tokens):
    B, E, cap = idx.shape
    rows = y_cm.shape[2]
    return pl.pallas_call(
        functools.partial(_scatter_kernel, cap=cap),
        grid=(B, E),
        in_specs=[pl.BlockSpec((1, 1, cap), lambda b, e: (b * E + e, 0, 0), memory_space=pltpu.SMEM),
                  pl.BlockSpec((1, 1, rows, LANES), lambda b, e: (b, e, 0, 0))],
        out_specs=pl.BlockSpec((1, n_tokens * SUBLANES, LANES), lambda b, e: (b, 0, 0)),
        out_shape=jax.ShapeDtypeStruct((B, n_tokens * SUBLANES, LANES), f32),
        compiler_params=pltpu.CompilerParams(
            dimension_semantics=("parallel", "arbitrary"), vmem_limit_bytes=VMEM_LIMIT_BYTES),
        name="expert_scatter_add",
    )(idx.reshape(B * E, 1, cap), y_cm)


def _final_kernel(x1_ref, g2_ref, acc_ref, o_ref, *, tm):
    ffn = jnp.concatenate([acc_ref[0, pl.ds(j, tm, stride=SUBLANES), :] for j in range(SUBLANES)], axis=-1)
    o_ref[0] = x1_ref[0] + g2_ref[0] * ffn


def _final_residual(x1, g2, acc, *, tm):
    B, N, D = x1.shape
    row_map = lambda b, i: (b, i, 0)
    return pl.pallas_call(
        functools.partial(_final_kernel, tm=tm),
        grid=(B, N // tm),
        in_specs=[pl.BlockSpec((1, tm, D), row_map),
                  pl.BlockSpec((1, 1, D), lambda b, i: (b, 0, 0)),
                  pl.BlockSpec((1, tm * SUBLANES, LANES), row_map)],
        out_specs=pl.BlockSpec((1, tm, D), row_map),
        out_shape=jax.ShapeDtypeStruct((B, N, D), f32),
        compiler_params=pltpu.CompilerParams(dimension_semantics=("parallel", "parallel")),
        name="final_residual",
    )(x1, g2, acc)


def _rope_tables(n):
    rows = n // GRID_W
    row, col = jnp.meshgrid(jnp.arange(rows), jnp.arange(GRID_W), indexing='ij')
    n_freq = HEAD_DIM // 4
    freqs = ROPE_BASE ** (-jnp.arange(n_freq, dtype=f32) / n_freq)
    ang = jnp.concatenate([row.reshape(-1, 1).astype(f32) * freqs, col.reshape(-1, 1).astype(f32) * freqs], -1)
    cos, sin = jnp.cos(ang), jnp.sin(ang)
    reps = LANES // HEAD_DIM
    return (jnp.tile(jnp.concatenate([cos, cos], -1), (1, reps)),
            jnp.tile(jnp.concatenate([-sin, sin], -1), (1, reps)))


def _pack_w_in(w_in):
    D = w_in.shape[0]
    o = 0
    aq = w_in[:, o:o + ATT_WIDTH]; o += ATT_WIDTH
    ak = w_in[:, o:o + KV_WIDTH]; o += KV_WIDTH
    av = w_in[:, o:o + KV_WIDTH]; o += KV_WIDTH
    mq = w_in[:, o:o + M_WIDTH]; o += M_WIDTH
    mk = w_in[:, o:o + M_WIDTH]; o += M_WIDTH
    mv = w_in[:, o:o + M_WIDTH]; o += M_WIDTH
    mo = w_in[:, o:o + M_WIDTH]; o += M_WIDTH
    gates = w_in[:, o:o + N_GATES]

    def dup(a):
        a = a.reshape(D, N_KV_HEADS, 1, HEAD_DIM)
        return jnp.broadcast_to(a, (D, N_KV_HEADS, 2, HEAD_DIM)).reshape(D, K2_WIDTH)

    pad = jnp.zeros((D, LANES - N_GATES), w_in.dtype)
    return jnp.concatenate([aq, dup(ak), dup(av), mq, mk, mv, mo, gates, pad], -1).astype(bf16)


def _gate_layouts(g):
    H = M_HEADS
    li_c = jnp.concatenate([g[..., 0:H], g[..., 2 * H:3 * H]], -1)
    lf_c = jnp.concatenate([g[..., H:2 * H], g[..., 3 * H:4 * H]], -1)
    return li_c, lf_c, jnp.swapaxes(li_c, 1, 2), jnp.swapaxes(lf_c, 1, 2)


def _layer(x, ctx, mod_x, mod_c, norm1_w, norm2_w, w_in, b_gates, conv_qk, q_norm_w, k_norm_w, sink,
           mlstm_norm_w, w_out, w_router, w_gate, w_up, w_down):
    B, N, D = x.shape
    L = ctx.shape[1]
    assert D == SUBLANES * LANES, "token rows are handled as one (8, 128) register tile"
    tm = min(512, N)
    sh1, sc1, g1, sh2, sc2, g2 = [m[:, None, :] for m in jnp.split(mod_x, 6, -1)]
    csh1, csc1 = [jnp.broadcast_to(m[None, None, :], (B, 1, D)) for m in jnp.split(mod_c, 6, -1)[:2]]

    w_all = _pack_w_in(w_in)
    qk_w = jnp.concatenate([jnp.tile(q_norm_w, N_HEADS), jnp.tile(k_norm_w, 2 * N_KV_HEADS)])[None, :]
    bg = jnp.concatenate([b_gates, jnp.zeros((LANES - N_GATES,), f32)])[None, :]
    nw1 = norm1_w[None, :]
    cos_t, sin_t = _rope_tables(N)
    ones_t, zeros_t = jnp.ones((L, LANES), f32), jnp.zeros((L, LANES), f32)

    aq, k2, v2, mq, mk, mv, mo, gts = _in_projection(x, sh1, sc1, nw1, w_all, cos_t, sin_t, qk_w, conv_qk, bg, tm=tm)
    _, ck2, cv2, _, cmk, cmv, _, cgts = _in_projection(ctx, csh1, csc1, nw1, w_all, ones_t, zeros_t, qk_w, conv_qk,
                                                       bg, tm=L)

    att = _attention(sink, aq, k2, v2, ck2, cv2)

    cli_c, clf_c, _, _ = _gate_layouts(cgts)
    c0, n0, m0 = _ctx_states(cmk, cmv, cli_c, clf_c)
    li_c, lf_c, li_r, lf_r = _gate_layouts(gts)
    hf, hb = _mlstm(mq, mk, mv, li_c, lf_c, li_r, lf_r, c0, n0, m0)

    wr_t = w_router.T
    wr_hi = wr_t.astype(bf16)
    wr_lo = (wr_t - wr_hi.astype(f32)).astype(bf16)
    x1, h2, aff_t = _out_projection(x, att, hf, hb, mo, mlstm_norm_w[None, :], w_out.astype(bf16), g1,
                                    norm2_w[None, :], sh2, sc2, wr_hi, wr_lo, tm=tm)

    cap = CAPACITY * N // N_EXPERTS
    gate, idx = lax.top_k(aff_t, cap)
    bidx = jnp.arange(B)[:, None, None]
    xg = h2[bidx, idx]
    y = _expert_ffn(xg, gate[..., None], w_gate.astype(bf16), w_up.astype(bf16), w_down.astype(bf16))
    acc = _scatter_add(idx, y, N)
    return _final_residual(x1, g2, acc, tm=tm)


def kernel(x, c, ctx, c_ctx, w_mod, b_mod, norm1_w, norm2_w, w_in, b_gates, conv_qk, q_norm_w, k_norm_w, sink,
           mlstm_norm_w, w_out, w_router, w_gate, w_up, w_down):
    depth = w_mod.shape[0]
    assert depth == 1, "only the final-layer (no context update) form of the block is implemented"
    B = x.shape[0]
    pad_rows = (-(B + 1)) % SUBLANES
    c_all = jnp.concatenate([c, c_ctx[None, :], jnp.zeros((pad_rows, c.shape[1]), c.dtype)], 0)
    mod = _modulation(c_all, w_mod[0], b_mod[0])
    return _layer(x, ctx, mod[:B], mod[B], norm1_w[0], norm2_w[0], w_in[0], b_gates[0], conv_qk[0], q_norm_w[0],
                  k_norm_w[0], sink[0], mlstm_norm_w[0], w_out[0], w_router[0], w_gate[0], w_up[0], w_down[0])
```

```python
import functools

import jax
import jax.numpy as jnp
from jax import lax
from jax.experimental import pallas as pl
from jax.experimental.pallas import tpu as pltpu

f32 = jnp.float32
bf16 = jnp.bfloat16

GRID_W = 64
N_HEADS = 8
N_KV_HEADS = 2
HEAD_DIM = 64
WINDOW = 128
BLOCK = 128
ROPE_BASE = 10000.0
M_HEADS = 4
M_DIM = 128
CHUNK = 128
CONV_W = 3
ATT_WIDTH = N_HEADS * HEAD_DIM
KV_WIDTH = N_KV_HEADS * HEAD_DIM
M_WIDTH = M_HEADS * M_DIM
N_EXPERTS = 16
CAPACITY = 2
EPS = 1e-6
NEG = -1e30

LANES = 128
SUBLANES = 8
VMEM_LIMIT_BYTES = 56 * 1024 * 1024

K2_WIDTH = 2 * KV_WIDTH
C_Q = 0
C_K = C_Q + ATT_WIDTH
C_V = C_K + K2_WIDTH
C_MQK = C_V + K2_WIDTH
C_MV = C_MQK + 2 * M_WIDTH
C_MO = C_MV + M_WIDTH
C_G = C_MO + M_WIDTH
W_COLS = C_G + LANES
N_GATES = 4 * M_HEADS


def _dot(a, b):
    return jnp.dot(a, b, preferred_element_type=f32)


def _dot_nt(a, b):
    return lax.dot_general(a, b, (((1,), (1,)), ((), ())), preferred_element_type=f32)


def _dot_tn(a, b):
    return lax.dot_general(a, b, (((0,), (0,)), ((), ())), preferred_element_type=f32)


def _split3(x):
    h = x.astype(bf16)
    r = x - h.astype(f32)
    m = r.astype(bf16)
    l = (r - m.astype(f32)).astype(bf16)
    return h, m, l


def _log_sigmoid(x):
    return jnp.minimum(x, 0.0) - jnp.log1p(jnp.exp(-jnp.abs(x)))


def _silu(x):
    return x / (1.0 + jnp.exp(-x))


def _mod_kernel(c_ref, w_ref, b_ref, o_ref):
    h, m, l = _split3(_silu(c_ref[...]))
    w = w_ref[...]
    wh = w.astype(bf16)
    wl = (w - wh.astype(f32)).astype(bf16)
    acc = _dot(h, wh) + _dot(m, wh) + _dot(h, wl) + _dot(l, wh) + _dot(m, wl)
    o_ref[...] = acc + b_ref[...]


def _modulation(c_all, w_mod, b_mod):
    rows, d = c_all.shape
    cols = w_mod.shape[1]
    tn = 512
    return pl.pallas_call(
        _mod_kernel,
        grid=(cols // tn,),
        in_specs=[pl.BlockSpec((rows, d), lambda j: (0, 0)),
                  pl.BlockSpec((d, tn), lambda j: (0, j)),
                  pl.BlockSpec((1, tn), lambda j: (0, j))],
        out_specs=pl.BlockSpec((rows, tn), lambda j: (0, j)),
        out_shape=jax.ShapeDtypeStruct((rows, cols), f32),
        name="adaln_mod",
    )(c_all, w_mod, b_mod.reshape(1, cols))


def _inproj_kernel(xp_ref, x_ref, xn_ref, shift_ref, scale_ref, nw_ref, w_ref, cos_ref, sin_ref, qkw_ref,
                   conv_ref, bg_ref,
                   aq_ref, k2_ref, v2_ref, mq_ref, mk_ref, mkt_ref, mv_ref, mo_ref, g_ref, conv_scr, *, tm, nt):
    i = pl.program_id(1)
    nw = nw_ref[...]
    sc = 1.0 + scale_ref[0]
    sh = shift_ref[0]

    def prep(xv):
        ms = jnp.mean(xv * xv, axis=-1, keepdims=True)
        return (xv * lax.rsqrt(ms + EPS) * nw) * sc + sh

    hm = prep(x_ref[0])
    lhs = hm.astype(bf16)
    lhs_halo = jnp.concatenate([prep(xp_ref[0]), hm, prep(xn_ref[0])], axis=0).astype(bf16)

    lane = lax.broadcasted_iota(jnp.int32, (1, LANES), 1)
    lo = lane < HEAD_DIM
    first_half = (lane % HEAD_DIM) < (HEAD_DIM // 2)
    cos = cos_ref[...]
    sin = sin_ref[...]
    n_qk = (ATT_WIDTH + K2_WIDTH) // LANES
    vqk = _dot(lhs, w_ref[:, C_Q:C_Q + ATT_WIDTH + K2_WIDTH])
    for g in range(n_qk):
        c0 = g * LANES
        v = vqk[:, c0:c0 + LANES]
        sq = v * v
        s_all = jnp.sum(sq, axis=-1, keepdims=True)
        s_lo = jnp.sum(jnp.where(lo, sq, 0.0), axis=-1, keepdims=True)
        ms = jnp.where(lo, s_lo, s_all - s_lo) * (1.0 / HEAD_DIM)
        nv = v * lax.rsqrt(ms + EPS) * qkw_ref[:, c0:c0 + LANES]
        swapped = jnp.where(first_half, pltpu.roll(nv, LANES - HEAD_DIM // 2, 1), pltpu.roll(nv, HEAD_DIM // 2, 1))
        r = nv * cos + swapped * sin
        if c0 < ATT_WIDTH:
            aq_ref[0, :, c0:c0 + LANES] = (r * (HEAD_DIM ** -0.5)).astype(bf16)
        else:
            k2_ref[0, :, c0 - ATT_WIDTH:c0 - ATT_WIDTH + LANES] = r.astype(bf16)

    v2_ref[0] = _dot(lhs, w_ref[:, C_V:C_V + K2_WIDTH]).astype(bf16)
    mv_ref[0] = _dot(lhs, w_ref[:, C_MV:C_MV + M_WIDTH]).astype(bf16)
    mo_ref[0] = _dot(lhs, w_ref[:, C_MO:C_MO + M_WIDTH]).astype(bf16)

    g = _dot(lhs, w_ref[:, C_G:C_G + LANES]) + bg_ref[...]
    is_forget = ((lane // M_HEADS) % 2) == 1
    g_ref[0] = jnp.where(is_forget, _log_sigmoid(g), g)[:, :N_GATES]

    conv_scr[...] = _dot(lhs_halo, w_ref[:, C_MQK:C_MQK + 2 * M_WIDTH])
    row = lax.broadcasted_iota(jnp.int32, (tm, 1), 0)
    prev = conv_scr[SUBLANES - 1:SUBLANES - 1 + tm, :]
    prev = jnp.where((row == 0) & (i == 0), 0.0, prev)
    nxt = conv_scr[SUBLANES + 1:SUBLANES + 1 + tm, :]
    nxt = jnp.where((row == tm - 1) & (i == nt - 1), 0.0, nxt)
    cur = conv_scr[SUBLANES:SUBLANES + tm, :]
    u = prev * conv_ref[0:1, :] + cur * conv_ref[1:2, :] + nxt * conv_ref[2:3, :]
    u = _silu(u)
    mq_ref[0] = (u[:, :M_WIDTH] * (M_DIM ** -0.5)).astype(bf16)
    mk = u[:, M_WIDTH:]
    mk_ref[0] = mk.astype(bf16)
    mkt_ref[0] = mk.T.astype(bf16)


def _in_projection(x, shift, scale, norm_w, w_all, cos_t, sin_t, qk_w, conv_qk, bg, *, tm):
    B, N, D = x.shape
    nt = N // tm
    hb = tm // SUBLANES
    nblk8 = N // SUBLANES
    kern = functools.partial(_inproj_kernel, tm=tm, nt=nt)
    row_map = lambda b, i: (b, i, 0)
    const2 = lambda b, i: (0, 0)
    outs = pl.pallas_call(
        kern,
        grid=(B, nt),
        in_specs=[
            pl.BlockSpec((1, SUBLANES, D), lambda b, i: (b, jnp.maximum(i * hb - 1, 0), 0)),
            pl.BlockSpec((1, tm, D), row_map),
            pl.BlockSpec((1, SUBLANES, D), lambda b, i: (b, jnp.minimum((i + 1) * hb, nblk8 - 1), 0)),
            pl.BlockSpec((1, 1, D), lambda b, i: (b, 0, 0)),
            pl.BlockSpec((1, 1, D), lambda b, i: (b, 0, 0)),
            pl.BlockSpec((1, D), const2),
            pl.BlockSpec((D, W_COLS), const2),
            pl.BlockSpec((tm, LANES), lambda b, i: (i, 0)),
            pl.BlockSpec((tm, LANES), lambda b, i: (i, 0)),
            pl.BlockSpec((1, ATT_WIDTH + K2_WIDTH), const2),
            pl.BlockSpec((CONV_W, 2 * M_WIDTH), const2),
            pl.BlockSpec((1, LANES), const2),
        ],
        out_specs=[
            pl.BlockSpec((1, tm, ATT_WIDTH), row_map),
            pl.BlockSpec((1, tm, K2_WIDTH), row_map),
            pl.BlockSpec((1, tm, K2_WIDTH), row_map),
            pl.BlockSpec((1, tm, M_WIDTH), row_map),
            pl.BlockSpec((1, tm, M_WIDTH), row_map),
            pl.BlockSpec((1, M_WIDTH, tm), lambda b, i: (b, 0, i)),
            pl.BlockSpec((1, tm, M_WIDTH), row_map),
            pl.BlockSpec((1, tm, M_WIDTH), row_map),
            pl.BlockSpec((1, tm, N_GATES), row_map),
        ],
        out_shape=[
            jax.ShapeDtypeStruct((B, N, ATT_WIDTH), bf16),
            jax.ShapeDtypeStruct((B, N, K2_WIDTH), bf16),
            jax.ShapeDtypeStruct((B, N, K2_WIDTH), bf16),
            jax.ShapeDtypeStruct((B, N, M_WIDTH), bf16),
            jax.ShapeDtypeStruct((B, N, M_WIDTH), bf16),
            jax.ShapeDtypeStruct((B, M_WIDTH, N), bf16),
            jax.ShapeDtypeStruct((B, N, M_WIDTH), bf16),
            jax.ShapeDtypeStruct((B, N, M_WIDTH), bf16),
            jax.ShapeDtypeStruct((B, N, N_GATES), f32),
        ],
        scratch_shapes=[pltpu.VMEM((tm + 2 * SUBLANES, 2 * M_WIDTH), f32)],
        compiler_params=pltpu.CompilerParams(
            dimension_semantics=("parallel", "parallel"), vmem_limit_bytes=VMEM_LIMIT_BYTES),
        name="in_projection",
    )(x, x, x, shift, scale, norm_w, w_all, cos_t, sin_t, qk_w, conv_qk, bg)
    return outs


def _attn_kernel(sink_ref, q_ref, kp_ref, kc_ref, kn_ref, vp_ref, vc_ref, vn_ref, ck_ref, cv_ref, o_ref, *, nb):
    n = pl.program_id(1)
    span = BLOCK + 2 * WINDOW
    grp = N_HEADS // N_KV_HEADS
    lane = lax.broadcasted_iota(jnp.int32, (1, LANES), 1)
    lo = lane < HEAD_DIM
    qi = lax.broadcasted_iota(jnp.int32, (BLOCK, span), 0)
    kj = lax.broadcasted_iota(jnp.int32, (BLOCK, span), 1)
    rel = kj - qi
    valid = (rel >= 0) & (rel <= 2 * WINDOW)
    valid = valid & ((kj >= WINDOW) | (n > 0)) & ((kj < WINDOW + BLOCK) | (n < nb - 1))
    valid2 = jnp.concatenate([valid, valid], axis=0)
    row2 = lax.broadcasted_iota(jnp.int32, (2 * BLOCK, 1), 0)
    zero = jnp.zeros((), bf16)
    for kh in range(N_KV_HEADS):
        ks = slice(kh * LANES, (kh + 1) * LANES)
        kk = jnp.concatenate([kp_ref[0, :, ks], kc_ref[0, :, ks], kn_ref[0, :, ks]], axis=0)
        vv = jnp.concatenate([vp_ref[0, :, ks], vc_ref[0, :, ks], vn_ref[0, :, ks]], axis=0)
        ckk = ck_ref[0, :, ks]
        cvv = cv_ref[0, :, ks]
        q0 = kh * grp * HEAD_DIM
        q2 = jnp.concatenate([q_ref[0, :, q0:q0 + LANES], q_ref[0, :, q0 + LANES:q0 + 2 * LANES]], axis=0)
        outs = []
        for half in range(2):
            keep = lo if half == 0 else jnp.logical_not(lo)
            kl = jnp.where(keep, kk, zero)
            ckl = jnp.where(keep, ckk, zero)
            s_loc = jnp.where(valid2, _dot_nt(q2, kl), NEG)
            s_ctx = _dot_nt(q2, ckl)
            sk = jnp.where(row2 < BLOCK, sink_ref[kh * grp + half], sink_ref[kh * grp + 2 + half])
            m = jnp.maximum(jnp.maximum(jnp.max(s_loc, axis=-1, keepdims=True),
                                        jnp.max(s_ctx, axis=-1, keepdims=True)), sk)
            p_loc = jnp.exp(s_loc - m)
            p_ctx = jnp.exp(s_ctx - m)
            den = (jnp.sum(p_loc, axis=-1, keepdims=True) + jnp.sum(p_ctx, axis=-1, keepdims=True)
                   + jnp.exp(sk - m))
            o2 = _dot(p_loc.astype(bf16), vv) + _dot(p_ctx.astype(bf16), cvv)
            outs.append(o2 / den)
        ok = jnp.where(lo, outs[0], outs[1]).astype(o_ref.dtype)
        o_ref[0, :, q0:q0 + LANES] = ok[:BLOCK]
        o_ref[0, :, q0 + LANES:q0 + 2 * LANES] = ok[BLOCK:]


def _attention(sink, q, k2, v2, ck2, cv2):
    B, N, _ = q.shape
    L = ck2.shape[1]
    nb = N // BLOCK
    kern = functools.partial(_attn_kernel, nb=nb)
    prev_map = lambda b, n, s: (b, jnp.maximum(n - 1, 0), 0)
    cur_map = lambda b, n, s: (b, n, 0)
    next_map = lambda b, n, s: (b, jnp.minimum(n + 1, nb - 1), 0)
    ctx_map = lambda b, n, s: (b, 0, 0)
    kv_spec = lambda m: pl.BlockSpec((1, BLOCK, K2_WIDTH), m)
    return pl.pallas_call(
        kern,
        grid_spec=pltpu.PrefetchScalarGridSpec(
            num_scalar_prefetch=1,
            grid=(B, nb),
            in_specs=[pl.BlockSpec((1, BLOCK, ATT_WIDTH), cur_map),
                      kv_spec(prev_map), kv_spec(cur_map), kv_spec(next_map),
                      kv_spec(prev_map), kv_spec(cur_map), kv_spec(next_map),
                      pl.BlockSpec((1, L, K2_WIDTH), ctx_map),
                      pl.BlockSpec((1, L, K2_WIDTH), ctx_map)],
            out_specs=pl.BlockSpec((1, BLOCK, ATT_WIDTH), cur_map),
        ),
        out_shape=jax.ShapeDtypeStruct((B, N, ATT_WIDTH), bf16),
        compiler_params=pltpu.CompilerParams(dimension_semantics=("parallel", "parallel")),
        name="window_attention",
    )(sink, q, k2, k2, k2, v2, v2, v2, ck2, cv2)


def _tri_cumsum_cols(tri, x):
    h, m, l = _split3(x)
    return _dot(tri, h) + _dot(tri, m) + _dot(tri, l)


def _tri_cumsum_rows(x, tri_t):
    h, m, l = _split3(x)
    return _dot(h, tri_t) + _dot(m, tri_t) + _dot(l, tri_t)


def _tri(t):
    r = lax.broadcasted_iota(jnp.int32, (t, t), 0)
    c = lax.broadcasted_iota(jnp.int32, (t, t), 1)
    return jnp.where(c <= r, 1.0, 0.0).astype(bf16), jnp.where(r <= c, 1.0, 0.0).astype(bf16)


def _ones_cols(v):
    return jnp.concatenate([v, jnp.ones((v.shape[0], LANES), f32).astype(bf16)], axis=1)


def _ctx_state_kernel(k_ref, v_ref, li_ref, lf_ref, c_ref, m_ref):
    L = k_ref.shape[1]
    tri, _ = _tri(L)
    li = li_ref[0]
    lf = lf_ref[0]
    lane = lax.broadcasted_iota(jnp.int32, (1, 2 * M_HEADS), 1)
    fwd = lane < M_HEADS
    cs = _tri_cumsum_cols(tri, lf)
    tot = cs[L - 1:L, :]
    b = jnp.where(fwd, cs, tot - cs + lf)
    w = tot - b + li
    m_new = jnp.maximum(tot, jnp.max(w, axis=0, keepdims=True))
    ws = jnp.exp(w - m_new)
    m_ref[0] = m_new
    for c in range(2 * M_HEADS):
        hs = slice((c % M_HEADS) * M_DIM, (c % M_HEADS + 1) * M_DIM)
        ks = k_ref[0, :, hs].astype(f32) * ws[:, c:c + 1]
        c_ref[0, c] = _dot_tn(ks.astype(bf16), _ones_cols(v_ref[0, :, hs]))


def _ctx_states(cmk, cmv, li_c, lf_c):
    B, L, _ = cmk.shape
    S = 2 * M_HEADS
    bmap = lambda b: (b, 0, 0)
    return pl.pallas_call(
        _ctx_state_kernel,
        grid=(B,),
        in_specs=[pl.BlockSpec((1, L, M_WIDTH), bmap), pl.BlockSpec((1, L, M_WIDTH), bmap),
                  pl.BlockSpec((1, L, S), bmap), pl.BlockSpec((1, L, S), bmap)],
        out_specs=[pl.BlockSpec((1, S, M_DIM, M_DIM + LANES), lambda b: (b, 0, 0, 0)),
                   pl.BlockSpec((1, 1, S), bmap)],
        out_shape=[jax.ShapeDtypeStruct((B, S, M_DIM, M_DIM + LANES), f32),
                   jax.ShapeDtypeStruct((B, 1, S), f32)],
        compiler_params=pltpu.CompilerParams(dimension_semantics=("parallel",)),
        name="mlstm_ctx_state",
    )(cmk, cmv, li_c, lf_c)


N_GATE_STATS = 6


def _gate_prep_kernel(li_ref, lf_ref, o_ref):
    S = 2 * M_HEADS
    T = CHUNK
    li = li_ref[0]
    lf = lf_ref[0]
    rows = li.shape[0]
    _, tri_t = _tri(T)
    fwd = (lax.broadcasted_iota(jnp.int32, (rows, 1), 0) % S) < M_HEADS
    lane = lax.broadcasted_iota(jnp.int32, (1, T), 1)
    cs = _tri_cumsum_rows(lf, tri_t)
    tot = cs[:, T - 1:T]
    b = jnp.where(fwd, cs, tot - cs + lf)
    e = li - b
    pm = e
    shift = 1
    while shift < T:
        from_left = jnp.where(lane >= shift, pltpu.roll(pm, shift, 1), -jnp.inf)
        from_right = jnp.where(lane < T - shift, pltpu.roll(pm, T - shift, 1), -jnp.inf)
        pm = jnp.maximum(pm, jnp.where(fwd, from_left, from_right))
        shift *= 2
    w = tot - b + li
    o_ref[0, 0] = e
    o_ref[0, 1] = pm
    o_ref[0, 2] = b
    o_ref[0, 3] = w
    o_ref[0, 4] = jnp.broadcast_to(tot, (rows, T))
    o_ref[0, 5] = jnp.broadcast_to(jnp.max(w, axis=-1, keepdims=True), (rows, T))


def _gate_prep(li_rc, lf_rc):
    B, rows, T = li_rc.shape
    bmap = lambda b: (b, 0, 0)
    return pl.pallas_call(
        _gate_prep_kernel,
        grid=(B,),
        in_specs=[pl.BlockSpec((1, rows, T), bmap), pl.BlockSpec((1, rows, T), bmap)],
        out_specs=pl.BlockSpec((1, N_GATE_STATS, rows, T), lambda b: (b, 0, 0, 0)),
        out_shape=jax.ShapeDtypeStruct((B, N_GATE_STATS, rows, T), f32),
        compiler_params=pltpu.CompilerParams(dimension_semantics=("parallel",)),
        name="mlstm_gate_prep",
    )(li_rc, lf_rc)


def _mlstm_kernel(qf_ref, ktf_ref, vf_ref, qb_ref, ktb_ref, vb_ref, gf_ref, gb_ref,
                  cn0_ref, m0_ref, hf_ref, hb_ref, cn_scr, m_scr):
    j = pl.program_id(1)
    S = 2 * M_HEADS
    T = CHUNK

    @pl.when(j == 0)
    def _():
        cn_scr[...] = cn0_ref[0]
        m_scr[...] = m0_ref[0]

    fwd_r = lax.broadcasted_iota(jnp.int32, (S, 1), 0) < M_HEADS
    stat = lambda k: jnp.where(fwd_r, gf_ref[0, k], gb_ref[0, k])
    e_r, pm, b_r, w_r, tot, w_max = [stat(k) for k in range(N_GATE_STATS)]

    m_prev = m_scr[...]
    g_hi = jnp.maximum(m_prev, pm).astype(bf16)
    g_used = g_hi.astype(f32)
    en_r = jnp.exp(-(b_r + g_used))
    en_hi = en_r.astype(bf16)
    en_lo = (en_r - en_hi.astype(f32)).astype(bf16)
    m_new = jnp.maximum(tot + m_prev, w_max)
    a_r = jnp.exp(tot + m_prev - m_new)
    ws_r = jnp.exp(w_r - m_new)
    m_scr[...] = m_new

    ti = lax.broadcasted_iota(jnp.int32, (T, T), 0)
    si = lax.broadcasted_iota(jnp.int32, (T, T), 1)
    eye = ti == si
    en_hi = en_hi.astype(f32)
    en_lo = en_lo.astype(f32)
    ones_tt = jnp.ones((T, T), f32).astype(bf16)
    zeros_tt = jnp.zeros((T, T), f32).astype(bf16)
    ones_2t = jnp.ones((2 * T, T), f32).astype(bf16)

    for c in range(S):
        is_fwd = c < M_HEADS
        hs = slice((c % M_HEADS) * M_DIM, (c % M_HEADS + 1) * M_DIM)
        row = lambda a: a[c:c + 1, :]
        diag = lambda a: jnp.where(eye, row(a), 0.0).astype(bf16)
        q = (qf_ref if is_fwd else qb_ref)[0, :, hs]
        kt = (ktf_ref if is_fwd else ktb_ref)[0, hs, :]
        v1 = _ones_cols((vf_ref if is_fwd else vb_ref)[0, :, hs])
        qk_g = _dot(jnp.concatenate([q, diag(g_used)], axis=1),
                    jnp.concatenate([jnp.concatenate([kt, zeros_tt], axis=1),
                                     jnp.concatenate([zeros_tt, ones_tt], axis=1)], axis=0))
        g_rep = qk_g[:, T:]
        en_rep = _dot(jnp.concatenate([diag(en_hi), diag(en_lo)], axis=1), ones_2t)
        within = (si <= ti) if is_fwd else (si >= ti)
        p_mat = jnp.exp(jnp.where(within, row(e_r) - g_rep, -jnp.inf))
        wts = p_mat * qk_g[:, :T]
        decay = jnp.exp(row(m_prev) - g_rep)
        cn = cn_scr[c]
        nd = _dot(wts.astype(bf16), v1)
        qc = _dot(q, cn.astype(bf16))
        num = nd[:, :M_DIM] + decay * qc[:, :M_DIM]
        den = nd[:, M_DIM:] + decay * qc[:, M_DIM:]
        h = num / jnp.maximum(jnp.abs(den), en_rep)
        (hf_ref if is_fwd else hb_ref)[0, :, hs] = h
        kst = (kt.astype(f32) * row(ws_r)).astype(bf16)
        a_c = row(a_r)
        cn_scr[c] = jnp.concatenate([a_c, a_c], axis=1) * cn + _dot(kst, v1)


def _mlstm(mq, mkt, mv, gstats, cn0, m0):
    B, N, _ = mq.shape
    nc = N // CHUNK
    S = 2 * M_HEADS
    fmap = lambda b, j: (b, j, 0)
    bmap = lambda b, j: (b, nc - 1 - j, 0)
    frow = lambda b, j: (b, 0, j)
    brow = lambda b, j: (b, 0, nc - 1 - j)
    seq = lambda m: pl.BlockSpec((1, CHUNK, M_WIDTH), m)
    seq_t = lambda m: pl.BlockSpec((1, M_WIDTH, CHUNK), m)
    gspec = lambda m: pl.BlockSpec((1, N_GATE_STATS, S, CHUNK), m)
    return pl.pallas_call(
        _mlstm_kernel,
        grid=(B, nc),
        in_specs=[seq(fmap), seq_t(frow), seq(fmap), seq(bmap), seq_t(brow), seq(bmap),
                  gspec(lambda b, j: (b, 0, j, 0)), gspec(lambda b, j: (b, 0, nc - 1 - j, 0)),
                  pl.BlockSpec((1, S, M_DIM, M_DIM + LANES), lambda b, j: (b, 0, 0, 0)),
                  pl.BlockSpec((1, S, CHUNK), lambda b, j: (b, 0, 0))],
        out_specs=[seq(fmap), seq(bmap)],
        out_shape=[jax.ShapeDtypeStruct((B, N, M_WIDTH), f32), jax.ShapeDtypeStruct((B, N, M_WIDTH), f32)],
        scratch_shapes=[pltpu.VMEM((S, M_DIM, M_DIM + LANES), f32), pltpu.VMEM((S, CHUNK), f32)],
        compiler_params=pltpu.CompilerParams(dimension_semantics=("parallel", "arbitrary")),
        name="mlstm_scan",
    )(mq, mkt, mv, mq, mkt, mv, gstats, gstats, cn0, m0)


def _outproj_kernel(x_ref, att_ref, hf_ref, hb_ref, mo_ref, mnw_ref, wo_ref, g1_ref, n2w_ref, sh2_ref, sc2_ref,
                    wrh_ref, wrl_ref, x1_ref, h2_ref, aff_ref):
    parts = []
    for h in range(M_HEADS):
        hs = slice(h * M_DIM, (h + 1) * M_DIM)
        s = hf_ref[0, :, hs] + hb_ref[0, :, hs]
        ms = jnp.mean(s * s, axis=-1, keepdims=True)
        hn = s * lax.rsqrt(ms + EPS) * mnw_ref[:, hs]
        gate = 1.0 / (1.0 + jnp.exp(-mo_ref[0, :, hs].astype(f32)))
        parts.append((hn * gate).astype(bf16))
    ml = jnp.concatenate(parts, axis=-1)
    proj = _dot(att_ref[0], wo_ref[0:ATT_WIDTH, :]) + _dot(ml, wo_ref[ATT_WIDTH:, :])
    x1 = x_ref[0] + g1_ref[0] * proj
    x1_ref[0] = x1
    ms = jnp.mean(x1 * x1, axis=-1, keepdims=True)
    h2 = (x1 * lax.rsqrt(ms + EPS) * n2w_ref[...]) * (1.0 + sc2_ref[0]) + sh2_ref[0]
    h2_ref[0] = h2.astype(bf16)
    hh = h2.astype(bf16)
    hl = (h2 - hh.astype(f32)).astype(bf16)
    wrh = wrh_ref[...]
    logits = _dot_nt(wrh, hh) + _dot_nt(wrh, hl) + _dot_nt(wrl_ref[...], hh)
    mx = jnp.max(logits, axis=0, keepdims=True)
    e = jnp.exp(logits - mx)
    aff_ref[0] = e / jnp.sum(e, axis=0, keepdims=True)


def _out_projection(x, att, hf, hb, mo, mnw, w_out, g1, n2w, sh2, sc2, wr_hi, wr_lo, *, tm):
    B, N, D = x.shape
    E = wr_hi.shape[0]
    row_map = lambda b, i: (b, i, 0)
    const2 = lambda b, i: (0, 0)
    bvec = lambda b, i: (b, 0, 0)
    return pl.pallas_call(
        _outproj_kernel,
        grid=(B, N // tm),
        in_specs=[pl.BlockSpec((1, tm, D), row_map),
                  pl.BlockSpec((1, tm, ATT_WIDTH), row_map),
                  pl.BlockSpec((1, tm, M_WIDTH), row_map),
                  pl.BlockSpec((1, tm, M_WIDTH), row_map),
                  pl.BlockSpec((1, tm, M_WIDTH), row_map),
                  pl.BlockSpec((1, M_WIDTH), const2),
                  pl.BlockSpec((ATT_WIDTH + M_WIDTH, D), const2),
                  pl.BlockSpec((1, 1, D), bvec),
                  pl.BlockSpec((1, D), const2),
                  pl.BlockSpec((1, 1, D), bvec),
                  pl.BlockSpec((1, 1, D), bvec),
                  pl.BlockSpec((E, D), const2),
                  pl.BlockSpec((E, D), const2)],
        out_specs=[pl.BlockSpec((1, tm, D), row_map),
                   pl.BlockSpec((1, tm, D), row_map),
                   pl.BlockSpec((1, E, tm), lambda b, i: (b, 0, i))],
        out_shape=[jax.ShapeDtypeStruct((B, N, D), f32),
                   jax.ShapeDtypeStruct((B, N, D), bf16),
                   jax.ShapeDtypeStruct((B, E, N), f32)],
        compiler_params=pltpu.CompilerParams(
            dimension_semantics=("parallel", "parallel"), vmem_limit_bytes=VMEM_LIMIT_BYTES),
        name="out_projection",
    )(x, att, hf, hb, mo, mnw, w_out, g1, n2w, sh2, sc2, wr_hi, wr_lo)


def _chunk_stride(cap):
    return cap + SUBLANES


def _ffn_kernel(xg_ref, gate_ref, wg_ref, wu_ref, wd_ref, y_ref, *, cap):
    xg = xg_ref[0, 0]
    g = _dot(xg, wg_ref[0])
    u = _dot(xg, wu_ref[0])
    hmid = (_silu(g) * u).astype(bf16)
    y = _dot(hmid, wd_ref[0]) * gate_ref[0, 0]
    stride = _chunk_stride(cap)
    for j in range(y.shape[1] // LANES):
        y_ref[0, 0, j * stride:j * stride + cap, :] = y[:, j * LANES:(j + 1) * LANES]
        y_ref[0, 0, j * stride + cap:(j + 1) * stride, :] = jnp.zeros((stride - cap, LANES), f32)


def _expert_ffn(xg, gate, wg, wu, wd):
    B, E, cap, D = xg.shape
    FF = wg.shape[2]
    rows = (D // LANES) * _chunk_stride(cap)
    tok = lambda e, b: (b, e, 0, 0)
    wmap = lambda e, b: (e, 0, 0)
    return pl.pallas_call(
        functools.partial(_ffn_kernel, cap=cap),
        grid=(E, B),
        in_specs=[pl.BlockSpec((1, 1, cap, D), tok),
                  pl.BlockSpec((1, 1, cap, 1), tok),
                  pl.BlockSpec((1, D, FF), wmap),
                  pl.BlockSpec((1, D, FF), wmap),
                  pl.BlockSpec((1, FF, D), wmap)],
        out_specs=pl.BlockSpec((1, 1, rows, LANES), tok),
        out_shape=jax.ShapeDtypeStruct((B, E, rows, LANES), f32),
        compiler_params=pltpu.CompilerParams(
            dimension_semantics=("parallel", "parallel"), vmem_limit_bytes=VMEM_LIMIT_BYTES),
        name="expert_ffn",
    )(xg, gate, wg, wu, wd)


SCATTER_UNROLL = 8


def _scatter_kernel(idx_ref, y_ref, acc_ref, *, cap):
    e = pl.program_id(1)

    @pl.when(e == 0)
    def _():
        acc_ref[...] = jnp.zeros_like(acc_ref)

    stride = _chunk_stride(cap)
    for i0 in range(0, cap, SCATTER_UNROLL):
        rows = [pl.multiple_of(idx_ref[0, 0, i0 + u] * SUBLANES, SUBLANES) for u in range(SCATTER_UNROLL)]
        vals = [acc_ref[0, pl.ds(rows[u], SUBLANES), :] + y_ref[0, 0, pl.ds(i0 + u, SUBLANES, stride=stride), :]
                for u in range(SCATTER_UNROLL)]
        for u in range(SCATTER_UNROLL):
            acc_ref[0, pl.ds(rows[u], SUBLANES), :] = vals[u]


def _scatter_add(idx, y_cm, n_tokens):
    B, E, cap = idx.shape
    rows = y_cm.shape[2]
    return pl.pallas_call(
        functools.partial(_scatter_kernel, cap=cap),
        grid=(B, E),
        in_specs=[pl.BlockSpec((1, 1, cap), lambda b, e: (b * E + e, 0, 0), memory_space=pltpu.SMEM),
                  pl.BlockSpec((1, 1, rows, LANES), lambda b, e: (b, e, 0, 0))],
        out_specs=pl.BlockSpec((1, n_tokens * SUBLANES, LANES), lambda b, e: (b, 0, 0)),
        out_shape=jax.ShapeDtypeStruct((B, n_tokens * SUBLANES, LANES), f32),
        compiler_params=pltpu.CompilerParams(
            dimension_semantics=("parallel", "arbitrary"), vmem_limit_bytes=VMEM_LIMIT_BYTES),
        name="expert_scatter_add",
    )(idx.reshape(B * E, 1, cap), y_cm)


def _final_kernel(x1_ref, g2_ref, acc_ref, o_ref, *, tm):
    ffn = jnp.concatenate([acc_ref[0, pl.ds(j, tm, stride=SUBLANES), :] for j in range(SUBLANES)], axis=-1)
    o_ref[0] = x1_ref[0] + g2_ref[0] * ffn


def _final_residual(x1, g2, acc, *, tm):
    B, N, D = x1.shape
    row_map = lambda b, i: (b, i, 0)
    return pl.pallas_call(
        functools.partial(_final_kernel, tm=tm),
        grid=(B, N // tm),
        in_specs=[pl.BlockSpec((1, tm, D), row_map),
                  pl.BlockSpec((1, 1, D), lambda b, i: (b, 0, 0)),
                  pl.BlockSpec((1, tm * SUBLANES, LANES), row_map)],
        out_specs=pl.BlockSpec((1, tm, D), row_map),
        out_shape=jax.ShapeDtypeStruct((B, N, D), f32),
        compiler_params=pltpu.CompilerParams(dimension_semantics=("parallel", "parallel")),
        name="final_residual",
    )(x1, g2, acc)


def _rope_tables(n):
    rows = n // GRID_W
    row, col = jnp.meshgrid(jnp.arange(rows), jnp.arange(GRID_W), indexing='ij')
    n_freq = HEAD_DIM // 4
    freqs = ROPE_BASE ** (-jnp.arange(n_freq, dtype=f32) / n_freq)
    ang = jnp.concatenate([row.reshape(-1, 1).astype(f32) * freqs, col.reshape(-1, 1).astype(f32) * freqs], -1)
    cos, sin = jnp.cos(ang), jnp.sin(ang)
    reps = LANES // HEAD_DIM
    return (jnp.tile(jnp.concatenate([cos, cos], -1), (1, reps)),
            jnp.tile(jnp.concatenate([-sin, sin], -1), (1, reps)))


def _pack_w_in(w_in):
    D = w_in.shape[0]
    o = 0
    aq = w_in[:, o:o + ATT_WIDTH]; o += ATT_WIDTH
    ak = w_in[:, o:o + KV_WIDTH]; o += KV_WIDTH
    av = w_in[:, o:o + KV_WIDTH]; o += KV_WIDTH
    mq = w_in[:, o:o + M_WIDTH]; o += M_WIDTH
    mk = w_in[:, o:o + M_WIDTH]; o += M_WIDTH
    mv = w_in[:, o:o + M_WIDTH]; o += M_WIDTH
    mo = w_in[:, o:o + M_WIDTH]; o += M_WIDTH
    gates = w_in[:, o:o + N_GATES]

    def dup(a):
        a = a.reshape(D, N_KV_HEADS, 1, HEAD_DIM)
        return jnp.broadcast_to(a, (D, N_KV_HEADS, 2, HEAD_DIM)).reshape(D, K2_WIDTH)

    pad = jnp.zeros((D, LANES - N_GATES), w_in.dtype)
    return jnp.concatenate([aq, dup(ak), dup(av), mq, mk, mv, mo, gates, pad], -1).astype(bf16)


def _gate_columns(g):
    H = M_HEADS
    li_c = jnp.concatenate([g[..., 0:H], g[..., 2 * H:3 * H]], -1)
    lf_c = jnp.concatenate([g[..., H:2 * H], g[..., 3 * H:4 * H]], -1)
    return li_c, lf_c


def _chunk_rows(a):
    B, N, S = a.shape
    return a.reshape(B, N // CHUNK, CHUNK, S).transpose(0, 1, 3, 2).reshape(B, (N // CHUNK) * S, CHUNK)


def _layer(x, ctx, mod_x, mod_c, norm1_w, norm2_w, w_in, b_gates, conv_qk, q_norm_w, k_norm_w, sink,
           mlstm_norm_w, w_out, w_router, w_gate, w_up, w_down):
    B, N, D = x.shape
    L = ctx.shape[1]
    assert D == SUBLANES * LANES, "token rows are handled as one (8, 128) register tile"
    tm = min(512, N)
    sh1, sc1, g1, sh2, sc2, g2 = [m[:, None, :] for m in jnp.split(mod_x, 6, -1)]
    csh1, csc1 = [jnp.broadcast_to(m[None, None, :], (B, 1, D)) for m in jnp.split(mod_c, 6, -1)[:2]]

    w_all = _pack_w_in(w_in)
    qk_w = jnp.concatenate([jnp.tile(q_norm_w, N_HEADS), jnp.tile(k_norm_w, 2 * N_KV_HEADS)])[None, :]
    bg = jnp.concatenate([b_gates, jnp.zeros((LANES - N_GATES,), f32)])[None, :]
    nw1 = norm1_w[None, :]
    cos_t, sin_t = _rope_tables(N)
    ones_t, zeros_t = jnp.ones((L, LANES), f32), jnp.zeros((L, LANES), f32)

    aq, k2, v2, mq, _, mkt, mv, mo, gts = _in_projection(x, sh1, sc1, nw1, w_all, cos_t, sin_t, qk_w, conv_qk, bg,
                                                         tm=tm)
    _, ck2, cv2, _, cmk, _, cmv, _, cgts = _in_projection(ctx, csh1, csc1, nw1, w_all, ones_t, zeros_t, qk_w,
                                                          conv_qk, bg, tm=L)

    att = _attention(sink, aq, k2, v2, ck2, cv2)

    cli_c, clf_c = _gate_columns(cgts)
    cn0, m0 = _ctx_states(cmk, cmv, cli_c, clf_c)
    m0 = jnp.broadcast_to(jnp.swapaxes(m0, 1, 2), (B, 2 * M_HEADS, CHUNK))
    li_c, lf_c = _gate_columns(gts)
    gstats = _gate_prep(_chunk_rows(li_c), _chunk_rows(lf_c))
    hf, hb = _mlstm(mq, mkt, mv, gstats, cn0, m0)

    wr_t = w_router.T
    wr_hi = wr_t.astype(bf16)
    wr_lo = (wr_t - wr_hi.astype(f32)).astype(bf16)
    x1, h2, aff_t = _out_projection(x, att, hf, hb, mo, mlstm_norm_w[None, :], w_out.astype(bf16), g1,
                                    norm2_w[None, :], sh2, sc2, wr_hi, wr_lo, tm=tm)

    cap = CAPACITY * N // N_EXPERTS
    gate, idx = lax.top_k(aff_t, cap)
    bidx = jnp.arange(B)[:, None, None]
    xg = h2[bidx, idx]
    y = _expert_ffn(xg, gate[..., None], w_gate.astype(bf16), w_up.astype(bf16), w_down.astype(bf16))
    acc = _scatter_add(idx, y, N)
    return _final_residual(x1, g2, acc, tm=tm)


def kernel(x, c, ctx, c_ctx, w_mod, b_mod, norm1_w, norm2_w, w_in, b_gates, conv_qk, q_norm_w, k_norm_w, sink,
           mlstm_norm_w, w_out, w_router, w_gate, w_up, w_down):
    depth = w_mod.shape[0]
    assert depth == 1, "only the final-layer (no context update) form of the block is implemented"
    B = x.shape[0]
    pad_rows = (-(B + 1)) % SUBLANES
    c_all = jnp.concatenate([c, c_ctx[None, :], jnp.zeros((pad_rows, c.shape[1]), c.dtype)], 0)
    mod = _modulation(c_all, w_mod[0], b_mod[0])
    return _layer(x, ctx, mod[:B], mod[B], norm1_w[0], norm2_w[0], w_in[0], b_gates[0], conv_qk[0], q_norm_w[0],
                  k_norm_w[0], sink[0], mlstm_norm_w[0], w_out[0], w_router[0], w_gate[0], w_up[0], w_down[0])
```

```python
import functools

import jax
import jax.numpy as jnp
from jax import lax
from jax.experimental import pallas as pl
from jax.experimental.pallas import tpu as pltpu

f32 = jnp.float32
bf16 = jnp.bfloat16

GRID_W = 64
N_HEADS = 8
N_KV_HEADS = 2
HEAD_DIM = 64
WINDOW = 128
BLOCK = 128
ROPE_BASE = 10000.0
M_HEADS = 4
M_DIM = 128
CHUNK = 128
CONV_W = 3
ATT_WIDTH = N_HEADS * HEAD_DIM
KV_WIDTH = N_KV_HEADS * HEAD_DIM
M_WIDTH = M_HEADS * M_DIM
N_EXPERTS = 16
CAPACITY = 2
EPS = 1e-6
NEG = -1e30
LOG2E = 1.4426950408889634
Q_SCALE = HEAD_DIM ** -0.5 * LOG2E

LANES = 128
SUBLANES = 8
VMEM_LIMIT_BYTES = 56 * 1024 * 1024

K2_WIDTH = 2 * KV_WIDTH
C_Q = 0
C_K = C_Q + ATT_WIDTH
C_V = C_K + K2_WIDTH
C_MQK = C_V + K2_WIDTH
C_MV = C_MQK + 2 * M_WIDTH
C_MO = C_MV + M_WIDTH
C_G = C_MO + M_WIDTH
W_COLS = C_G + LANES
N_GATES = 4 * M_HEADS


def _dot(a, b):
    return jnp.dot(a, b, preferred_element_type=f32)


def _dot_nt(a, b):
    return lax.dot_general(a, b, (((1,), (1,)), ((), ())), preferred_element_type=f32)


def _dot_tn(a, b):
    return lax.dot_general(a, b, (((0,), (0,)), ((), ())), preferred_element_type=f32)


def _ones_cols(v):
    return jnp.concatenate([v, jnp.ones((v.shape[0], LANES), f32).astype(bf16)], axis=1)


def _split3(x):
    h = x.astype(bf16)
    r = x - h.astype(f32)
    m = r.astype(bf16)
    l = (r - m.astype(f32)).astype(bf16)
    return h, m, l


def _log_sigmoid(x):
    return jnp.minimum(x, 0.0) - jnp.log1p(jnp.exp(-jnp.abs(x)))


def _silu(x):
    return x / (1.0 + jnp.exp(-x))


def _mod_kernel(c_ref, w_ref, b_ref, o_ref):
    h, m, l = _split3(_silu(c_ref[...]))
    w = w_ref[...]
    wh = w.astype(bf16)
    wl = (w - wh.astype(f32)).astype(bf16)
    acc = _dot(h, wh) + _dot(m, wh) + _dot(h, wl) + _dot(l, wh) + _dot(m, wl)
    o_ref[...] = acc + b_ref[...]


def _modulation(c_all, w_mod, b_mod):
    rows, d = c_all.shape
    cols = w_mod.shape[1]
    tn = 512
    return pl.pallas_call(
        _mod_kernel,
        grid=(cols // tn,),
        in_specs=[pl.BlockSpec((rows, d), lambda j: (0, 0)),
                  pl.BlockSpec((d, tn), lambda j: (0, j)),
                  pl.BlockSpec((1, tn), lambda j: (0, j))],
        out_specs=pl.BlockSpec((rows, tn), lambda j: (0, j)),
        out_shape=jax.ShapeDtypeStruct((rows, cols), f32),
        name="adaln_mod",
    )(c_all, w_mod, b_mod.reshape(1, cols))


def _inproj_kernel(xp_ref, x_ref, xn_ref, shift_ref, scale_ref, nw_ref, w_ref, cos_ref, sin_ref, qkw_ref,
                   conv_ref, bg_ref,
                   aq_ref, k2_ref, v2_ref, mq_ref, mk_ref, mkt_ref, mv_ref, mo_ref, g_ref, conv_scr, *, tm, nt):
    i = pl.program_id(1)
    nw = nw_ref[...]
    sc = 1.0 + scale_ref[0]
    sh = shift_ref[0]

    def prep(xv):
        ms = jnp.mean(xv * xv, axis=-1, keepdims=True)
        return (xv * lax.rsqrt(ms + EPS) * nw) * sc + sh

    hm = prep(x_ref[0])
    lhs = hm.astype(bf16)
    lhs_halo = jnp.concatenate([prep(xp_ref[0]), hm, prep(xn_ref[0])], axis=0).astype(bf16)

    lane = lax.broadcasted_iota(jnp.int32, (1, LANES), 1)
    lo = lane < HEAD_DIM
    first_half = (lane % HEAD_DIM) < (HEAD_DIM // 2)
    cos = cos_ref[...]
    sin = sin_ref[...]
    n_qk = (ATT_WIDTH + K2_WIDTH) // LANES
    vqk = _dot(lhs, w_ref[:, C_Q:C_Q + ATT_WIDTH + K2_WIDTH])
    for g in range(n_qk):
        c0 = g * LANES
        v = vqk[:, c0:c0 + LANES]
        sq = v * v
        s_all = jnp.sum(sq, axis=-1, keepdims=True)
        s_lo = jnp.sum(jnp.where(lo, sq, 0.0), axis=-1, keepdims=True)
        ms = jnp.where(lo, s_lo, s_all - s_lo) * (1.0 / HEAD_DIM)
        nv = v * lax.rsqrt(ms + EPS) * qkw_ref[:, c0:c0 + LANES]
        swapped = jnp.where(first_half, pltpu.roll(nv, LANES - HEAD_DIM // 2, 1), pltpu.roll(nv, HEAD_DIM // 2, 1))
        r = nv * cos + swapped * sin
        if c0 < ATT_WIDTH:
            aq_ref[0, :, c0:c0 + LANES] = (r * Q_SCALE).astype(bf16)
        else:
            k2_ref[0, :, c0 - ATT_WIDTH:c0 - ATT_WIDTH + LANES] = r.astype(bf16)

    v2_ref[0] = _dot(lhs, w_ref[:, C_V:C_V + K2_WIDTH]).astype(bf16)
    mv_ref[0] = _dot(lhs, w_ref[:, C_MV:C_MV + M_WIDTH]).astype(bf16)
    mo_ref[0] = _dot(lhs, w_ref[:, C_MO:C_MO + M_WIDTH]).astype(bf16)

    g = _dot(lhs, w_ref[:, C_G:C_G + LANES]) + bg_ref[...]
    is_forget = ((lane // M_HEADS) % 2) == 1
    g_ref[0] = jnp.where(is_forget, _log_sigmoid(g), g)[:, :N_GATES]

    conv_scr[...] = _dot(lhs_halo, w_ref[:, C_MQK:C_MQK + 2 * M_WIDTH])
    row = lax.broadcasted_iota(jnp.int32, (tm, 1), 0)
    prev = conv_scr[SUBLANES - 1:SUBLANES - 1 + tm, :]
    prev = jnp.where((row == 0) & (i == 0), 0.0, prev)
    nxt = conv_scr[SUBLANES + 1:SUBLANES + 1 + tm, :]
    nxt = jnp.where((row == tm - 1) & (i == nt - 1), 0.0, nxt)
    cur = conv_scr[SUBLANES:SUBLANES + tm, :]
    u = prev * conv_ref[0:1, :] + cur * conv_ref[1:2, :] + nxt * conv_ref[2:3, :]
    u = _silu(u)
    mq_ref[0] = (u[:, :M_WIDTH] * (M_DIM ** -0.5)).astype(bf16)
    mk = u[:, M_WIDTH:]
    mk_ref[0] = mk.astype(bf16)
    mkt_ref[0] = mk.T.astype(bf16)


def _in_projection(x, shift, scale, norm_w, w_all, cos_t, sin_t, qk_w, conv_qk, bg, *, tm):
    B, N, D = x.shape
    nt = N // tm
    hb = tm // SUBLANES
    nblk8 = N // SUBLANES
    kern = functools.partial(_inproj_kernel, tm=tm, nt=nt)
    row_map = lambda b, i: (b, i, 0)
    const2 = lambda b, i: (0, 0)
    outs = pl.pallas_call(
        kern,
        grid=(B, nt),
        in_specs=[
            pl.BlockSpec((1, SUBLANES, D), lambda b, i: (b, jnp.maximum(i * hb - 1, 0), 0)),
            pl.BlockSpec((1, tm, D), row_map),
            pl.BlockSpec((1, SUBLANES, D), lambda b, i: (b, jnp.minimum((i + 1) * hb, nblk8 - 1), 0)),
            pl.BlockSpec((1, 1, D), lambda b, i: (b, 0, 0)),
            pl.BlockSpec((1, 1, D), lambda b, i: (b, 0, 0)),
            pl.BlockSpec((1, D), const2),
            pl.BlockSpec((D, W_COLS), const2),
            pl.BlockSpec((tm, LANES), lambda b, i: (i, 0)),
            pl.BlockSpec((tm, LANES), lambda b, i: (i, 0)),
            pl.BlockSpec((1, ATT_WIDTH + K2_WIDTH), const2),
            pl.BlockSpec((CONV_W, 2 * M_WIDTH), const2),
            pl.BlockSpec((1, LANES), const2),
        ],
        out_specs=[
            pl.BlockSpec((1, tm, ATT_WIDTH), row_map),
            pl.BlockSpec((1, tm, K2_WIDTH), row_map),
            pl.BlockSpec((1, tm, K2_WIDTH), row_map),
            pl.BlockSpec((1, tm, M_WIDTH), row_map),
            pl.BlockSpec((1, tm, M_WIDTH), row_map),
            pl.BlockSpec((1, M_WIDTH, tm), lambda b, i: (b, 0, i)),
            pl.BlockSpec((1, tm, M_WIDTH), row_map),
            pl.BlockSpec((1, tm, M_WIDTH), row_map),
            pl.BlockSpec((1, tm, N_GATES), row_map),
        ],
        out_shape=[
            jax.ShapeDtypeStruct((B, N, ATT_WIDTH), bf16),
            jax.ShapeDtypeStruct((B, N, K2_WIDTH), bf16),
            jax.ShapeDtypeStruct((B, N, K2_WIDTH), bf16),
            jax.ShapeDtypeStruct((B, N, M_WIDTH), bf16),
            jax.ShapeDtypeStruct((B, N, M_WIDTH), bf16),
            jax.ShapeDtypeStruct((B, M_WIDTH, N), bf16),
            jax.ShapeDtypeStruct((B, N, M_WIDTH), bf16),
            jax.ShapeDtypeStruct((B, N, M_WIDTH), bf16),
            jax.ShapeDtypeStruct((B, N, N_GATES), f32),
        ],
        scratch_shapes=[pltpu.VMEM((tm + 2 * SUBLANES, 2 * M_WIDTH), f32)],
        compiler_params=pltpu.CompilerParams(
            dimension_semantics=("parallel", "parallel"), vmem_limit_bytes=VMEM_LIMIT_BYTES),
        name="in_projection",
    )(x, x, x, shift, scale, norm_w, w_all, cos_t, sin_t, qk_w, conv_qk, bg)
    return outs


def _attn_kernel(sink_ref, q_ref, kp_ref, kc_ref, kn_ref, vp_ref, vc_ref, vn_ref, ck_ref, cv_ref, o_ref, *, nb):
    n = pl.program_id(1)
    span = BLOCK + 2 * WINDOW
    grp = N_HEADS // N_KV_HEADS
    lane = lax.broadcasted_iota(jnp.int32, (1, LANES), 1)
    lo = lane < HEAD_DIM
    qi = lax.broadcasted_iota(jnp.int32, (BLOCK, span), 0)
    kj = lax.broadcasted_iota(jnp.int32, (BLOCK, span), 1)
    rel = kj - qi
    valid = (rel >= 0) & (rel <= 2 * WINDOW)
    valid = valid & ((kj >= WINDOW) | (n > 0)) & ((kj < WINDOW + BLOCK) | (n < nb - 1))
    valid2 = jnp.concatenate([valid, valid], axis=0)
    row2 = lax.broadcasted_iota(jnp.int32, (2 * BLOCK, 1), 0)
    zero = jnp.zeros((), bf16)
    for kh in range(N_KV_HEADS):
        ks = slice(kh * LANES, (kh + 1) * LANES)
        kk = jnp.concatenate([kp_ref[0, :, ks], kc_ref[0, :, ks], kn_ref[0, :, ks]], axis=0)
        vv = _ones_cols(jnp.concatenate([vp_ref[0, :, ks], vc_ref[0, :, ks], vn_ref[0, :, ks]], axis=0))
        ckk = ck_ref[0, :, ks]
        cvv = _ones_cols(cv_ref[0, :, ks])
        q0 = kh * grp * HEAD_DIM
        q2 = jnp.concatenate([q_ref[0, :, q0:q0 + LANES], q_ref[0, :, q0 + LANES:q0 + 2 * LANES]], axis=0)
        outs = []
        for half in range(2):
            keep = lo if half == 0 else jnp.logical_not(lo)
            kl = jnp.where(keep, kk, zero)
            ckl = jnp.where(keep, ckk, zero)
            s_loc = jnp.where(valid2, _dot_nt(q2, kl), NEG)
            s_ctx = _dot_nt(q2, ckl)
            sk = jnp.where(row2 < BLOCK, sink_ref[kh * grp + half], sink_ref[kh * grp + 2 + half]) * LOG2E
            m = jnp.maximum(jnp.maximum(jnp.max(s_loc, axis=-1, keepdims=True),
                                        jnp.max(s_ctx, axis=-1, keepdims=True)), sk)
            p_loc = jnp.exp2(s_loc - m)
            p_ctx = jnp.exp2(s_ctx - m)
            o2 = _dot(p_loc.astype(bf16), vv) + _dot(p_ctx.astype(bf16), cvv)
            den = o2[:, LANES:] + jnp.exp2(sk - m)
            outs.append(o2[:, :LANES] / den)
        ok = jnp.where(lo, outs[0], outs[1]).astype(o_ref.dtype)
        o_ref[0, :, q0:q0 + LANES] = ok[:BLOCK]
        o_ref[0, :, q0 + LANES:q0 + 2 * LANES] = ok[BLOCK:]


def _attention(sink, q, k2, v2, ck2, cv2):
    B, N, _ = q.shape
    L = ck2.shape[1]
    nb = N // BLOCK
    kern = functools.partial(_attn_kernel, nb=nb)
    prev_map = lambda b, n, s: (b, jnp.maximum(n - 1, 0), 0)
    cur_map = lambda b, n, s: (b, n, 0)
    next_map = lambda b, n, s: (b, jnp.minimum(n + 1, nb - 1), 0)
    ctx_map = lambda b, n, s: (b, 0, 0)
    kv_spec = lambda m: pl.BlockSpec((1, BLOCK, K2_WIDTH), m)
    return pl.pallas_call(
        kern,
        grid_spec=pltpu.PrefetchScalarGridSpec(
            num_scalar_prefetch=1,
            grid=(B, nb),
            in_specs=[pl.BlockSpec((1, BLOCK, ATT_WIDTH), cur_map),
                      kv_spec(prev_map), kv_spec(cur_map), kv_spec(next_map),
                      kv_spec(prev_map), kv_spec(cur_map), kv_spec(next_map),
                      pl.BlockSpec((1, L, K2_WIDTH), ctx_map),
                      pl.BlockSpec((1, L, K2_WIDTH), ctx_map)],
            out_specs=pl.BlockSpec((1, BLOCK, ATT_WIDTH), cur_map),
        ),
        out_shape=jax.ShapeDtypeStruct((B, N, ATT_WIDTH), bf16),
        compiler_params=pltpu.CompilerParams(dimension_semantics=("parallel", "parallel")),
        name="window_attention",
    )(sink, q, k2, k2, k2, v2, v2, v2, ck2, cv2)


def _tri_cumsum_cols(tri, x):
    h, m, l = _split3(x)
    return _dot(tri, h) + _dot(tri, m) + _dot(tri, l)


def _tri_cumsum_rows(x, tri_t):
    h, m, l = _split3(x)
    return _dot(h, tri_t) + _dot(m, tri_t) + _dot(l, tri_t)


def _tri(t):
    r = lax.broadcasted_iota(jnp.int32, (t, t), 0)
    c = lax.broadcasted_iota(jnp.int32, (t, t), 1)
    return jnp.where(c <= r, 1.0, 0.0).astype(bf16), jnp.where(r <= c, 1.0, 0.0).astype(bf16)


def _ctx_state_kernel(k_ref, v_ref, li_ref, lf_ref, c_ref, m_ref):
    L = k_ref.shape[1]
    tri, _ = _tri(L)
    li = li_ref[0]
    lf = lf_ref[0]
    lane = lax.broadcasted_iota(jnp.int32, (1, 2 * M_HEADS), 1)
    fwd = lane < M_HEADS
    cs = _tri_cumsum_cols(tri, lf)
    tot = cs[L - 1:L, :]
    b = jnp.where(fwd, cs, tot - cs + lf)
    w = tot - b + li
    m_new = jnp.maximum(tot, jnp.max(w, axis=0, keepdims=True))
    ws = jnp.exp(w - m_new)
    m_ref[0] = m_new
    for c in range(2 * M_HEADS):
        hs = slice((c % M_HEADS) * M_DIM, (c % M_HEADS + 1) * M_DIM)
        ks = k_ref[0, :, hs].astype(f32) * ws[:, c:c + 1]
        c_ref[0, c] = _dot_tn(ks.astype(bf16), _ones_cols(v_ref[0, :, hs]))


def _ctx_states(cmk, cmv, li_c, lf_c):
    B, L, _ = cmk.shape
    S = 2 * M_HEADS
    bmap = lambda b: (b, 0, 0)
    return pl.pallas_call(
        _ctx_state_kernel,
        grid=(B,),
        in_specs=[pl.BlockSpec((1, L, M_WIDTH), bmap), pl.BlockSpec((1, L, M_WIDTH), bmap),
                  pl.BlockSpec((1, L, S), bmap), pl.BlockSpec((1, L, S), bmap)],
        out_specs=[pl.BlockSpec((1, S, M_DIM, M_DIM + LANES), lambda b: (b, 0, 0, 0)),
                   pl.BlockSpec((1, 1, S), bmap)],
        out_shape=[jax.ShapeDtypeStruct((B, S, M_DIM, M_DIM + LANES), f32),
                   jax.ShapeDtypeStruct((B, 1, S), f32)],
        compiler_params=pltpu.CompilerParams(dimension_semantics=("parallel",)),
        name="mlstm_ctx_state",
    )(cmk, cmv, li_c, lf_c)


N_GATE_STATS = 6


def _gate_prep_kernel(li_ref, lf_ref, o_ref):
    S = 2 * M_HEADS
    T = CHUNK
    li = li_ref[0]
    lf = lf_ref[0]
    rows = li.shape[0]
    _, tri_t = _tri(T)
    fwd = (lax.broadcasted_iota(jnp.int32, (rows, 1), 0) % S) < M_HEADS
    lane = lax.broadcasted_iota(jnp.int32, (1, T), 1)
    cs = _tri_cumsum_rows(lf, tri_t)
    tot = cs[:, T - 1:T]
    b = jnp.where(fwd, cs, tot - cs + lf)
    e = li - b
    pm = e
    shift = 1
    while shift < T:
        from_left = jnp.where(lane >= shift, pltpu.roll(pm, shift, 1), -jnp.inf)
        from_right = jnp.where(lane < T - shift, pltpu.roll(pm, T - shift, 1), -jnp.inf)
        pm = jnp.maximum(pm, jnp.where(fwd, from_left, from_right))
        shift *= 2
    w = tot - b + li
    o_ref[0, 0] = e
    o_ref[0, 1] = pm
    o_ref[0, 2] = b
    o_ref[0, 3] = w
    o_ref[0, 4] = jnp.broadcast_to(tot, (rows, T))
    o_ref[0, 5] = jnp.broadcast_to(jnp.max(w, axis=-1, keepdims=True), (rows, T))


def _gate_prep(li_rc, lf_rc):
    B, rows, T = li_rc.shape
    bmap = lambda b: (b, 0, 0)
    return pl.pallas_call(
        _gate_prep_kernel,
        grid=(B,),
        in_specs=[pl.BlockSpec((1, rows, T), bmap), pl.BlockSpec((1, rows, T), bmap)],
        out_specs=pl.BlockSpec((1, N_GATE_STATS, rows, T), lambda b: (b, 0, 0, 0)),
        out_shape=jax.ShapeDtypeStruct((B, N_GATE_STATS, rows, T), f32),
        compiler_params=pltpu.CompilerParams(dimension_semantics=("parallel",)),
        name="mlstm_gate_prep",
    )(li_rc, lf_rc)


def _mlstm_kernel(qf_ref, ktf_ref, vf_ref, qb_ref, ktb_ref, vb_ref, gf_ref, gb_ref,
                  cn0_ref, m0_ref, hf_ref, hb_ref, cn_scr, m_scr):
    j = pl.program_id(1)
    S = 2 * M_HEADS
    T = CHUNK

    @pl.when(j == 0)
    def _():
        cn_scr[...] = cn0_ref[0]
        m_scr[...] = m0_ref[0]

    fwd_r = lax.broadcasted_iota(jnp.int32, (S, 1), 0) < M_HEADS
    stat = lambda k: jnp.where(fwd_r, gf_ref[0, k], gb_ref[0, k])
    e_r, pm, b_r, w_r, tot, w_max = [stat(k) for k in range(N_GATE_STATS)]

    m_prev = m_scr[...]
    g_hi = jnp.maximum(m_prev, pm).astype(bf16)
    g_used = g_hi.astype(f32)
    en_r = jnp.exp(-(b_r + g_used))
    en_hi = en_r.astype(bf16)
    en_lo = (en_r - en_hi.astype(f32)).astype(bf16)
    m_new = jnp.maximum(tot + m_prev, w_max)
    a_r = jnp.exp(tot + m_prev - m_new)
    ws_r = jnp.exp(w_r - m_new)
    m_scr[...] = m_new

    ti = lax.broadcasted_iota(jnp.int32, (T, T), 0)
    si = lax.broadcasted_iota(jnp.int32, (T, T), 1)
    eye = ti == si
    en_hi = en_hi.astype(f32)
    en_lo = en_lo.astype(f32)
    ones_tt = jnp.ones((T, T), f32).astype(bf16)
    zeros_tt = jnp.zeros((T, T), f32).astype(bf16)
    ones_2t = jnp.ones((2 * T, T), f32).astype(bf16)

    for c in range(S):
        is_fwd = c < M_HEADS
        hs = slice((c % M_HEADS) * M_DIM, (c % M_HEADS + 1) * M_DIM)
        row = lambda a: a[c:c + 1, :]
        diag = lambda a: jnp.where(eye, row(a), 0.0).astype(bf16)
        q = (qf_ref if is_fwd else qb_ref)[0, :, hs]
        kt = (ktf_ref if is_fwd else ktb_ref)[0, hs, :]
        v1 = _ones_cols((vf_ref if is_fwd else vb_ref)[0, :, hs])
        qk_g = _dot(jnp.concatenate([q, diag(g_used)], axis=1),
                    jnp.concatenate([jnp.concatenate([kt, zeros_tt], axis=1),
                                     jnp.concatenate([zeros_tt, ones_tt], axis=1)], axis=0))
        g_rep = qk_g[:, T:]
        en_rep = _dot(jnp.concatenate([diag(en_hi), diag(en_lo)], axis=1), ones_2t)
        within = (si <= ti) if is_fwd else (si >= ti)
        p_mat = jnp.exp(jnp.where(within, row(e_r) - g_rep, -jnp.inf))
        wts = p_mat * qk_g[:, :T]
        decay = jnp.exp(row(m_prev) - g_rep)
        cn = cn_scr[c]
        nd = _dot(wts.astype(bf16), v1)
        qc = _dot(q, cn.astype(bf16))
        num = nd[:, :M_DIM] + decay * qc[:, :M_DIM]
        den = nd[:, M_DIM:] + decay * qc[:, M_DIM:]
        h = num / jnp.maximum(jnp.abs(den), en_rep)
        (hf_ref if is_fwd else hb_ref)[0, :, hs] = h
        kst = (kt.astype(f32) * row(ws_r)).astype(bf16)
        a_c = row(a_r)
        cn_scr[c] = jnp.concatenate([a_c, a_c], axis=1) * cn + _dot(kst, v1)


def _mlstm(mq, mkt, mv, gstats, cn0, m0):
    B, N, _ = mq.shape
    nc = N // CHUNK
    S = 2 * M_HEADS
    fmap = lambda b, j: (b, j, 0)
    bmap = lambda b, j: (b, nc - 1 - j, 0)
    frow = lambda b, j: (b, 0, j)
    brow = lambda b, j: (b, 0, nc - 1 - j)
    seq = lambda m: pl.BlockSpec((1, CHUNK, M_WIDTH), m)
    seq_t = lambda m: pl.BlockSpec((1, M_WIDTH, CHUNK), m)
    gspec = lambda m: pl.BlockSpec((1, N_GATE_STATS, S, CHUNK), m)
    return pl.pallas_call(
        _mlstm_kernel,
        grid=(B, nc),
        in_specs=[seq(fmap), seq_t(frow), seq(fmap), seq(bmap), seq_t(brow), seq(bmap),
                  gspec(lambda b, j: (b, 0, j, 0)), gspec(lambda b, j: (b, 0, nc - 1 - j, 0)),
                  pl.BlockSpec((1, S, M_DIM, M_DIM + LANES), lambda b, j: (b, 0, 0, 0)),
                  pl.BlockSpec((1, S, CHUNK), lambda b, j: (b, 0, 0))],
        out_specs=[seq(fmap), seq(bmap)],
        out_shape=[jax.ShapeDtypeStruct((B, N, M_WIDTH), f32), jax.ShapeDtypeStruct((B, N, M_WIDTH), f32)],
        scratch_shapes=[pltpu.VMEM((S, M_DIM, M_DIM + LANES), f32), pltpu.VMEM((S, CHUNK), f32)],
        compiler_params=pltpu.CompilerParams(dimension_semantics=("parallel", "arbitrary")),
        name="mlstm_scan",
    )(mq, mkt, mv, mq, mkt, mv, gstats, gstats, cn0, m0)


def _outproj_kernel(x_ref, att_ref, hf_ref, hb_ref, mo_ref, mnw_ref, wo_ref, g1_ref, n2w_ref, sh2_ref, sc2_ref,
                    wrh_ref, wrl_ref, x1_ref, h2_ref, aff_ref, affr_ref, *, tm):
    parts = []
    for h in range(M_HEADS):
        hs = slice(h * M_DIM, (h + 1) * M_DIM)
        s = hf_ref[0, :, hs] + hb_ref[0, :, hs]
        ms = jnp.mean(s * s, axis=-1, keepdims=True)
        hn = s * lax.rsqrt(ms + EPS) * mnw_ref[:, hs]
        gate = 1.0 / (1.0 + jnp.exp(-mo_ref[0, :, hs].astype(f32)))
        parts.append((hn * gate).astype(bf16))
    ml = jnp.concatenate(parts, axis=-1)
    proj = _dot(att_ref[0], wo_ref[0:ATT_WIDTH, :]) + _dot(ml, wo_ref[ATT_WIDTH:, :])
    x1 = x_ref[0] + g1_ref[0] * proj
    x1_ref[0] = x1
    ms = jnp.mean(x1 * x1, axis=-1, keepdims=True)
    h2 = (x1 * lax.rsqrt(ms + EPS) * n2w_ref[...]) * (1.0 + sc2_ref[0]) + sh2_ref[0]
    for j in range(SUBLANES):
        h2_ref[0, pl.ds(j, tm, stride=SUBLANES), :] = h2[:, j * LANES:(j + 1) * LANES]
    hh = h2.astype(bf16)
    hl = (h2 - hh.astype(f32)).astype(bf16)
    wrh = wrh_ref[...]
    logits = _dot_nt(wrh, hh) + _dot_nt(wrh, hl) + _dot_nt(wrl_ref[...], hh)
    mx = jnp.max(logits, axis=0, keepdims=True)
    e = jnp.exp(logits - mx)
    aff = e / jnp.sum(e, axis=0, keepdims=True)
    aff_ref[0] = aff
    n_exp = aff.shape[0]
    affr_ref[0] = jnp.concatenate([aff.T, jnp.zeros((tm, LANES - n_exp), f32)], axis=1)


def _out_projection(x, att, hf, hb, mo, mnw, w_out, g1, n2w, sh2, sc2, wr_hi, wr_lo, *, tm):
    B, N, D = x.shape
    E = wr_hi.shape[0]
    row_map = lambda b, i: (b, i, 0)
    const2 = lambda b, i: (0, 0)
    bvec = lambda b, i: (b, 0, 0)
    return pl.pallas_call(
        functools.partial(_outproj_kernel, tm=tm),
        grid=(B, N // tm),
        in_specs=[pl.BlockSpec((1, tm, D), row_map),
                  pl.BlockSpec((1, tm, ATT_WIDTH), row_map),
                  pl.BlockSpec((1, tm, M_WIDTH), row_map),
                  pl.BlockSpec((1, tm, M_WIDTH), row_map),
                  pl.BlockSpec((1, tm, M_WIDTH), row_map),
                  pl.BlockSpec((1, M_WIDTH), const2),
                  pl.BlockSpec((ATT_WIDTH + M_WIDTH, D), const2),
                  pl.BlockSpec((1, 1, D), bvec),
                  pl.BlockSpec((1, D), const2),
                  pl.BlockSpec((1, 1, D), bvec),
                  pl.BlockSpec((1, 1, D), bvec),
                  pl.BlockSpec((E, D), const2),
                  pl.BlockSpec((E, D), const2)],
        out_specs=[pl.BlockSpec((1, tm, D), row_map),
                   pl.BlockSpec((1, tm * SUBLANES, LANES), row_map),
                   pl.BlockSpec((1, E, tm), lambda b, i: (b, 0, i)),
                   pl.BlockSpec((1, tm, LANES), row_map)],
        out_shape=[jax.ShapeDtypeStruct((B, N, D), f32),
                   jax.ShapeDtypeStruct((B, N * SUBLANES, LANES), f32),
                   jax.ShapeDtypeStruct((B, E, N), f32),
                   jax.ShapeDtypeStruct((B, N, LANES), f32)],
        compiler_params=pltpu.CompilerParams(
            dimension_semantics=("parallel", "parallel"), vmem_limit_bytes=VMEM_LIMIT_BYTES),
        name="out_projection",
    )(x, att, hf, hb, mo, mnw, w_out, g1, n2w, sh2, sc2, wr_hi, wr_lo)


def _chunk_stride(cap):
    return cap + SUBLANES


def _route_kernel(aff_ref, idx_ref, c_scr, *, cap):
    E, N = aff_ref.shape[1], aff_ref.shape[2]
    bits = pltpu.bitcast(aff_ref[0], jnp.int32)
    thr = jnp.zeros((E, 1), jnp.int32)
    for bit in range(30, -1, -1):
        cand = thr | (1 << bit)
        cnt = jnp.sum(jnp.where(bits >= cand, 1.0, 0.0), axis=-1, keepdims=True)
        thr = jnp.where(cnt >= cap, cand, thr)
    above = jnp.where(bits > thr, 1.0, 0.0)
    equal = jnp.where(bits == thr, 1.0, 0.0)
    need = cap - jnp.sum(above, axis=-1, keepdims=True)
    _, tri_t = _tri(LANES)
    both = jnp.concatenate([above, equal], axis=0).astype(bf16)
    run = jnp.zeros((2 * E, 1), f32)
    for k in range(N // LANES):
        ck = _dot(both[:, k * LANES:(k + 1) * LANES], tri_t)
        count = (ck[:E] + run[:E]) + jnp.minimum(ck[E:] + run[E:], need)
        for e in range(E):
            c_scr[e, :, k * LANES:(k + 1) * LANES] = jnp.broadcast_to(count[e:e + 1, :], (SUBLANES, LANES))
        run = run + ck[:, LANES - 1:LANES]

    slot_tile = 64
    slot = lax.broadcasted_iota(jnp.int32, (slot_tile, 1), 0).astype(f32)

    def per_expert(e, carry):
        for jt in range(cap // slot_tile):
            lim = slot + float(jt * slot_tile)
            acc = jnp.zeros((slot_tile, LANES), f32)
            for k in range(N // LANES):
                acc = acc + jnp.where(c_scr[e, 0:1, k * LANES:(k + 1) * LANES] <= lim, 1.0, 0.0)
            idx_ref[0, e, jt * slot_tile:(jt + 1) * slot_tile, :] = (
                jnp.sum(acc, axis=-1, keepdims=True).astype(jnp.int32))
        return carry

    lax.fori_loop(0, E, per_expert, 0)


def _route(aff_t, cap):
    B, E, N = aff_t.shape
    return pl.pallas_call(
        functools.partial(_route_kernel, cap=cap),
        grid=(B,),
        in_specs=[pl.BlockSpec((1, E, N), lambda b: (b, 0, 0))],
        out_specs=pl.BlockSpec((1, E, cap, 1), lambda b: (b, 0, 0, 0)),
        out_shape=jax.ShapeDtypeStruct((B, E, cap, 1), jnp.int32),
        scratch_shapes=[pltpu.VMEM((E, SUBLANES, N), f32)],
        compiler_params=pltpu.CompilerParams(dimension_semantics=("parallel",)),
        name="expert_choice_route",
    )(aff_t)


def _ffn_kernel(idx_ref, src_ref, affr_ref, wg_ref, wu_ref, wd_ref, y_ref, x_scr, g_scr, *, cap):
    e = pl.program_id(1)
    stride = _chunk_stride(cap)
    for i in range(cap):
        t = idx_ref[0, 0, i]
        x_scr[pl.ds(i, SUBLANES, stride=stride), :] = src_ref[0, pl.ds(pl.multiple_of(t * SUBLANES, SUBLANES), SUBLANES), :]
        g_scr[pl.ds(i, 1), :] = affr_ref[0, pl.ds(t, 1), :]
    xg = jnp.concatenate([x_scr[j * stride:j * stride + cap, :] for j in range(SUBLANES)], axis=1).astype(bf16)
    lane = lax.broadcasted_iota(jnp.int32, (1, LANES), 1)
    gate = jnp.sum(jnp.where(lane == e, g_scr[...], 0.0), axis=-1, keepdims=True)
    g = _dot(xg, wg_ref[0])
    u = _dot(xg, wu_ref[0])
    hmid = (_silu(g) * u).astype(bf16)
    y = _dot(hmid, wd_ref[0]) * gate
    for j in range(y.shape[1] // LANES):
        y_ref[0, 0, j * stride:j * stride + cap, :] = y[:, j * LANES:(j + 1) * LANES]
        y_ref[0, 0, j * stride + cap:(j + 1) * stride, :] = jnp.zeros((stride - cap, LANES), f32)


def _expert_ffn(idx, h2_slab, aff_rows, wg, wu, wd):
    B, E, cap = idx.shape
    D, FF = wg.shape[1], wg.shape[2]
    rows = (D // LANES) * _chunk_stride(cap)
    wmap = lambda b, e: (e, 0, 0)
    bmap = lambda b, e: (b, 0, 0)
    return pl.pallas_call(
        functools.partial(_ffn_kernel, cap=cap),
        grid=(B, E),
        in_specs=[pl.BlockSpec((1, 1, cap), lambda b, e: (b * E + e, 0, 0), memory_space=pltpu.SMEM),
                  pl.BlockSpec((1,) + h2_slab.shape[1:], bmap, pipeline_mode=pl.Buffered(1)),
                  pl.BlockSpec((1,) + aff_rows.shape[1:], bmap),
                  pl.BlockSpec((1, D, FF), wmap),
                  pl.BlockSpec((1, D, FF), wmap),
                  pl.BlockSpec((1, FF, D), wmap)],
        out_specs=pl.BlockSpec((1, 1, rows, LANES), lambda b, e: (b, e, 0, 0)),
        out_shape=jax.ShapeDtypeStruct((B, E, rows, LANES), f32),
        scratch_shapes=[pltpu.VMEM((rows, LANES), f32), pltpu.VMEM((cap, LANES), f32)],
        compiler_params=pltpu.CompilerParams(
            dimension_semantics=("parallel", "arbitrary"), vmem_limit_bytes=VMEM_LIMIT_BYTES),
        name="expert_ffn",
    )(idx.reshape(B * E, 1, cap), h2_slab, aff_rows, wg, wu, wd)


SCATTER_UNROLL = 8


def _scatter_kernel(idx_ref, y_ref, acc_ref, *, cap):
    e = pl.program_id(1)

    @pl.when(e == 0)
    def _():
        acc_ref[...] = jnp.zeros_like(acc_ref)

    stride = _chunk_stride(cap)
    for i0 in range(0, cap, SCATTER_UNROLL):
        rows = [pl.multiple_of(idx_ref[0, 0, i0 + u] * SUBLANES, SUBLANES) for u in range(SCATTER_UNROLL)]
        vals = [acc_ref[0, pl.ds(rows[u], SUBLANES), :] + y_ref[0, 0, pl.ds(i0 + u, SUBLANES, stride=stride), :]
                for u in range(SCATTER_UNROLL)]
        for u in range(SCATTER_UNROLL):
            acc_ref[0, pl.ds(rows[u], SUBLANES), :] = vals[u]


def _scatter_add(idx, y_cm, n_tokens):
    B, E, cap = idx.shape
    rows = y_cm.shape[2]
    return pl.pallas_call(
        functools.partial(_scatter_kernel, cap=cap),
        grid=(B, E),
        in_specs=[pl.BlockSpec((1, 1, cap), lambda b, e: (b * E + e, 0, 0), memory_space=pltpu.SMEM),
                  pl.BlockSpec((1, 1, rows, LANES), lambda b, e: (b, e, 0, 0))],
        out_specs=pl.BlockSpec((1, n_tokens * SUBLANES, LANES), lambda b, e: (b, 0, 0)),
        out_shape=jax.ShapeDtypeStruct((B, n_tokens * SUBLANES, LANES), f32),
        compiler_params=pltpu.CompilerParams(
            dimension_semantics=("parallel", "arbitrary"), vmem_limit_bytes=VMEM_LIMIT_BYTES),
        name="expert_scatter_add",
    )(idx.reshape(B * E, 1, cap), y_cm)


def _final_kernel(x1_ref, g2_ref, acc_ref, o_ref, *, tm):
    ffn = jnp.concatenate([acc_ref[0, pl.ds(j, tm, stride=SUBLANES), :] for j in range(SUBLANES)], axis=-1)
    o_ref[0] = x1_ref[0] + g2_ref[0] * ffn


def _final_residual(x1, g2, acc, *, tm):
    B, N, D = x1.shape
    row_map = lambda b, i: (b, i, 0)
    return pl.pallas_call(
        functools.partial(_final_kernel, tm=tm),
        grid=(B, N // tm),
        in_specs=[pl.BlockSpec((1, tm, D), row_map),
                  pl.BlockSpec((1, 1, D), lambda b, i: (b, 0, 0)),
                  pl.BlockSpec((1, tm * SUBLANES, LANES), row_map)],
        out_specs=pl.BlockSpec((1, tm, D), row_map),
        out_shape=jax.ShapeDtypeStruct((B, N, D), f32),
        compiler_params=pltpu.CompilerParams(dimension_semantics=("parallel", "parallel")),
        name="final_residual",
    )(x1, g2, acc)


def _rope_tables(n):
    rows = n // GRID_W
    row, col = jnp.meshgrid(jnp.arange(rows), jnp.arange(GRID_W), indexing='ij')
    n_freq = HEAD_DIM // 4
    freqs = ROPE_BASE ** (-jnp.arange(n_freq, dtype=f32) / n_freq)
    ang = jnp.concatenate([row.reshape(-1, 1).astype(f32) * freqs, col.reshape(-1, 1).astype(f32) * freqs], -1)
    cos, sin = jnp.cos(ang), jnp.sin(ang)
    reps = LANES // HEAD_DIM
    return (jnp.tile(jnp.concatenate([cos, cos], -1), (1, reps)),
            jnp.tile(jnp.concatenate([-sin, sin], -1), (1, reps)))


def _pack_w_in(w_in):
    D = w_in.shape[0]
    o = 0
    aq = w_in[:, o:o + ATT_WIDTH]; o += ATT_WIDTH
    ak = w_in[:, o:o + KV_WIDTH]; o += KV_WIDTH
    av = w_in[:, o:o + KV_WIDTH]; o += KV_WIDTH
    mq = w_in[:, o:o + M_WIDTH]; o += M_WIDTH
    mk = w_in[:, o:o + M_WIDTH]; o += M_WIDTH
    mv = w_in[:, o:o + M_WIDTH]; o += M_WIDTH
    mo = w_in[:, o:o + M_WIDTH]; o += M_WIDTH
    gates = w_in[:, o:o + N_GATES]

    def dup(a):
        a = a.reshape(D, N_KV_HEADS, 1, HEAD_DIM)
        return jnp.broadcast_to(a, (D, N_KV_HEADS, 2, HEAD_DIM)).reshape(D, K2_WIDTH)

    pad = jnp.zeros((D, LANES - N_GATES), w_in.dtype)
    return jnp.concatenate([aq, dup(ak), dup(av), mq, mk, mv, mo, gates, pad], -1).astype(bf16)


def _gate_columns(g):
    H = M_HEADS
    li_c = jnp.concatenate([g[..., 0:H], g[..., 2 * H:3 * H]], -1)
    lf_c = jnp.concatenate([g[..., H:2 * H], g[..., 3 * H:4 * H]], -1)
    return li_c, lf_c


def _chunk_rows(a):
    B, N, S = a.shape
    return a.reshape(B, N // CHUNK, CHUNK, S).transpose(0, 1, 3, 2).reshape(B, (N // CHUNK) * S, CHUNK)


def _layer(x, ctx, mod_x, mod_c, norm1_w, norm2_w, w_in, b_gates, conv_qk, q_norm_w, k_norm_w, sink,
           mlstm_norm_w, w_out, w_router, w_gate, w_up, w_down):
    B, N, D = x.shape
    L = ctx.shape[1]
    assert D == SUBLANES * LANES, "token rows are handled as one (8, 128) register tile"
    tm = min(512, N)
    sh1, sc1, g1, sh2, sc2, g2 = [m[:, None, :] for m in jnp.split(mod_x, 6, -1)]
    csh1, csc1 = [jnp.broadcast_to(m[None, None, :], (B, 1, D)) for m in jnp.split(mod_c, 6, -1)[:2]]

    w_all = _pack_w_in(w_in)
    qk_w = jnp.concatenate([jnp.tile(q_norm_w, N_HEADS), jnp.tile(k_norm_w, 2 * N_KV_HEADS)])[None, :]
    bg = jnp.concatenate([b_gates, jnp.zeros((LANES - N_GATES,), f32)])[None, :]
    nw1 = norm1_w[None, :]
    cos_t, sin_t = _rope_tables(N)
    ones_t, zeros_t = jnp.ones((L, LANES), f32), jnp.zeros((L, LANES), f32)

    aq, k2, v2, mq, _, mkt, mv, mo, gts = _in_projection(x, sh1, sc1, nw1, w_all, cos_t, sin_t, qk_w, conv_qk, bg,
                                                         tm=tm)
    _, ck2, cv2, _, cmk, _, cmv, _, cgts = _in_projection(ctx, csh1, csc1, nw1, w_all, ones_t, zeros_t, qk_w,
                                                          conv_qk, bg, tm=L)

    att = _attention(sink, aq, k2, v2, ck2, cv2)

    cli_c, clf_c = _gate_columns(cgts)
    cn0, m0 = _ctx_states(cmk, cmv, cli_c, clf_c)
    m0 = jnp.broadcast_to(jnp.swapaxes(m0, 1, 2), (B, 2 * M_HEADS, CHUNK))
    li_c, lf_c = _gate_columns(gts)
    gstats = _gate_prep(_chunk_rows(li_c), _chunk_rows(lf_c))
    hf, hb = _mlstm(mq, mkt, mv, gstats, cn0, m0)

    wr_t = w_router.T
    wr_hi = wr_t.astype(bf16)
    wr_lo = (wr_t - wr_hi.astype(f32)).astype(bf16)
    x1, h2_slab, aff_t, aff_rows = _out_projection(x, att, hf, hb, mo, mlstm_norm_w[None, :], w_out.astype(bf16),
                                                   g1, norm2_w[None, :], sh2, sc2, wr_hi, wr_lo, tm=tm)

    cap = CAPACITY * N // N_EXPERTS
    idx = _route(aff_t, cap)[..., 0]
    y = _expert_ffn(idx, h2_slab, aff_rows, w_gate.astype(bf16), w_up.astype(bf16), w_down.astype(bf16))
    acc = _scatter_add(idx, y, N)
    return _final_residual(x1, g2, acc, tm=tm)


def kernel(x, c, ctx, c_ctx, w_mod, b_mod, norm1_w, norm2_w, w_in, b_gates, conv_qk, q_norm_w, k_norm_w, sink,
           mlstm_norm_w, w_out, w_router, w_gate, w_up, w_down):
    depth = w_mod.shape[0]
    assert depth == 1, "only the final-layer (no context update) form of the block is implemented"
    B = x.shape[0]
    pad_rows = (-(B + 1)) % SUBLANES
    c_all = jnp.concatenate([c, c_ctx[None, :], jnp.zeros((pad_rows, c.shape[1]), c.dtype)], 0)
    mod = _modulation(c_all, w_mod[0], b_mod[0])
    return _layer(x, ctx, mod[:B], mod[B], norm1_w[0], norm2_w[0], w_in[0], b_gates[0], conv_qk[0], q_norm_w[0],
                  k_norm_w[0], sink[0], mlstm_norm_w[0], w_out[0], w_router[0], w_gate[0], w_up[0], w_down[0])
```

```python
import functools

import jax
import jax.numpy as jnp
from jax import lax
from jax.experimental import pallas as pl
from jax.experimental.pallas import tpu as pltpu

f32 = jnp.float32
bf16 = jnp.bfloat16

GRID_W = 64
N_HEADS = 8
N_KV_HEADS = 2
HEAD_DIM = 64
WINDOW = 128
BLOCK = 128
ROPE_BASE = 10000.0
M_HEADS = 4
M_DIM = 128
CHUNK = 128
CONV_W = 3
ATT_WIDTH = N_HEADS * HEAD_DIM
KV_WIDTH = N_KV_HEADS * HEAD_DIM
M_WIDTH = M_HEADS * M_DIM
N_EXPERTS = 16
CAPACITY = 2
EPS = 1e-6
NEG = -1e30
LOG2E = 1.4426950408889634
Q_SCALE = HEAD_DIM ** -0.5 * LOG2E

LANES = 128
SUBLANES = 8
VMEM_LIMIT_BYTES = 56 * 1024 * 1024

K2_WIDTH = 2 * KV_WIDTH
C_Q = 0
C_K = C_Q + ATT_WIDTH
C_V = C_K + K2_WIDTH
C_MQK = C_V + K2_WIDTH
C_MV = C_MQK + 2 * M_WIDTH
C_MO = C_MV + M_WIDTH
C_G = C_MO + M_WIDTH
W_COLS = C_G + LANES
N_GATES = 4 * M_HEADS


def _dot(a, b):
    return jnp.dot(a, b, preferred_element_type=f32)


def _dot_nt(a, b):
    return lax.dot_general(a, b, (((1,), (1,)), ((), ())), preferred_element_type=f32)


def _dot_tn(a, b):
    return lax.dot_general(a, b, (((0,), (0,)), ((), ())), preferred_element_type=f32)


def _ones_cols(v):
    return jnp.concatenate([v, jnp.ones((v.shape[0], LANES), f32).astype(bf16)], axis=1)


def _split3(x):
    h = x.astype(bf16)
    r = x - h.astype(f32)
    m = r.astype(bf16)
    l = (r - m.astype(f32)).astype(bf16)
    return h, m, l


def _log_sigmoid(x):
    return jnp.minimum(x, 0.0) - jnp.log1p(jnp.exp(-jnp.abs(x)))


def _silu(x):
    return x / (1.0 + jnp.exp(-x))


def _mod_kernel(c_ref, w_ref, b_ref, o_ref):
    h, m, l = _split3(_silu(c_ref[...]))
    w = w_ref[...]
    wh = w.astype(bf16)
    wl = (w - wh.astype(f32)).astype(bf16)
    acc = _dot(h, wh) + _dot(m, wh) + _dot(h, wl) + _dot(l, wh) + _dot(m, wl)
    o_ref[...] = acc + b_ref[...]


def _modulation(c_all, w_mod, b_mod):
    rows, d = c_all.shape
    cols = w_mod.shape[1]
    tn = 512
    return pl.pallas_call(
        _mod_kernel,
        grid=(cols // tn,),
        in_specs=[pl.BlockSpec((rows, d), lambda j: (0, 0)),
                  pl.BlockSpec((d, tn), lambda j: (0, j)),
                  pl.BlockSpec((1, tn), lambda j: (0, j))],
        out_specs=pl.BlockSpec((rows, tn), lambda j: (0, j)),
        out_shape=jax.ShapeDtypeStruct((rows, cols), f32),
        name="adaln_mod",
    )(c_all, w_mod, b_mod.reshape(1, cols))


def _inproj_kernel(xp_ref, x_ref, xn_ref, shift_ref, scale_ref, nw_ref, w_ref, cos_ref, sin_ref, qkw_ref,
                   conv_ref, bg_ref,
                   aq_ref, k2_ref, v2_ref, mq_ref, mk_ref, mkt_ref, mv_ref, mo_ref, g_ref, conv_scr, *, tm, nt):
    i = pl.program_id(1)
    nw = nw_ref[...]
    sc = 1.0 + scale_ref[0]
    sh = shift_ref[0]

    def prep(xv):
        ms = jnp.mean(xv * xv, axis=-1, keepdims=True)
        return (xv * lax.rsqrt(ms + EPS) * nw) * sc + sh

    hm = prep(x_ref[0])
    lhs = hm.astype(bf16)
    lhs_halo = jnp.concatenate([prep(xp_ref[0]), hm, prep(xn_ref[0])], axis=0).astype(bf16)

    lane = lax.broadcasted_iota(jnp.int32, (1, LANES), 1)
    lo = lane < HEAD_DIM
    first_half = (lane % HEAD_DIM) < (HEAD_DIM // 2)
    cos = cos_ref[...]
    sin = sin_ref[...]
    n_qk = (ATT_WIDTH + K2_WIDTH) // LANES
    vqk = _dot(lhs, w_ref[:, C_Q:C_Q + ATT_WIDTH + K2_WIDTH])
    for g in range(n_qk):
        c0 = g * LANES
        v = vqk[:, c0:c0 + LANES]
        sq = v * v
        s_all = jnp.sum(sq, axis=-1, keepdims=True)
        s_lo = jnp.sum(jnp.where(lo, sq, 0.0), axis=-1, keepdims=True)
        ms = jnp.where(lo, s_lo, s_all - s_lo) * (1.0 / HEAD_DIM)
        nv = v * lax.rsqrt(ms + EPS) * qkw_ref[:, c0:c0 + LANES]
        swapped = jnp.where(first_half, pltpu.roll(nv, LANES - HEAD_DIM // 2, 1), pltpu.roll(nv, HEAD_DIM // 2, 1))
        r = nv * cos + swapped * sin
        if c0 < ATT_WIDTH:
            aq_ref[0, :, c0:c0 + LANES] = (r * Q_SCALE).astype(bf16)
        else:
            k2_ref[0, :, c0 - ATT_WIDTH:c0 - ATT_WIDTH + LANES] = r.astype(bf16)

    v2_ref[0] = _dot(lhs, w_ref[:, C_V:C_V + K2_WIDTH]).astype(bf16)
    mv_ref[0] = _dot(lhs, w_ref[:, C_MV:C_MV + M_WIDTH]).astype(bf16)
    mo_ref[0] = _dot(lhs, w_ref[:, C_MO:C_MO + M_WIDTH]).astype(bf16)

    g = _dot(lhs, w_ref[:, C_G:C_G + LANES]) + bg_ref[...]
    is_forget = ((lane // M_HEADS) % 2) == 1
    g_ref[0] = jnp.where(is_forget, _log_sigmoid(g), g)[:, :N_GATES]

    conv_scr[...] = _dot(lhs_halo, w_ref[:, C_MQK:C_MQK + 2 * M_WIDTH])
    row = lax.broadcasted_iota(jnp.int32, (tm, 1), 0)
    prev = conv_scr[SUBLANES - 1:SUBLANES - 1 + tm, :]
    prev = jnp.where((row == 0) & (i == 0), 0.0, prev)
    nxt = conv_scr[SUBLANES + 1:SUBLANES + 1 + tm, :]
    nxt = jnp.where((row == tm - 1) & (i == nt - 1), 0.0, nxt)
    cur = conv_scr[SUBLANES:SUBLANES + tm, :]
    u = prev * conv_ref[0:1, :] + cur * conv_ref[1:2, :] + nxt * conv_ref[2:3, :]
    u = _silu(u)
    mq_ref[0] = (u[:, :M_WIDTH] * (M_DIM ** -0.5)).astype(bf16)
    mk = u[:, M_WIDTH:]
    mk_ref[0] = mk.astype(bf16)
    mkt_ref[0] = mk.T.astype(bf16)


def _in_projection(x, shift, scale, norm_w, w_all, cos_t, sin_t, qk_w, conv_qk, bg, *, tm):
    B, N, D = x.shape
    nt = N // tm
    hb = tm // SUBLANES
    nblk8 = N // SUBLANES
    kern = functools.partial(_inproj_kernel, tm=tm, nt=nt)
    row_map = lambda b, i: (b, i, 0)
    const2 = lambda b, i: (0, 0)
    outs = pl.pallas_call(
        kern,
        grid=(B, nt),
        in_specs=[
            pl.BlockSpec((1, SUBLANES, D), lambda b, i: (b, jnp.maximum(i * hb - 1, 0), 0)),
            pl.BlockSpec((1, tm, D), row_map),
            pl.BlockSpec((1, SUBLANES, D), lambda b, i: (b, jnp.minimum((i + 1) * hb, nblk8 - 1), 0)),
            pl.BlockSpec((1, 1, D), lambda b, i: (b, 0, 0)),
            pl.BlockSpec((1, 1, D), lambda b, i: (b, 0, 0)),
            pl.BlockSpec((1, D), const2),
            pl.BlockSpec((D, W_COLS), const2),
            pl.BlockSpec((tm, LANES), lambda b, i: (i, 0)),
            pl.BlockSpec((tm, LANES), lambda b, i: (i, 0)),
            pl.BlockSpec((1, ATT_WIDTH + K2_WIDTH), const2),
            pl.BlockSpec((CONV_W, 2 * M_WIDTH), const2),
            pl.BlockSpec((1, LANES), const2),
        ],
        out_specs=[
            pl.BlockSpec((1, tm, ATT_WIDTH), row_map),
            pl.BlockSpec((1, tm, K2_WIDTH), row_map),
            pl.BlockSpec((1, tm, K2_WIDTH), row_map),
            pl.BlockSpec((1, tm, M_WIDTH), row_map),
            pl.BlockSpec((1, tm, M_WIDTH), row_map),
            pl.BlockSpec((1, M_WIDTH, tm), lambda b, i: (b, 0, i)),
            pl.BlockSpec((1, tm, M_WIDTH), row_map),
            pl.BlockSpec((1, tm, M_WIDTH), row_map),
            pl.BlockSpec((1, tm, N_GATES), row_map),
        ],
        out_shape=[
            jax.ShapeDtypeStruct((B, N, ATT_WIDTH), bf16),
            jax.ShapeDtypeStruct((B, N, K2_WIDTH), bf16),
            jax.ShapeDtypeStruct((B, N, K2_WIDTH), bf16),
            jax.ShapeDtypeStruct((B, N, M_WIDTH), bf16),
            jax.ShapeDtypeStruct((B, N, M_WIDTH), bf16),
            jax.ShapeDtypeStruct((B, M_WIDTH, N), bf16),
            jax.ShapeDtypeStruct((B, N, M_WIDTH), bf16),
            jax.ShapeDtypeStruct((B, N, M_WIDTH), bf16),
            jax.ShapeDtypeStruct((B, N, N_GATES), f32),
        ],
        scratch_shapes=[pltpu.VMEM((tm + 2 * SUBLANES, 2 * M_WIDTH), f32)],
        compiler_params=pltpu.CompilerParams(
            dimension_semantics=("parallel", "parallel"), vmem_limit_bytes=VMEM_LIMIT_BYTES),
        name="in_projection",
    )(x, x, x, shift, scale, norm_w, w_all, cos_t, sin_t, qk_w, conv_qk, bg)
    return outs


def _attn_kernel(sink_ref, q_ref, kp_ref, kc_ref, kn_ref, vp_ref, vc_ref, vn_ref, ck_ref, cv_ref, o_ref, *, nb):
    n = pl.program_id(1)
    span = BLOCK + 2 * WINDOW
    grp = N_HEADS // N_KV_HEADS
    lane = lax.broadcasted_iota(jnp.int32, (1, LANES), 1)
    lo = lane < HEAD_DIM
    qi = lax.broadcasted_iota(jnp.int32, (BLOCK, span), 0)
    kj = lax.broadcasted_iota(jnp.int32, (BLOCK, span), 1)
    rel = kj - qi
    valid = (rel >= 0) & (rel <= 2 * WINDOW)
    valid = valid & ((kj >= WINDOW) | (n > 0)) & ((kj < WINDOW + BLOCK) | (n < nb - 1))
    valid2 = jnp.concatenate([valid, valid], axis=0)
    row2 = lax.broadcasted_iota(jnp.int32, (2 * BLOCK, 1), 0)
    zero = jnp.zeros((), bf16)
    for kh in range(N_KV_HEADS):
        ks = slice(kh * LANES, (kh + 1) * LANES)
        kk = jnp.concatenate([kp_ref[0, :, ks], kc_ref[0, :, ks], kn_ref[0, :, ks]], axis=0)
        vv = _ones_cols(jnp.concatenate([vp_ref[0, :, ks], vc_ref[0, :, ks], vn_ref[0, :, ks]], axis=0))
        ckk = ck_ref[0, :, ks]
        cvv = _ones_cols(cv_ref[0, :, ks])
        q0 = kh * grp * HEAD_DIM
        q2 = jnp.concatenate([q_ref[0, :, q0:q0 + LANES], q_ref[0, :, q0 + LANES:q0 + 2 * LANES]], axis=0)
        outs = []
        for half in range(2):
            keep = lo if half == 0 else jnp.logical_not(lo)
            kl = jnp.where(keep, kk, zero)
            ckl = jnp.where(keep, ckk, zero)
            s_loc = jnp.where(valid2, _dot_nt(q2, kl), NEG)
            s_ctx = _dot_nt(q2, ckl)
            sk = jnp.where(row2 < BLOCK, sink_ref[kh * grp + half], sink_ref[kh * grp + 2 + half]) * LOG2E
            m = jnp.maximum(jnp.maximum(jnp.max(s_loc, axis=-1, keepdims=True),
                                        jnp.max(s_ctx, axis=-1, keepdims=True)), sk)
            p_loc = jnp.exp2(s_loc - m)
            p_ctx = jnp.exp2(s_ctx - m)
            o2 = _dot(p_loc.astype(bf16), vv) + _dot(p_ctx.astype(bf16), cvv)
            den = o2[:, LANES:] + jnp.exp2(sk - m)
            outs.append(o2[:, :LANES] / den)
        ok = jnp.where(lo, outs[0], outs[1]).astype(o_ref.dtype)
        o_ref[0, :, q0:q0 + LANES] = ok[:BLOCK]
        o_ref[0, :, q0 + LANES:q0 + 2 * LANES] = ok[BLOCK:]


def _attention(sink, q, k2, v2, ck2, cv2):
    B, N, _ = q.shape
    L = ck2.shape[1]
    nb = N // BLOCK
    kern = functools.partial(_attn_kernel, nb=nb)
    prev_map = lambda b, n, s: (b, jnp.maximum(n - 1, 0), 0)
    cur_map = lambda b, n, s: (b, n, 0)
    next_map = lambda b, n, s: (b, jnp.minimum(n + 1, nb - 1), 0)
    ctx_map = lambda b, n, s: (b, 0, 0)
    kv_spec = lambda m: pl.BlockSpec((1, BLOCK, K2_WIDTH), m)
    return pl.pallas_call(
        kern,
        grid_spec=pltpu.PrefetchScalarGridSpec(
            num_scalar_prefetch=1,
            grid=(B, nb),
            in_specs=[pl.BlockSpec((1, BLOCK, ATT_WIDTH), cur_map),
                      kv_spec(prev_map), kv_spec(cur_map), kv_spec(next_map),
                      kv_spec(prev_map), kv_spec(cur_map), kv_spec(next_map),
                      pl.BlockSpec((1, L, K2_WIDTH), ctx_map),
                      pl.BlockSpec((1, L, K2_WIDTH), ctx_map)],
            out_specs=pl.BlockSpec((1, BLOCK, ATT_WIDTH), cur_map),
        ),
        out_shape=jax.ShapeDtypeStruct((B, N, ATT_WIDTH), bf16),
        compiler_params=pltpu.CompilerParams(dimension_semantics=("parallel", "parallel")),
        name="window_attention",
    )(sink, q, k2, k2, k2, v2, v2, v2, ck2, cv2)


def _tri_cumsum_cols(tri, x):
    h, m, l = _split3(x)
    return _dot(tri, h) + _dot(tri, m) + _dot(tri, l)


def _tri_cumsum_rows(x, tri_t):
    h, m, l = _split3(x)
    return _dot(h, tri_t) + _dot(m, tri_t) + _dot(l, tri_t)


def _tri(t):
    r = lax.broadcasted_iota(jnp.int32, (t, t), 0)
    c = lax.broadcasted_iota(jnp.int32, (t, t), 1)
    return jnp.where(c <= r, 1.0, 0.0).astype(bf16), jnp.where(r <= c, 1.0, 0.0).astype(bf16)


def _ctx_state_kernel(k_ref, v_ref, li_ref, lf_ref, c_ref, m_ref):
    L = k_ref.shape[1]
    tri, _ = _tri(L)
    li = li_ref[0]
    lf = lf_ref[0]
    lane = lax.broadcasted_iota(jnp.int32, (1, 2 * M_HEADS), 1)
    fwd = lane < M_HEADS
    cs = _tri_cumsum_cols(tri, lf)
    tot = cs[L - 1:L, :]
    b = jnp.where(fwd, cs, tot - cs + lf)
    w = tot - b + li
    m_new = jnp.maximum(tot, jnp.max(w, axis=0, keepdims=True))
    ws = jnp.exp(w - m_new)
    m_ref[0] = m_new
    for c in range(2 * M_HEADS):
        hs = slice((c % M_HEADS) * M_DIM, (c % M_HEADS + 1) * M_DIM)
        ks = k_ref[0, :, hs].astype(f32) * ws[:, c:c + 1]
        c_ref[0, c] = _dot_tn(ks.astype(bf16), _ones_cols(v_ref[0, :, hs]))


def _ctx_states(cmk, cmv, li_c, lf_c):
    B, L, _ = cmk.shape
    S = 2 * M_HEADS
    bmap = lambda b: (b, 0, 0)
    return pl.pallas_call(
        _ctx_state_kernel,
        grid=(B,),
        in_specs=[pl.BlockSpec((1, L, M_WIDTH), bmap), pl.BlockSpec((1, L, M_WIDTH), bmap),
                  pl.BlockSpec((1, L, S), bmap), pl.BlockSpec((1, L, S), bmap)],
        out_specs=[pl.BlockSpec((1, S, M_DIM, M_DIM + LANES), lambda b: (b, 0, 0, 0)),
                   pl.BlockSpec((1, 1, S), bmap)],
        out_shape=[jax.ShapeDtypeStruct((B, S, M_DIM, M_DIM + LANES), f32),
                   jax.ShapeDtypeStruct((B, 1, S), f32)],
        compiler_params=pltpu.CompilerParams(dimension_semantics=("parallel",)),
        name="mlstm_ctx_state",
    )(cmk, cmv, li_c, lf_c)


N_GATE_STATS = 6


def _gate_prep_kernel(li_ref, lf_ref, o_ref):
    S = 2 * M_HEADS
    T = CHUNK
    li = li_ref[0]
    lf = lf_ref[0]
    rows = li.shape[0]
    _, tri_t = _tri(T)
    fwd = (lax.broadcasted_iota(jnp.int32, (rows, 1), 0) % S) < M_HEADS
    lane = lax.broadcasted_iota(jnp.int32, (1, T), 1)
    cs = _tri_cumsum_rows(lf, tri_t)
    tot = cs[:, T - 1:T]
    b = jnp.where(fwd, cs, tot - cs + lf)
    e = li - b
    pm = e
    shift = 1
    while shift < T:
        from_left = jnp.where(lane >= shift, pltpu.roll(pm, shift, 1), -jnp.inf)
        from_right = jnp.where(lane < T - shift, pltpu.roll(pm, T - shift, 1), -jnp.inf)
        pm = jnp.maximum(pm, jnp.where(fwd, from_left, from_right))
        shift *= 2
    w = tot - b + li
    o_ref[0, 0] = e
    o_ref[0, 1] = pm
    o_ref[0, 2] = b
    o_ref[0, 3] = w
    o_ref[0, 4] = jnp.broadcast_to(tot, (rows, T))
    o_ref[0, 5] = jnp.broadcast_to(jnp.max(w, axis=-1, keepdims=True), (rows, T))


def _gate_prep(li_rc, lf_rc):
    B, rows, T = li_rc.shape
    bmap = lambda b: (b, 0, 0)
    return pl.pallas_call(
        _gate_prep_kernel,
        grid=(B,),
        in_specs=[pl.BlockSpec((1, rows, T), bmap), pl.BlockSpec((1, rows, T), bmap)],
        out_specs=pl.BlockSpec((1, N_GATE_STATS, rows, T), lambda b: (b, 0, 0, 0)),
        out_shape=jax.ShapeDtypeStruct((B, N_GATE_STATS, rows, T), f32),
        compiler_params=pltpu.CompilerParams(dimension_semantics=("parallel",)),
        name="mlstm_gate_prep",
    )(li_rc, lf_rc)


def _mlstm_kernel(qf_ref, ktf_ref, vf_ref, qb_ref, ktb_ref, vb_ref, gf_ref, gb_ref,
                  cn0_ref, m0_ref, hf_ref, hb_ref, cn_scr, m_scr):
    j = pl.program_id(1)
    S = 2 * M_HEADS
    T = CHUNK

    @pl.when(j == 0)
    def _():
        cn_scr[...] = cn0_ref[0]
        m_scr[...] = m0_ref[0]

    fwd_r = lax.broadcasted_iota(jnp.int32, (S, 1), 0) < M_HEADS
    stat = lambda k: jnp.where(fwd_r, gf_ref[0, k], gb_ref[0, k])
    e_r, pm, b_r, w_r, tot, w_max = [stat(k) for k in range(N_GATE_STATS)]

    m_prev = m_scr[...]
    g_hi = jnp.maximum(m_prev, pm).astype(bf16)
    g_used = g_hi.astype(f32)
    en_r = jnp.exp(-(b_r + g_used))
    en_hi = en_r.astype(bf16)
    en_lo = (en_r - en_hi.astype(f32)).astype(bf16)
    m_new = jnp.maximum(tot + m_prev, w_max)
    a_r = jnp.exp(tot + m_prev - m_new)
    ws_r = jnp.exp(w_r - m_new)
    m_scr[...] = m_new

    ti = lax.broadcasted_iota(jnp.int32, (T, T), 0)
    si = lax.broadcasted_iota(jnp.int32, (T, T), 1)
    eye = ti == si
    en_hi = en_hi.astype(f32)
    en_lo = en_lo.astype(f32)
    ones_tt = jnp.ones((T, T), f32).astype(bf16)
    zeros_tt = jnp.zeros((T, T), f32).astype(bf16)
    ones_2t = jnp.ones((2 * T, T), f32).astype(bf16)

    for c in range(S):
        is_fwd = c < M_HEADS
        hs = slice((c % M_HEADS) * M_DIM, (c % M_HEADS + 1) * M_DIM)
        row = lambda a: a[c:c + 1, :]
        diag = lambda a: jnp.where(eye, row(a), 0.0).astype(bf16)
        q = (qf_ref if is_fwd else qb_ref)[0, :, hs]
        kt = (ktf_ref if is_fwd else ktb_ref)[0, hs, :]
        v1 = _ones_cols((vf_ref if is_fwd else vb_ref)[0, :, hs])
        qk_g = _dot(jnp.concatenate([q, diag(g_used)], axis=1),
                    jnp.concatenate([jnp.concatenate([kt, zeros_tt], axis=1),
                                     jnp.concatenate([zeros_tt, ones_tt], axis=1)], axis=0))
        g_rep = qk_g[:, T:]
        en_rep = _dot(jnp.concatenate([diag(en_hi), diag(en_lo)], axis=1), ones_2t)
        within = (si <= ti) if is_fwd else (si >= ti)
        p_mat = jnp.exp(jnp.where(within, row(e_r) - g_rep, -jnp.inf))
        wts = p_mat * qk_g[:, :T]
        decay = jnp.exp(row(m_prev) - g_rep)
        cn = cn_scr[c]
        nd = _dot(wts.astype(bf16), v1)
        qc = _dot(q, cn.astype(bf16))
        num = nd[:, :M_DIM] + decay * qc[:, :M_DIM]
        den = nd[:, M_DIM:] + decay * qc[:, M_DIM:]
        h = num / jnp.maximum(jnp.abs(den), en_rep)
        (hf_ref if is_fwd else hb_ref)[0, :, hs] = h.astype(hf_ref.dtype)
        kst = (kt.astype(f32) * row(ws_r)).astype(bf16)
        a_c = row(a_r)
        cn_scr[c] = jnp.concatenate([a_c, a_c], axis=1) * cn + _dot(kst, v1)


def _mlstm(mq, mkt, mv, gstats, cn0, m0):
    B, N, _ = mq.shape
    nc = N // CHUNK
    S = 2 * M_HEADS
    fmap = lambda b, j: (b, j, 0)
    bmap = lambda b, j: (b, nc - 1 - j, 0)
    frow = lambda b, j: (b, 0, j)
    brow = lambda b, j: (b, 0, nc - 1 - j)
    seq = lambda m: pl.BlockSpec((1, CHUNK, M_WIDTH), m)
    seq_t = lambda m: pl.BlockSpec((1, M_WIDTH, CHUNK), m)
    gspec = lambda m: pl.BlockSpec((1, N_GATE_STATS, S, CHUNK), m)
    return pl.pallas_call(
        _mlstm_kernel,
        grid=(B, nc),
        in_specs=[seq(fmap), seq_t(frow), seq(fmap), seq(bmap), seq_t(brow), seq(bmap),
                  gspec(lambda b, j: (b, 0, j, 0)), gspec(lambda b, j: (b, 0, nc - 1 - j, 0)),
                  pl.BlockSpec((1, S, M_DIM, M_DIM + LANES), lambda b, j: (b, 0, 0, 0)),
                  pl.BlockSpec((1, S, CHUNK), lambda b, j: (b, 0, 0))],
        out_specs=[seq(fmap), seq(bmap)],
        out_shape=[jax.ShapeDtypeStruct((B, N, M_WIDTH), bf16), jax.ShapeDtypeStruct((B, N, M_WIDTH), bf16)],
        scratch_shapes=[pltpu.VMEM((S, M_DIM, M_DIM + LANES), f32), pltpu.VMEM((S, CHUNK), f32)],
        compiler_params=pltpu.CompilerParams(dimension_semantics=("parallel", "arbitrary")),
        name="mlstm_scan",
    )(mq, mkt, mv, mq, mkt, mv, gstats, gstats, cn0, m0)


def _outproj_kernel(x_ref, att_ref, hf_ref, hb_ref, mo_ref, mnw_ref, wo_ref, g1_ref, n2w_ref, sh2_ref, sc2_ref,
                    wrh_ref, wrl_ref, x1_ref, h2_ref, aff_ref, affr_ref, *, tm):
    parts = []
    for h in range(M_HEADS):
        hs = slice(h * M_DIM, (h + 1) * M_DIM)
        s = hf_ref[0, :, hs].astype(f32) + hb_ref[0, :, hs].astype(f32)
        ms = jnp.mean(s * s, axis=-1, keepdims=True)
        hn = s * lax.rsqrt(ms + EPS) * mnw_ref[:, hs]
        gate = 1.0 / (1.0 + jnp.exp(-mo_ref[0, :, hs].astype(f32)))
        parts.append((hn * gate).astype(bf16))
    ml = jnp.concatenate(parts, axis=-1)
    proj = _dot(att_ref[0], wo_ref[0:ATT_WIDTH, :]) + _dot(ml, wo_ref[ATT_WIDTH:, :])
    x1 = x_ref[0] + g1_ref[0] * proj
    x1_ref[0] = x1
    ms = jnp.mean(x1 * x1, axis=-1, keepdims=True)
    h2 = (x1 * lax.rsqrt(ms + EPS) * n2w_ref[...]) * (1.0 + sc2_ref[0]) + sh2_ref[0]
    for j in range(SUBLANES):
        h2_ref[0, pl.ds(j, tm, stride=SUBLANES), :] = h2[:, j * LANES:(j + 1) * LANES]
    hh = h2.astype(bf16)
    hl = (h2 - hh.astype(f32)).astype(bf16)
    wrh = wrh_ref[...]
    logits = _dot_nt(wrh, hh) + _dot_nt(wrh, hl) + _dot_nt(wrl_ref[...], hh)
    mx = jnp.max(logits, axis=0, keepdims=True)
    e = jnp.exp(logits - mx)
    aff = e / jnp.sum(e, axis=0, keepdims=True)
    aff_ref[0] = aff
    n_exp = aff.shape[0]
    affr_ref[0] = jnp.concatenate([aff.T, jnp.zeros((tm, LANES - n_exp), f32)], axis=1)


def _out_projection(x, att, hf, hb, mo, mnw, w_out, g1, n2w, sh2, sc2, wr_hi, wr_lo, *, tm):
    B, N, D = x.shape
    E = wr_hi.shape[0]
    row_map = lambda b, i: (b, i, 0)
    const2 = lambda b, i: (0, 0)
    bvec = lambda b, i: (b, 0, 0)
    return pl.pallas_call(
        functools.partial(_outproj_kernel, tm=tm),
        grid=(B, N // tm),
        in_specs=[pl.BlockSpec((1, tm, D), row_map),
                  pl.BlockSpec((1, tm, ATT_WIDTH), row_map),
                  pl.BlockSpec((1, tm, M_WIDTH), row_map),
                  pl.BlockSpec((1, tm, M_WIDTH), row_map),
                  pl.BlockSpec((1, tm, M_WIDTH), row_map),
                  pl.BlockSpec((1, M_WIDTH), const2),
                  pl.BlockSpec((ATT_WIDTH + M_WIDTH, D), const2),
                  pl.BlockSpec((1, 1, D), bvec),
                  pl.BlockSpec((1, D), const2),
                  pl.BlockSpec((1, 1, D), bvec),
                  pl.BlockSpec((1, 1, D), bvec),
                  pl.BlockSpec((E, D), const2),
                  pl.BlockSpec((E, D), const2)],
        out_specs=[pl.BlockSpec((1, tm, D), row_map),
                   pl.BlockSpec((1, tm * SUBLANES, LANES), row_map),
                   pl.BlockSpec((1, E, tm), lambda b, i: (b, 0, i)),
                   pl.BlockSpec((1, tm, LANES), row_map)],
        out_shape=[jax.ShapeDtypeStruct((B, N, D), f32),
                   jax.ShapeDtypeStruct((B, N * SUBLANES, LANES), f32),
                   jax.ShapeDtypeStruct((B, E, N), f32),
                   jax.ShapeDtypeStruct((B, N, LANES), f32)],
        compiler_params=pltpu.CompilerParams(
            dimension_semantics=("parallel", "parallel"), vmem_limit_bytes=VMEM_LIMIT_BYTES),
        name="out_projection",
    )(x, att, hf, hb, mo, mnw, w_out, g1, n2w, sh2, sc2, wr_hi, wr_lo)


def _chunk_stride(cap):
    return cap + SUBLANES


def _route_kernel(aff_ref, idx_ref, cl_scr, *, cap):
    E, N = aff_ref.shape[1], aff_ref.shape[2]
    bits = pltpu.bitcast(aff_ref[0], jnp.int32)
    thr = jnp.zeros((E, 1), jnp.int32)
    for bit in range(30, -1, -1):
        cand = thr | (1 << bit)
        cnt = jnp.sum(jnp.where(bits >= cand, 1.0, 0.0), axis=-1, keepdims=True)
        thr = jnp.where(cnt >= cap, cand, thr)
    above = jnp.where(bits > thr, 1.0, 0.0)
    equal = jnp.where(bits == thr, 1.0, 0.0)
    need = cap - jnp.sum(above, axis=-1, keepdims=True)
    n_chunks = N // LANES
    assert n_chunks <= LANES
    _, tri_t = _tri(LANES)
    lane = lax.broadcasted_iota(jnp.int32, (1, LANES), 1)
    both = jnp.concatenate([above, equal], axis=0).astype(bf16)
    run = jnp.zeros((2 * E, 1), f32)
    start = jnp.zeros((E, 1), f32)
    ends = jnp.full((E, LANES), float(2 * N), f32)
    cl_scr[...] = jnp.zeros_like(cl_scr)
    for k in range(n_chunks):
        ck = _dot(both[:, k * LANES:(k + 1) * LANES], tri_t)
        count = (ck[:E] + run[:E]) + jnp.minimum(ck[E:] + run[E:], need)
        local = count - start
        for e in range(E):
            cl_scr[e, k:k + 1, :] = local[e:e + 1, :]
        start = count[:, LANES - 1:LANES]
        ends = jnp.where(lane == k, start, ends)
        run = run + ck[:, LANES - 1:LANES]

    slot = lax.broadcasted_iota(jnp.int32, (cap, 1), 0).astype(f32)
    for e in range(E):
        ends_e = ends[e:e + 1, :]
        before = ends_e <= slot
        chunk = jnp.sum(jnp.where(before, 1.0, 0.0), axis=-1, keepdims=True)
        chunk_start = jnp.max(jnp.where(before, ends_e, 0.0), axis=-1, keepdims=True)
        pick = jnp.where(lane.astype(f32) == chunk, 1.0, 0.0).astype(bf16)
        local = _dot(pick, cl_scr[e].astype(bf16))
        pos = jnp.sum(jnp.where(local <= slot - chunk_start, 1.0, 0.0), axis=-1, keepdims=True)
        idx_ref[0, e] = (chunk * float(LANES) + pos).astype(jnp.int32)


def _route(aff_t, cap):
    B, E, N = aff_t.shape
    return pl.pallas_call(
        functools.partial(_route_kernel, cap=cap),
        grid=(B,),
        in_specs=[pl.BlockSpec((1, E, N), lambda b: (b, 0, 0))],
        out_specs=pl.BlockSpec((1, E, cap, 1), lambda b: (b, 0, 0, 0)),
        out_shape=jax.ShapeDtypeStruct((B, E, cap, 1), jnp.int32),
        scratch_shapes=[pltpu.VMEM((E, LANES, LANES), f32)],
        compiler_params=pltpu.CompilerParams(dimension_semantics=("parallel",)),
        name="expert_choice_route",
    )(aff_t)


def _ffn_kernel(idx_ref, idx_next_ref, src_ref, affr_ref, wg_ref, wu_ref, wd_ref, y_ref,
                x_even, x_odd, g_even, g_odd, *, cap):
    e = pl.program_id(1)
    stride = _chunk_stride(cap)

    def gather(ids_ref, x_scr, g_scr):
        for i in range(cap):
            t = ids_ref[0, 0, i]
            x_scr[pl.ds(i, SUBLANES, stride=stride), :] = (
                src_ref[0, pl.ds(pl.multiple_of(t * SUBLANES, SUBLANES), SUBLANES), :])
            g_scr[pl.ds(i, 1), :] = affr_ref[0, pl.ds(t, 1), :]

    def ffn(x_scr, g_scr):
        xg = jnp.concatenate([x_scr[j * stride:j * stride + cap, :].astype(bf16) for j in range(SUBLANES)], axis=1)
        lane = lax.broadcasted_iota(jnp.int32, (1, LANES), 1)
        gate = jnp.sum(jnp.where(lane == e, g_scr[...], 0.0), axis=-1, keepdims=True)
        g = _dot(xg, wg_ref[0])
        u = _dot(xg, wu_ref[0])
        hmid = (_silu(g) * u).astype(bf16)
        y = _dot(hmid, wd_ref[0]) * gate
        for j in range(y.shape[1] // LANES):
            y_ref[0, 0, j * stride:j * stride + cap, :] = y[:, j * LANES:(j + 1) * LANES]
            y_ref[0, 0, j * stride + cap:(j + 1) * stride, :] = jnp.zeros((stride - cap, LANES), f32)

    @pl.when(e == 0)
    def _():
        gather(idx_ref, x_even, g_even)

    @pl.when(e % 2 == 0)
    def _():
        gather(idx_next_ref, x_odd, g_odd)
        ffn(x_even, g_even)

    @pl.when(e % 2 == 1)
    def _():
        gather(idx_next_ref, x_even, g_even)
        ffn(x_odd, g_odd)


def _expert_ffn(idx, h2_slab, aff_rows, wg, wu, wd):
    B, E, cap = idx.shape
    D, FF = wg.shape[1], wg.shape[2]
    rows = (D // LANES) * _chunk_stride(cap)
    wmap = lambda b, e: (e, 0, 0)
    bmap = lambda b, e: (b, 0, 0)
    ids = idx.reshape(B * E, 1, cap)
    ids_spec = lambda m: pl.BlockSpec((1, 1, cap), m, memory_space=pltpu.SMEM)
    return pl.pallas_call(
        functools.partial(_ffn_kernel, cap=cap),
        grid=(B, E),
        in_specs=[ids_spec(lambda b, e: (b * E + e, 0, 0)),
                  ids_spec(lambda b, e: (b * E + jnp.minimum(e + 1, E - 1), 0, 0)),
                  pl.BlockSpec((1,) + h2_slab.shape[1:], bmap, pipeline_mode=pl.Buffered(1)),
                  pl.BlockSpec((1,) + aff_rows.shape[1:], bmap, pipeline_mode=pl.Buffered(1)),
                  pl.BlockSpec((1, D, FF), wmap),
                  pl.BlockSpec((1, D, FF), wmap),
                  pl.BlockSpec((1, FF, D), wmap)],
        out_specs=pl.BlockSpec((1, 1, rows, LANES), lambda b, e: (b, e, 0, 0)),
        out_shape=jax.ShapeDtypeStruct((B, E, rows, LANES), f32),
        scratch_shapes=[pltpu.VMEM((rows, LANES), f32), pltpu.VMEM((rows, LANES), f32),
                        pltpu.VMEM((cap, LANES), f32), pltpu.VMEM((cap, LANES), f32)],
        compiler_params=pltpu.CompilerParams(
            dimension_semantics=("parallel", "arbitrary"), vmem_limit_bytes=VMEM_LIMIT_BYTES),
        name="expert_ffn",
    )(ids, ids, h2_slab, aff_rows, wg, wu, wd)


SCATTER_UNROLL = 8


def _scatter_kernel(idx_ref, y_ref, acc_ref, *, cap):
    e = pl.program_id(1)

    @pl.when(e == 0)
    def _():
        acc_ref[...] = jnp.zeros_like(acc_ref)

    stride = _chunk_stride(cap)
    for i0 in range(0, cap, SCATTER_UNROLL):
        rows = [pl.multiple_of(idx_ref[0, 0, i0 + u] * SUBLANES, SUBLANES) for u in range(SCATTER_UNROLL)]
        vals = [acc_ref[0, pl.ds(rows[u], SUBLANES), :] + y_ref[0, 0, pl.ds(i0 + u, SUBLANES, stride=stride), :]
                for u in range(SCATTER_UNROLL)]
        for u in range(SCATTER_UNROLL):
            acc_ref[0, pl.ds(rows[u], SUBLANES), :] = vals[u]


def _scatter_add(idx, y_cm, n_tokens):
    B, E, cap = idx.shape
    rows = y_cm.shape[2]
    return pl.pallas_call(
        functools.partial(_scatter_kernel, cap=cap),
        grid=(B, E),
        in_specs=[pl.BlockSpec((1, 1, cap), lambda b, e: (b * E + e, 0, 0), memory_space=pltpu.SMEM),
                  pl.BlockSpec((1, 1, rows, LANES), lambda b, e: (b, e, 0, 0))],
        out_specs=pl.BlockSpec((1, n_tokens * SUBLANES, LANES), lambda b, e: (b, 0, 0)),
        out_shape=jax.ShapeDtypeStruct((B, n_tokens * SUBLANES, LANES), f32),
        compiler_params=pltpu.CompilerParams(
            dimension_semantics=("parallel", "arbitrary"), vmem_limit_bytes=VMEM_LIMIT_BYTES),
        name="expert_scatter_add",
    )(idx.reshape(B * E, 1, cap), y_cm)


def _final_kernel(x1_ref, g2_ref, acc_ref, o_ref, *, tm):
    ffn = jnp.concatenate([acc_ref[0, pl.ds(j, tm, stride=SUBLANES), :] for j in range(SUBLANES)], axis=-1)
    o_ref[0] = x1_ref[0] + g2_ref[0] * ffn


def _final_residual(x1, g2, acc, *, tm):
    B, N, D = x1.shape
    row_map = lambda b, i: (b, i, 0)
    return pl.pallas_call(
        functools.partial(_final_kernel, tm=tm),
        grid=(B, N // tm),
        in_specs=[pl.BlockSpec((1, tm, D), row_map),
                  pl.BlockSpec((1, 1, D), lambda b, i: (b, 0, 0)),
                  pl.BlockSpec((1, tm * SUBLANES, LANES), row_map)],
        out_specs=pl.BlockSpec((1, tm, D), row_map),
        out_shape=jax.ShapeDtypeStruct((B, N, D), f32),
        compiler_params=pltpu.CompilerParams(dimension_semantics=("parallel", "parallel")),
        name="final_residual",
    )(x1, g2, acc)


def _rope_tables(n):
    rows = n // GRID_W
    row, col = jnp.meshgrid(jnp.arange(rows), jnp.arange(GRID_W), indexing='ij')
    n_freq = HEAD_DIM // 4
    freqs = ROPE_BASE ** (-jnp.arange(n_freq, dtype=f32) / n_freq)
    ang = jnp.concatenate([row.reshape(-1, 1).astype(f32) * freqs, col.reshape(-1, 1).astype(f32) * freqs], -1)
    cos, sin = jnp.cos(ang), jnp.sin(ang)
    reps = LANES // HEAD_DIM
    return (jnp.tile(jnp.concatenate([cos, cos], -1), (1, reps)),
            jnp.tile(jnp.concatenate([-sin, sin], -1), (1, reps)))


def _pack_w_in(w_in):
    D = w_in.shape[0]
    o = 0
    aq = w_in[:, o:o + ATT_WIDTH]; o += ATT_WIDTH
    ak = w_in[:, o:o + KV_WIDTH]; o += KV_WIDTH
    av = w_in[:, o:o + KV_WIDTH]; o += KV_WIDTH
    mq = w_in[:, o:o + M_WIDTH]; o += M_WIDTH
    mk = w_in[:, o:o + M_WIDTH]; o += M_WIDTH
    mv = w_in[:, o:o + M_WIDTH]; o += M_WIDTH
    mo = w_in[:, o:o + M_WIDTH]; o += M_WIDTH
    gates = w_in[:, o:o + N_GATES]

    def dup(a):
        a = a.reshape(D, N_KV_HEADS, 1, HEAD_DIM)
        return jnp.broadcast_to(a, (D, N_KV_HEADS, 2, HEAD_DIM)).reshape(D, K2_WIDTH)

    pad = jnp.zeros((D, LANES - N_GATES), w_in.dtype)
    return jnp.concatenate([aq, dup(ak), dup(av), mq, mk, mv, mo, gates, pad], -1).astype(bf16)


def _gate_columns(g):
    H = M_HEADS
    li_c = jnp.concatenate([g[..., 0:H], g[..., 2 * H:3 * H]], -1)
    lf_c = jnp.concatenate([g[..., H:2 * H], g[..., 3 * H:4 * H]], -1)
    return li_c, lf_c


def _chunk_rows(a):
    B, N, S = a.shape
    return a.reshape(B, N // CHUNK, CHUNK, S).transpose(0, 1, 3, 2).reshape(B, (N // CHUNK) * S, CHUNK)


def _layer(x, ctx, mod_x, mod_c, norm1_w, norm2_w, w_in, b_gates, conv_qk, q_norm_w, k_norm_w, sink,
           mlstm_norm_w, w_out, w_router, w_gate, w_up, w_down):
    B, N, D = x.shape
    L = ctx.shape[1]
    assert D == SUBLANES * LANES, "token rows are handled as one (8, 128) register tile"
    tm = min(512, N)
    sh1, sc1, g1, sh2, sc2, g2 = [m[:, None, :] for m in jnp.split(mod_x, 6, -1)]
    csh1, csc1 = [jnp.broadcast_to(m[None, None, :], (B, 1, D)) for m in jnp.split(mod_c, 6, -1)[:2]]

    w_all = _pack_w_in(w_in)
    qk_w = jnp.concatenate([jnp.tile(q_norm_w, N_HEADS), jnp.tile(k_norm_w, 2 * N_KV_HEADS)])[None, :]
    bg = jnp.concatenate([b_gates, jnp.zeros((LANES - N_GATES,), f32)])[None, :]
    nw1 = norm1_w[None, :]
    cos_t, sin_t = _rope_tables(N)
    ones_t, zeros_t = jnp.ones((L, LANES), f32), jnp.zeros((L, LANES), f32)

    aq, k2, v2, mq, _, mkt, mv, mo, gts = _in_projection(x, sh1, sc1, nw1, w_all, cos_t, sin_t, qk_w, conv_qk, bg,
                                                         tm=tm)
    _, ck2, cv2, _, cmk, _, cmv, _, cgts = _in_projection(ctx, csh1, csc1, nw1, w_all, ones_t, zeros_t, qk_w,
                                                          conv_qk, bg, tm=L)

    att = _attention(sink, aq, k2, v2, ck2, cv2)

    cli_c, clf_c = _gate_columns(cgts)
    cn0, m0 = _ctx_states(cmk, cmv, cli_c, clf_c)
    m0 = jnp.broadcast_to(jnp.swapaxes(m0, 1, 2), (B, 2 * M_HEADS, CHUNK))
    li_c, lf_c = _gate_columns(gts)
    gstats = _gate_prep(_chunk_rows(li_c), _chunk_rows(lf_c))
    hf, hb = _mlstm(mq, mkt, mv, gstats, cn0, m0)

    wr_t = w_router.T
    wr_hi = wr_t.astype(bf16)
    wr_lo = (wr_t - wr_hi.astype(f32)).astype(bf16)
    x1, h2_slab, aff_t, aff_rows = _out_projection(x, att, hf, hb, mo, mlstm_norm_w[None, :], w_out.astype(bf16),
                                                   g1, norm2_w[None, :], sh2, sc2, wr_hi, wr_lo, tm=tm)

    cap = CAPACITY * N // N_EXPERTS
    idx = _route(aff_t, cap)[..., 0]
    y = _expert_ffn(idx, h2_slab, aff_rows, w_gate.astype(bf16), w_up.astype(bf16), w_down.astype(bf16))
    acc = _scatter_add(idx, y, N)
    return _final_residual(x1, g2, acc, tm=tm)


def kernel(x, c, ctx, c_ctx, w_mod, b_mod, norm1_w, norm2_w, w_in, b_gates, conv_qk, q_norm_w, k_norm_w, sink,
           mlstm_norm_w, w_out, w_router, w_gate, w_up, w_down):
    depth = w_mod.shape[0]
    assert depth == 1, "only the final-layer (no context update) form of the block is implemented"
    B = x.shape[0]
    pad_rows = (-(B + 1)) % SUBLANES
    c_all = jnp.concatenate([c, c_ctx[None, :], jnp.zeros((pad_rows, c.shape[1]), c.dtype)], 0)
    mod = _modulation(c_all, w_mod[0], b_mod[0])
    return _layer(x, ctx, mod[:B], mod[B], norm1_w[0], norm2_w[0], w_in[0], b_gates[0], conv_qk[0], q_norm_w[0],
                  k_norm_w[0], sink[0], mlstm_norm_w[0], w_out[0], w_router[0], w_gate[0], w_up[0], w_down[0])
```

```python
import functools

import jax
import jax.numpy as jnp
from jax import lax
from jax.experimental import pallas as pl
from jax.experimental.pallas import tpu as pltpu

f32 = jnp.float32
bf16 = jnp.bfloat16

GRID_W = 64
N_HEADS = 8
N_KV_HEADS = 2
HEAD_DIM = 64
WINDOW = 128
BLOCK = 128
ROPE_BASE = 10000.0
M_HEADS = 4
M_DIM = 128
CHUNK = 128
CONV_W = 3
ATT_WIDTH = N_HEADS * HEAD_DIM
KV_WIDTH = N_KV_HEADS * HEAD_DIM
M_WIDTH = M_HEADS * M_DIM
N_EXPERTS = 16
CAPACITY = 2
EPS = 1e-6
NEG = -1e30
LOG2E = 1.4426950408889634
Q_SCALE = HEAD_DIM ** -0.5 * LOG2E

LANES = 128
SUBLANES = 8
VMEM_LIMIT_BYTES = 56 * 1024 * 1024

K2_WIDTH = 2 * KV_WIDTH
C_Q = 0
C_K = C_Q + ATT_WIDTH
C_V = C_K + K2_WIDTH
C_MQK = C_V + K2_WIDTH
C_MV = C_MQK + 2 * M_WIDTH
C_MO = C_MV + M_WIDTH
C_G = C_MO + M_WIDTH
W_COLS = C_G + LANES
N_GATES = 4 * M_HEADS


def _dot(a, b):
    return jnp.dot(a, b, preferred_element_type=f32)


def _dot_nt(a, b):
    return lax.dot_general(a, b, (((1,), (1,)), ((), ())), preferred_element_type=f32)


def _dot_tn(a, b):
    return lax.dot_general(a, b, (((0,), (0,)), ((), ())), preferred_element_type=f32)


def _ones_cols(v):
    return jnp.concatenate([v, jnp.ones((v.shape[0], LANES), f32).astype(bf16)], axis=1)


def _split3(x):
    h = x.astype(bf16)
    r = x - h.astype(f32)
    m = r.astype(bf16)
    l = (r - m.astype(f32)).astype(bf16)
    return h, m, l


def _log_sigmoid(x):
    return jnp.minimum(x, 0.0) - jnp.log1p(jnp.exp(-jnp.abs(x)))


def _silu(x):
    return x / (1.0 + jnp.exp(-x))


def _mod_kernel(c_ref, w_ref, b_ref, o_ref):
    h, m, l = _split3(_silu(c_ref[...]))
    w = w_ref[...]
    wh = w.astype(bf16)
    wl = (w - wh.astype(f32)).astype(bf16)
    acc = _dot(h, wh) + _dot(m, wh) + _dot(h, wl) + _dot(l, wh) + _dot(m, wl)
    o_ref[...] = acc + b_ref[...]


def _modulation(c_all, w_mod, b_mod):
    rows, d = c_all.shape
    cols = w_mod.shape[1]
    tn = 512
    return pl.pallas_call(
        _mod_kernel,
        grid=(cols // tn,),
        in_specs=[pl.BlockSpec((rows, d), lambda j: (0, 0)),
                  pl.BlockSpec((d, tn), lambda j: (0, j)),
                  pl.BlockSpec((1, tn), lambda j: (0, j))],
        out_specs=pl.BlockSpec((rows, tn), lambda j: (0, j)),
        out_shape=jax.ShapeDtypeStruct((rows, cols), f32),
        name="adaln_mod",
    )(c_all, w_mod, b_mod.reshape(1, cols))


def _inproj_kernel(xp_ref, x_ref, xn_ref, shift_ref, scale_ref, nw_ref, w_ref, cos_ref, sin_ref, qkw_ref,
                   conv_ref, bg_ref,
                   aq_ref, k2_ref, v2_ref, mq_ref, mk_ref, mkt_ref, mv_ref, mo_ref, g_ref, conv_scr, *, tm, nt):
    i = pl.program_id(1)
    nw = nw_ref[...]
    sc = 1.0 + scale_ref[0]
    sh = shift_ref[0]

    def prep(xv):
        ms = jnp.mean(xv * xv, axis=-1, keepdims=True)
        return (xv * lax.rsqrt(ms + EPS) * nw) * sc + sh

    hm = prep(x_ref[0])
    lhs = hm.astype(bf16)
    lhs_halo = jnp.concatenate([prep(xp_ref[0]), hm, prep(xn_ref[0])], axis=0).astype(bf16)

    lane = lax.broadcasted_iota(jnp.int32, (1, LANES), 1)
    lo = lane < HEAD_DIM
    first_half = (lane % HEAD_DIM) < (HEAD_DIM // 2)
    cos = cos_ref[...]
    sin = sin_ref[...]
    n_qk = (ATT_WIDTH + K2_WIDTH) // LANES
    vqk = _dot(lhs, w_ref[:, C_Q:C_Q + ATT_WIDTH + K2_WIDTH])
    for g in range(n_qk):
        c0 = g * LANES
        v = vqk[:, c0:c0 + LANES]
        sq = v * v
        s_all = jnp.sum(sq, axis=-1, keepdims=True)
        s_lo = jnp.sum(jnp.where(lo, sq, 0.0), axis=-1, keepdims=True)
        ms = jnp.where(lo, s_lo, s_all - s_lo) * (1.0 / HEAD_DIM)
        nv = v * lax.rsqrt(ms + EPS) * qkw_ref[:, c0:c0 + LANES]
        swapped = jnp.where(first_half, pltpu.roll(nv, LANES - HEAD_DIM // 2, 1), pltpu.roll(nv, HEAD_DIM // 2, 1))
        r = nv * cos + swapped * sin
        if c0 < ATT_WIDTH:
            aq_ref[0, :, c0:c0 + LANES] = (r * Q_SCALE).astype(bf16)
        else:
            k2_ref[0, :, c0 - ATT_WIDTH:c0 - ATT_WIDTH + LANES] = r.astype(bf16)

    v2_ref[0] = _dot(lhs, w_ref[:, C_V:C_V + K2_WIDTH]).astype(bf16)
    mv_ref[0] = _dot(lhs, w_ref[:, C_MV:C_MV + M_WIDTH]).astype(bf16)
    mo_ref[0] = _dot(lhs, w_ref[:, C_MO:C_MO + M_WIDTH]).astype(bf16)

    g = _dot(lhs, w_ref[:, C_G:C_G + LANES]) + bg_ref[...]
    is_forget = ((lane // M_HEADS) % 2) == 1
    g_ref[0] = jnp.where(is_forget, _log_sigmoid(g), g)[:, :N_GATES]

    conv_scr[...] = _dot(lhs_halo, w_ref[:, C_MQK:C_MQK + 2 * M_WIDTH])
    row = lax.broadcasted_iota(jnp.int32, (tm, 1), 0)
    prev = conv_scr[SUBLANES - 1:SUBLANES - 1 + tm, :]
    prev = jnp.where((row == 0) & (i == 0), 0.0, prev)
    nxt = conv_scr[SUBLANES + 1:SUBLANES + 1 + tm, :]
    nxt = jnp.where((row == tm - 1) & (i == nt - 1), 0.0, nxt)
    cur = conv_scr[SUBLANES:SUBLANES + tm, :]
    u = prev * conv_ref[0:1, :] + cur * conv_ref[1:2, :] + nxt * conv_ref[2:3, :]
    u = _silu(u)
    mq_ref[0] = (u[:, :M_WIDTH] * (M_DIM ** -0.5)).astype(bf16)
    mk = u[:, M_WIDTH:]
    mk_ref[0] = mk.astype(bf16)
    mkt_ref[0] = mk.T.astype(bf16)


def _in_projection(x, shift, scale, norm_w, w_all, cos_t, sin_t, qk_w, conv_qk, bg, *, tm):
    B, N, D = x.shape
    nt = N // tm
    hb = tm // SUBLANES
    nblk8 = N // SUBLANES
    kern = functools.partial(_inproj_kernel, tm=tm, nt=nt)
    row_map = lambda b, i: (b, i, 0)
    const2 = lambda b, i: (0, 0)
    outs = pl.pallas_call(
        kern,
        grid=(B, nt),
        in_specs=[
            pl.BlockSpec((1, SUBLANES, D), lambda b, i: (b, jnp.maximum(i * hb - 1, 0), 0)),
            pl.BlockSpec((1, tm, D), row_map),
            pl.BlockSpec((1, SUBLANES, D), lambda b, i: (b, jnp.minimum((i + 1) * hb, nblk8 - 1), 0)),
            pl.BlockSpec((1, 1, D), lambda b, i: (b, 0, 0)),
            pl.BlockSpec((1, 1, D), lambda b, i: (b, 0, 0)),
            pl.BlockSpec((1, D), const2),
            pl.BlockSpec((D, W_COLS), const2),
            pl.BlockSpec((tm, LANES), lambda b, i: (i, 0)),
            pl.BlockSpec((tm, LANES), lambda b, i: (i, 0)),
            pl.BlockSpec((1, ATT_WIDTH + K2_WIDTH), const2),
            pl.BlockSpec((CONV_W, 2 * M_WIDTH), const2),
            pl.BlockSpec((1, LANES), const2),
        ],
        out_specs=[
            pl.BlockSpec((1, tm, ATT_WIDTH), row_map),
            pl.BlockSpec((1, tm, K2_WIDTH), row_map),
            pl.BlockSpec((1, tm, K2_WIDTH), row_map),
            pl.BlockSpec((1, tm, M_WIDTH), row_map),
            pl.BlockSpec((1, tm, M_WIDTH), row_map),
            pl.BlockSpec((1, M_WIDTH, tm), lambda b, i: (b, 0, i)),
            pl.BlockSpec((1, tm, M_WIDTH), row_map),
            pl.BlockSpec((1, tm, M_WIDTH), row_map),
            pl.BlockSpec((1, tm, N_GATES), row_map),
        ],
        out_shape=[
            jax.ShapeDtypeStruct((B, N, ATT_WIDTH), bf16),
            jax.ShapeDtypeStruct((B, N, K2_WIDTH), bf16),
            jax.ShapeDtypeStruct((B, N, K2_WIDTH), bf16),
            jax.ShapeDtypeStruct((B, N, M_WIDTH), bf16),
            jax.ShapeDtypeStruct((B, N, M_WIDTH), bf16),
            jax.ShapeDtypeStruct((B, M_WIDTH, N), bf16),
            jax.ShapeDtypeStruct((B, N, M_WIDTH), bf16),
            jax.ShapeDtypeStruct((B, N, M_WIDTH), bf16),
            jax.ShapeDtypeStruct((B, N, N_GATES), f32),
        ],
        scratch_shapes=[pltpu.VMEM((tm + 2 * SUBLANES, 2 * M_WIDTH), f32)],
        compiler_params=pltpu.CompilerParams(
            dimension_semantics=("parallel", "parallel"), vmem_limit_bytes=VMEM_LIMIT_BYTES),
        name="in_projection",
    )(x, x, x, shift, scale, norm_w, w_all, cos_t, sin_t, qk_w, conv_qk, bg)
    return outs


def _attn_kernel(sink_ref, q_ref, kp_ref, kc_ref, kn_ref, vp_ref, vc_ref, vn_ref, ck_ref, cv_ref, o_ref, *, nb):
    n = pl.program_id(1)
    span = BLOCK + 2 * WINDOW
    grp = N_HEADS // N_KV_HEADS
    lane = lax.broadcasted_iota(jnp.int32, (1, LANES), 1)
    lo = lane < HEAD_DIM
    qi = lax.broadcasted_iota(jnp.int32, (BLOCK, span), 0)
    kj = lax.broadcasted_iota(jnp.int32, (BLOCK, span), 1)
    rel = kj - qi
    valid = (rel >= 0) & (rel <= 2 * WINDOW)
    valid = valid & ((kj >= WINDOW) | (n > 0)) & ((kj < WINDOW + BLOCK) | (n < nb - 1))
    valid2 = jnp.concatenate([valid, valid], axis=0)
    row2 = lax.broadcasted_iota(jnp.int32, (2 * BLOCK, 1), 0)
    zero = jnp.zeros((), bf16)
    for kh in range(N_KV_HEADS):
        ks = slice(kh * LANES, (kh + 1) * LANES)
        kk = jnp.concatenate([kp_ref[0, :, ks], kc_ref[0, :, ks], kn_ref[0, :, ks]], axis=0)
        vv = _ones_cols(jnp.concatenate([vp_ref[0, :, ks], vc_ref[0, :, ks], vn_ref[0, :, ks]], axis=0))
        ckk = ck_ref[0, :, ks]
        cvv = _ones_cols(cv_ref[0, :, ks])
        q0 = kh * grp * HEAD_DIM
        q2 = jnp.concatenate([q_ref[0, :, q0:q0 + LANES], q_ref[0, :, q0 + LANES:q0 + 2 * LANES]], axis=0)
        outs = []
        for half in range(2):
            keep = lo if half == 0 else jnp.logical_not(lo)
            kl = jnp.where(keep, kk, zero)
            ckl = jnp.where(keep, ckk, zero)
            s_loc = jnp.where(valid2, _dot_nt(q2, kl), NEG)
            s_ctx = _dot_nt(q2, ckl)
            sk = jnp.where(row2 < BLOCK, sink_ref[kh * grp + half], sink_ref[kh * grp + 2 + half]) * LOG2E
            m = jnp.maximum(jnp.maximum(jnp.max(s_loc, axis=-1, keepdims=True),
                                        jnp.max(s_ctx, axis=-1, keepdims=True)), sk)
            p_loc = jnp.exp2(s_loc - m)
            p_ctx = jnp.exp2(s_ctx - m)
            o2 = _dot(p_loc.astype(bf16), vv) + _dot(p_ctx.astype(bf16), cvv)
            den = o2[:, LANES:] + jnp.exp2(sk - m)
            outs.append(o2[:, :LANES] / den)
        ok = jnp.where(lo, outs[0], outs[1]).astype(o_ref.dtype)
        o_ref[0, :, q0:q0 + LANES] = ok[:BLOCK]
        o_ref[0, :, q0 + LANES:q0 + 2 * LANES] = ok[BLOCK:]


def _attention(sink, q, k2, v2, ck2, cv2):
    B, N, _ = q.shape
    L = ck2.shape[1]
    nb = N // BLOCK
    kern = functools.partial(_attn_kernel, nb=nb)
    prev_map = lambda b, n, s: (b, jnp.maximum(n - 1, 0), 0)
    cur_map = lambda b, n, s: (b, n, 0)
    next_map = lambda b, n, s: (b, jnp.minimum(n + 1, nb - 1), 0)
    ctx_map = lambda b, n, s: (b, 0, 0)
    kv_spec = lambda m: pl.BlockSpec((1, BLOCK, K2_WIDTH), m)
    return pl.pallas_call(
        kern,
        grid_spec=pltpu.PrefetchScalarGridSpec(
            num_scalar_prefetch=1,
            grid=(B, nb),
            in_specs=[pl.BlockSpec((1, BLOCK, ATT_WIDTH), cur_map),
                      kv_spec(prev_map), kv_spec(cur_map), kv_spec(next_map),
                      kv_spec(prev_map), kv_spec(cur_map), kv_spec(next_map),
                      pl.BlockSpec((1, L, K2_WIDTH), ctx_map),
                      pl.BlockSpec((1, L, K2_WIDTH), ctx_map)],
            out_specs=pl.BlockSpec((1, BLOCK, ATT_WIDTH), cur_map),
        ),
        out_shape=jax.ShapeDtypeStruct((B, N, ATT_WIDTH), bf16),
        compiler_params=pltpu.CompilerParams(dimension_semantics=("parallel", "parallel")),
        name="window_attention",
    )(sink, q, k2, k2, k2, v2, v2, v2, ck2, cv2)


def _tri_cumsum_cols(tri, x):
    h, m, l = _split3(x)
    return _dot(tri, h) + _dot(tri, m) + _dot(tri, l)


def _tri_cumsum_rows(x, tri_t):
    h, m, l = _split3(x)
    return _dot(h, tri_t) + _dot(m, tri_t) + _dot(l, tri_t)


def _tri(t):
    r = lax.broadcasted_iota(jnp.int32, (t, t), 0)
    c = lax.broadcasted_iota(jnp.int32, (t, t), 1)
    return jnp.where(c <= r, 1.0, 0.0).astype(bf16), jnp.where(r <= c, 1.0, 0.0).astype(bf16)


def _ctx_state_kernel(k_ref, v_ref, li_ref, lf_ref, c_ref, m_ref):
    L = k_ref.shape[1]
    tri, _ = _tri(L)
    li = li_ref[0]
    lf = lf_ref[0]
    lane = lax.broadcasted_iota(jnp.int32, (1, 2 * M_HEADS), 1)
    fwd = lane < M_HEADS
    cs = _tri_cumsum_cols(tri, lf)
    tot = cs[L - 1:L, :]
    b = jnp.where(fwd, cs, tot - cs + lf)
    w = tot - b + li
    m_new = jnp.maximum(tot, jnp.max(w, axis=0, keepdims=True))
    ws = jnp.exp(w - m_new)
    m_ref[0] = m_new
    for c in range(2 * M_HEADS):
        hs = slice((c % M_HEADS) * M_DIM, (c % M_HEADS + 1) * M_DIM)
        ks = k_ref[0, :, hs].astype(f32) * ws[:, c:c + 1]
        c_ref[0, c] = _dot_tn(ks.astype(bf16), _ones_cols(v_ref[0, :, hs]))


def _ctx_states(cmk, cmv, li_c, lf_c):
    B, L, _ = cmk.shape
    S = 2 * M_HEADS
    bmap = lambda b: (b, 0, 0)
    return pl.pallas_call(
        _ctx_state_kernel,
        grid=(B,),
        in_specs=[pl.BlockSpec((1, L, M_WIDTH), bmap), pl.BlockSpec((1, L, M_WIDTH), bmap),
                  pl.BlockSpec((1, L, S), bmap), pl.BlockSpec((1, L, S), bmap)],
        out_specs=[pl.BlockSpec((1, S, M_DIM, M_DIM + LANES), lambda b: (b, 0, 0, 0)),
                   pl.BlockSpec((1, 1, S), bmap)],
        out_shape=[jax.ShapeDtypeStruct((B, S, M_DIM, M_DIM + LANES), f32),
                   jax.ShapeDtypeStruct((B, 1, S), f32)],
        compiler_params=pltpu.CompilerParams(dimension_semantics=("parallel",)),
        name="mlstm_ctx_state",
    )(cmk, cmv, li_c, lf_c)


N_GATE_STATS = 6


def _gate_prep_kernel(li_ref, lf_ref, o_ref):
    S = 2 * M_HEADS
    T = CHUNK
    li = li_ref[0]
    lf = lf_ref[0]
    rows = li.shape[0]
    _, tri_t = _tri(T)
    fwd = (lax.broadcasted_iota(jnp.int32, (rows, 1), 0) % S) < M_HEADS
    lane = lax.broadcasted_iota(jnp.int32, (1, T), 1)
    cs = _tri_cumsum_rows(lf, tri_t)
    tot = cs[:, T - 1:T]
    b = jnp.where(fwd, cs, tot - cs + lf)
    e = li - b
    pm = e
    shift = 1
    while shift < T:
        from_left = jnp.where(lane >= shift, pltpu.roll(pm, shift, 1), -jnp.inf)
        from_right = jnp.where(lane < T - shift, pltpu.roll(pm, T - shift, 1), -jnp.inf)
        pm = jnp.maximum(pm, jnp.where(fwd, from_left, from_right))
        shift *= 2
    w = tot - b + li
    o_ref[0, 0] = e
    o_ref[0, 1] = pm
    o_ref[0, 2] = b
    o_ref[0, 3] = w
    o_ref[0, 4] = jnp.broadcast_to(tot, (rows, T))
    o_ref[0, 5] = jnp.broadcast_to(jnp.max(w, axis=-1, keepdims=True), (rows, T))


def _gate_prep(li_rc, lf_rc):
    B, rows, T = li_rc.shape
    bmap = lambda b: (b, 0, 0)
    return pl.pallas_call(
        _gate_prep_kernel,
        grid=(B,),
        in_specs=[pl.BlockSpec((1, rows, T), bmap), pl.BlockSpec((1, rows, T), bmap)],
        out_specs=pl.BlockSpec((1, N_GATE_STATS, rows, T), lambda b: (b, 0, 0, 0)),
        out_shape=jax.ShapeDtypeStruct((B, N_GATE_STATS, rows, T), f32),
        compiler_params=pltpu.CompilerParams(dimension_semantics=("parallel",)),
        name="mlstm_gate_prep",
    )(li_rc, lf_rc)


def _mlstm_kernel(qf_ref, ktf_ref, vf_ref, qb_ref, ktb_ref, vb_ref, gf_ref, gb_ref,
                  cn0_ref, m0_ref, hf_ref, hb_ref, cn_scr, m_scr):
    j = pl.program_id(1)
    S = 2 * M_HEADS
    T = CHUNK

    @pl.when(j == 0)
    def _():
        cn_scr[...] = cn0_ref[0]
        m_scr[...] = m0_ref[0]

    fwd_r = lax.broadcasted_iota(jnp.int32, (S, 1), 0) < M_HEADS
    stat = lambda k: jnp.where(fwd_r, gf_ref[0, k], gb_ref[0, k])
    e_r, pm, b_r, w_r, tot, w_max = [stat(k) for k in range(N_GATE_STATS)]

    m_prev = m_scr[...]
    g_hi = jnp.maximum(m_prev, pm).astype(bf16)
    g_used = g_hi.astype(f32)
    en_r = jnp.exp(-(b_r + g_used))
    en_hi = en_r.astype(bf16)
    en_lo = (en_r - en_hi.astype(f32)).astype(bf16)
    m_new = jnp.maximum(tot + m_prev, w_max)
    a_r = jnp.exp(tot + m_prev - m_new)
    ws_r = jnp.exp(w_r - m_new)
    m_scr[...] = m_new

    ti = lax.broadcasted_iota(jnp.int32, (T, T), 0)
    si = lax.broadcasted_iota(jnp.int32, (T, T), 1)
    eye = ti == si
    en_hi = en_hi.astype(f32)
    en_lo = en_lo.astype(f32)
    ones_tt = jnp.ones((T, T), f32).astype(bf16)
    zeros_tt = jnp.zeros((T, T), f32).astype(bf16)
    ones_2t = jnp.ones((2 * T, T), f32).astype(bf16)

    for c in range(S):
        is_fwd = c < M_HEADS
        hs = slice((c % M_HEADS) * M_DIM, (c % M_HEADS + 1) * M_DIM)
        row = lambda a: a[c:c + 1, :]
        diag = lambda a: jnp.where(eye, row(a), 0.0).astype(bf16)
        q = (qf_ref if is_fwd else qb_ref)[0, :, hs]
        kt = (ktf_ref if is_fwd else ktb_ref)[0, hs, :]
        v1 = _ones_cols((vf_ref if is_fwd else vb_ref)[0, :, hs])
        qk_g = _dot(jnp.concatenate([q, diag(g_used)], axis=1),
                    jnp.concatenate([jnp.concatenate([kt, zeros_tt], axis=1),
                                     jnp.concatenate([zeros_tt, ones_tt], axis=1)], axis=0))
        g_rep = qk_g[:, T:]
        en_rep = _dot(jnp.concatenate([diag(en_hi), diag(en_lo)], axis=1), ones_2t)
        within = (si <= ti) if is_fwd else (si >= ti)
        p_mat = jnp.exp(jnp.where(within, row(e_r) - g_rep, -jnp.inf))
        wts = p_mat * qk_g[:, :T]
        decay = jnp.exp(row(m_prev) - g_rep)
        cn = cn_scr[c]
        nd = _dot(wts.astype(bf16), v1)
        qc = _dot(q, cn.astype(bf16))
        num = nd[:, :M_DIM] + decay * qc[:, :M_DIM]
        den = nd[:, M_DIM:] + decay * qc[:, M_DIM:]
        h = num / jnp.maximum(jnp.abs(den), en_rep)
        (hf_ref if is_fwd else hb_ref)[0, :, hs] = h.astype(hf_ref.dtype)
        kst = (kt.astype(f32) * row(ws_r)).astype(bf16)
        a_c = row(a_r)
        cn_scr[c] = jnp.concatenate([a_c, a_c], axis=1) * cn + _dot(kst, v1)


def _mlstm(mq, mkt, mv, gstats, cn0, m0):
    B, N, _ = mq.shape
    nc = N // CHUNK
    S = 2 * M_HEADS
    fmap = lambda b, j: (b, j, 0)
    bmap = lambda b, j: (b, nc - 1 - j, 0)
    frow = lambda b, j: (b, 0, j)
    brow = lambda b, j: (b, 0, nc - 1 - j)
    seq = lambda m: pl.BlockSpec((1, CHUNK, M_WIDTH), m)
    seq_t = lambda m: pl.BlockSpec((1, M_WIDTH, CHUNK), m)
    gspec = lambda m: pl.BlockSpec((1, N_GATE_STATS, S, CHUNK), m)
    return pl.pallas_call(
        _mlstm_kernel,
        grid=(B, nc),
        in_specs=[seq(fmap), seq_t(frow), seq(fmap), seq(bmap), seq_t(brow), seq(bmap),
                  gspec(lambda b, j: (b, 0, j, 0)), gspec(lambda b, j: (b, 0, nc - 1 - j, 0)),
                  pl.BlockSpec((1, S, M_DIM, M_DIM + LANES), lambda b, j: (b, 0, 0, 0)),
                  pl.BlockSpec((1, S, CHUNK), lambda b, j: (b, 0, 0))],
        out_specs=[seq(fmap), seq(bmap)],
        out_shape=[jax.ShapeDtypeStruct((B, N, M_WIDTH), bf16), jax.ShapeDtypeStruct((B, N, M_WIDTH), bf16)],
        scratch_shapes=[pltpu.VMEM((S, M_DIM, M_DIM + LANES), f32), pltpu.VMEM((S, CHUNK), f32)],
        compiler_params=pltpu.CompilerParams(dimension_semantics=("parallel", "arbitrary")),
        name="mlstm_scan",
    )(mq, mkt, mv, mq, mkt, mv, gstats, gstats, cn0, m0)


def _outproj_kernel(x_ref, att_ref, hf_ref, hb_ref, mo_ref, mnw_ref, wo_ref, g1_ref, n2w_ref, sh2_ref, sc2_ref,
                    wrh_ref, wrl_ref, x1_ref, h2_ref, aff_ref, affr_ref, *, tm):
    parts = []
    for h in range(M_HEADS):
        hs = slice(h * M_DIM, (h + 1) * M_DIM)
        s = hf_ref[0, :, hs].astype(f32) + hb_ref[0, :, hs].astype(f32)
        ms = jnp.mean(s * s, axis=-1, keepdims=True)
        hn = s * lax.rsqrt(ms + EPS) * mnw_ref[:, hs]
        gate = 1.0 / (1.0 + jnp.exp(-mo_ref[0, :, hs].astype(f32)))
        parts.append((hn * gate).astype(bf16))
    ml = jnp.concatenate(parts, axis=-1)
    proj = _dot(att_ref[0], wo_ref[0:ATT_WIDTH, :]) + _dot(ml, wo_ref[ATT_WIDTH:, :])
    x1 = x_ref[0] + g1_ref[0] * proj
    x1_ref[0] = x1
    ms = jnp.mean(x1 * x1, axis=-1, keepdims=True)
    h2 = (x1 * lax.rsqrt(ms + EPS) * n2w_ref[...]) * (1.0 + sc2_ref[0]) + sh2_ref[0]
    for j in range(SUBLANES):
        h2_ref[0, pl.ds(j, tm, stride=SUBLANES), :] = h2[:, j * LANES:(j + 1) * LANES]
    hh = h2.astype(bf16)
    hl = (h2 - hh.astype(f32)).astype(bf16)
    wrh = wrh_ref[...]
    logits = _dot_nt(wrh, hh) + _dot_nt(wrh, hl) + _dot_nt(wrl_ref[...], hh)
    mx = jnp.max(logits, axis=0, keepdims=True)
    e = jnp.exp(logits - mx)
    aff = e / jnp.sum(e, axis=0, keepdims=True)
    aff_ref[0] = aff
    n_exp = aff.shape[0]
    affr_ref[0] = jnp.concatenate([aff.T, jnp.zeros((tm, LANES - n_exp), f32)], axis=1)


def _out_projection(x, att, hf, hb, mo, mnw, w_out, g1, n2w, sh2, sc2, wr_hi, wr_lo, *, tm):
    B, N, D = x.shape
    E = wr_hi.shape[0]
    row_map = lambda b, i: (b, i, 0)
    const2 = lambda b, i: (0, 0)
    bvec = lambda b, i: (b, 0, 0)
    return pl.pallas_call(
        functools.partial(_outproj_kernel, tm=tm),
        grid=(B, N // tm),
        in_specs=[pl.BlockSpec((1, tm, D), row_map),
                  pl.BlockSpec((1, tm, ATT_WIDTH), row_map),
                  pl.BlockSpec((1, tm, M_WIDTH), row_map),
                  pl.BlockSpec((1, tm, M_WIDTH), row_map),
                  pl.BlockSpec((1, tm, M_WIDTH), row_map),
                  pl.BlockSpec((1, M_WIDTH), const2),
                  pl.BlockSpec((ATT_WIDTH + M_WIDTH, D), const2),
                  pl.BlockSpec((1, 1, D), bvec),
                  pl.BlockSpec((1, D), const2),
                  pl.BlockSpec((1, 1, D), bvec),
                  pl.BlockSpec((1, 1, D), bvec),
                  pl.BlockSpec((E, D), const2),
                  pl.BlockSpec((E, D), const2)],
        out_specs=[pl.BlockSpec((1, tm, D), row_map),
                   pl.BlockSpec((1, tm * SUBLANES, LANES), row_map),
                   pl.BlockSpec((1, E, tm), lambda b, i: (b, 0, i)),
                   pl.BlockSpec((1, tm, LANES), row_map)],
        out_shape=[jax.ShapeDtypeStruct((B, N, D), f32),
                   jax.ShapeDtypeStruct((B, N * SUBLANES, LANES), f32),
                   jax.ShapeDtypeStruct((B, E, N), f32),
                   jax.ShapeDtypeStruct((B, N, LANES), f32)],
        compiler_params=pltpu.CompilerParams(
            dimension_semantics=("parallel", "parallel"), vmem_limit_bytes=VMEM_LIMIT_BYTES),
        name="out_projection",
    )(x, att, hf, hb, mo, mnw, w_out, g1, n2w, sh2, sc2, wr_hi, wr_lo)


def _chunk_stride(cap):
    return cap + SUBLANES


ROUTE_EXPONENT_BITS = (64, 32, 16, 8, 4, 2, 1)
ROUTE_REFINE_STEPS = 25
ROUTE_TINY = 1e-30


def _route_kernel(aff_ref, idx_ref, cl_scr, *, cap):
    E, N = aff_ref.shape[1], aff_ref.shape[2]
    aff = aff_ref[0]

    def count_above(t):
        return jnp.sum(jnp.where(aff > t, 1.0, 0.0), axis=-1, keepdims=True)

    hi = jnp.full((E, 1), 2.0, f32)
    for bit in ROUTE_EXPONENT_BITS:
        cand = hi * (2.0 ** -bit)
        hi = jnp.where(count_above(cand) < cap, cand, hi)
    lo = jnp.where(hi < ROUTE_TINY, -1.0, 0.5 * hi)
    for _ in range(ROUTE_REFINE_STEPS):
        q = 0.25 * (hi - lo)
        m1, m2, m3 = lo + q, lo + 2.0 * q, lo + 3.0 * q
        ok1, ok2, ok3 = [count_above(m) >= cap for m in (m1, m2, m3)]
        lo, hi = (jnp.where(ok3, m3, jnp.where(ok2, m2, jnp.where(ok1, m1, lo))),
                  jnp.where(ok3, hi, jnp.where(ok2, m3, jnp.where(ok1, m2, m1))))
    above = jnp.where(aff > hi, 1.0, 0.0)
    equal = jnp.where(aff > lo, 1.0, 0.0) - above
    need = cap - jnp.sum(above, axis=-1, keepdims=True)
    n_chunks = N // LANES
    assert n_chunks <= LANES
    _, tri_t = _tri(LANES)
    lane = lax.broadcasted_iota(jnp.int32, (1, LANES), 1)
    both = jnp.concatenate([above, equal], axis=0).astype(bf16)
    run = jnp.zeros((2 * E, 1), f32)
    start = jnp.zeros((E, 1), f32)
    ends = jnp.full((E, LANES), float(2 * N), f32)
    cl_scr[...] = jnp.zeros_like(cl_scr)
    for k in range(n_chunks):
        ck = _dot(both[:, k * LANES:(k + 1) * LANES], tri_t)
        count = (ck[:E] + run[:E]) + jnp.minimum(ck[E:] + run[E:], need)
        local = count - start
        for e in range(E):
            cl_scr[e, k:k + 1, :] = local[e:e + 1, :]
        start = count[:, LANES - 1:LANES]
        ends = jnp.where(lane == k, start, ends)
        run = run + ck[:, LANES - 1:LANES]

    slot = lax.broadcasted_iota(jnp.int32, (cap, 1), 0).astype(f32)
    for e in range(E):
        ends_e = ends[e:e + 1, :]
        before = ends_e <= slot
        chunk = jnp.sum(jnp.where(before, 1.0, 0.0), axis=-1, keepdims=True)
        chunk_start = jnp.max(jnp.where(before, ends_e, 0.0), axis=-1, keepdims=True)
        pick = jnp.where(lane.astype(f32) == chunk, 1.0, 0.0).astype(bf16)
        local = _dot(pick, cl_scr[e].astype(bf16))
        pos = jnp.sum(jnp.where(local <= slot - chunk_start, 1.0, 0.0), axis=-1, keepdims=True)
        idx_ref[0, e] = (chunk * float(LANES) + pos).astype(jnp.int32)


def _route(aff_t, cap):
    B, E, N = aff_t.shape
    return pl.pallas_call(
        functools.partial(_route_kernel, cap=cap),
        grid=(B,),
        in_specs=[pl.BlockSpec((1, E, N), lambda b: (b, 0, 0))],
        out_specs=pl.BlockSpec((1, E, cap, 1), lambda b: (b, 0, 0, 0)),
        out_shape=jax.ShapeDtypeStruct((B, E, cap, 1), jnp.int32),
        scratch_shapes=[pltpu.VMEM((E, LANES, LANES), f32)],
        compiler_params=pltpu.CompilerParams(dimension_semantics=("parallel",)),
        name="expert_choice_route",
    )(aff_t)


SCATTER_UNROLL = 8
EXPERT_VMEM_LIMIT_BYTES = 60 * 1024 * 1024


def _expert_kernel(idx_prev_ref, idx_ref, idx_next_ref, src_ref, affr_ref, wg_ref, wu_ref, wd_ref, acc_ref,
                   x_even, x_odd, g_even, g_odd, y_even, y_odd, *, cap, n_experts):
    e = pl.program_id(1)
    stride = _chunk_stride(cap)

    def gather(ids_ref, x_scr, g_scr):
        for i in range(cap):
            t = ids_ref[0, 0, i]
            x_scr[pl.ds(i, SUBLANES, stride=stride), :] = (
                src_ref[0, pl.ds(pl.multiple_of(t * SUBLANES, SUBLANES), SUBLANES), :])
            g_scr[pl.ds(i, 1), :] = affr_ref[0, pl.ds(t, 1), :]

    def ffn(x_scr, g_scr, y_scr):
        xg = jnp.concatenate([x_scr[j * stride:j * stride + cap, :].astype(bf16) for j in range(SUBLANES)], axis=1)
        lane = lax.broadcasted_iota(jnp.int32, (1, LANES), 1)
        gate = jnp.sum(jnp.where(lane == e, g_scr[...], 0.0), axis=-1, keepdims=True)
        g = _dot(xg, wg_ref[0])
        u = _dot(xg, wu_ref[0])
        hmid = (_silu(g) * u).astype(bf16)
        y = _dot(hmid, wd_ref[0]) * gate
        for j in range(y.shape[1] // LANES):
            y_scr[j * stride:j * stride + cap, :] = y[:, j * LANES:(j + 1) * LANES]

    def scatter(ids_ref, y_scr):
        for i0 in range(0, cap, SCATTER_UNROLL):
            rows = [pl.multiple_of(ids_ref[0, 0, i0 + u] * SUBLANES, SUBLANES) for u in range(SCATTER_UNROLL)]
            vals = [acc_ref[0, pl.ds(rows[u], SUBLANES), :] + y_scr[pl.ds(i0 + u, SUBLANES, stride=stride), :]
                    for u in range(SCATTER_UNROLL)]
            for u in range(SCATTER_UNROLL):
                acc_ref[0, pl.ds(rows[u], SUBLANES), :] = vals[u]

    @pl.when(e == 0)
    def _():
        acc_ref[...] = jnp.zeros_like(acc_ref)
        y_odd[...] = jnp.zeros_like(y_odd)
        gather(idx_ref, x_even, g_even)

    @pl.when(e % 2 == 0)
    def _():
        gather(idx_next_ref, x_odd, g_odd)
        ffn(x_even, g_even, y_even)
        scatter(idx_prev_ref, y_odd)

    @pl.when(e % 2 == 1)
    def _():
        gather(idx_next_ref, x_even, g_even)
        ffn(x_odd, g_odd, y_odd)
        scatter(idx_prev_ref, y_even)

    @pl.when(e == n_experts - 1)
    def _():
        scatter(idx_ref, y_odd if (n_experts - 1) % 2 else y_even)


def _expert_mixture(idx, h2_slab, aff_rows, wg, wu, wd):
    B, E, cap = idx.shape
    D, FF = wg.shape[1], wg.shape[2]
    rows = (D // LANES) * _chunk_stride(cap)
    wmap = lambda b, e: (e, 0, 0)
    bmap = lambda b, e: (b, 0, 0)
    ids = idx.reshape(B * E, 1, cap)
    ids_spec = lambda m: pl.BlockSpec((1, 1, cap), m, memory_space=pltpu.SMEM)
    return pl.pallas_call(
        functools.partial(_expert_kernel, cap=cap, n_experts=E),
        grid=(B, E),
        in_specs=[ids_spec(lambda b, e: (b * E + jnp.maximum(e - 1, 0), 0, 0)),
                  ids_spec(lambda b, e: (b * E + e, 0, 0)),
                  ids_spec(lambda b, e: (b * E + jnp.minimum(e + 1, E - 1), 0, 0)),
                  pl.BlockSpec((1,) + h2_slab.shape[1:], bmap, pipeline_mode=pl.Buffered(1)),
                  pl.BlockSpec((1,) + aff_rows.shape[1:], bmap, pipeline_mode=pl.Buffered(1)),
                  pl.BlockSpec((1, D, FF), wmap),
                  pl.BlockSpec((1, D, FF), wmap),
                  pl.BlockSpec((1, FF, D), wmap)],
        out_specs=pl.BlockSpec((1,) + h2_slab.shape[1:], bmap, pipeline_mode=pl.Buffered(1)),
        out_shape=jax.ShapeDtypeStruct(h2_slab.shape, f32),
        scratch_shapes=[pltpu.VMEM((rows, LANES), f32), pltpu.VMEM((rows, LANES), f32),
                        pltpu.VMEM((cap, LANES), f32), pltpu.VMEM((cap, LANES), f32),
                        pltpu.VMEM((rows, LANES), f32), pltpu.VMEM((rows, LANES), f32)],
        compiler_params=pltpu.CompilerParams(
            dimension_semantics=("parallel", "arbitrary"), vmem_limit_bytes=EXPERT_VMEM_LIMIT_BYTES),
        name="expert_mixture",
    )(ids, ids, ids, h2_slab, aff_rows, wg, wu, wd)


def _final_kernel(x1_ref, g2_ref, acc_ref, o_ref, *, tm):
    ffn = jnp.concatenate([acc_ref[0, pl.ds(j, tm, stride=SUBLANES), :] for j in range(SUBLANES)], axis=-1)
    o_ref[0] = x1_ref[0] + g2_ref[0] * ffn


def _final_residual(x1, g2, acc, *, tm):
    B, N, D = x1.shape
    row_map = lambda b, i: (b, i, 0)
    return pl.pallas_call(
        functools.partial(_final_kernel, tm=tm),
        grid=(B, N // tm),
        in_specs=[pl.BlockSpec((1, tm, D), row_map),
                  pl.BlockSpec((1, 1, D), lambda b, i: (b, 0, 0)),
                  pl.BlockSpec((1, tm * SUBLANES, LANES), row_map)],
        out_specs=pl.BlockSpec((1, tm, D), row_map),
        out_shape=jax.ShapeDtypeStruct((B, N, D), f32),
        compiler_params=pltpu.CompilerParams(dimension_semantics=("parallel", "parallel")),
        name="final_residual",
    )(x1, g2, acc)


def _rope_tables(n):
    rows = n // GRID_W
    row, col = jnp.meshgrid(jnp.arange(rows), jnp.arange(GRID_W), indexing='ij')
    n_freq = HEAD_DIM // 4
    freqs = ROPE_BASE ** (-jnp.arange(n_freq, dtype=f32) / n_freq)
    ang = jnp.concatenate([row.reshape(-1, 1).astype(f32) * freqs, col.reshape(-1, 1).astype(f32) * freqs], -1)
    cos, sin = jnp.cos(ang), jnp.sin(ang)
    reps = LANES // HEAD_DIM
    return (jnp.tile(jnp.concatenate([cos, cos], -1), (1, reps)),
            jnp.tile(jnp.concatenate([-sin, sin], -1), (1, reps)))


def _pack_w_in(w_in):
    D = w_in.shape[0]
    o = 0
    aq = w_in[:, o:o + ATT_WIDTH]; o += ATT_WIDTH
    ak = w_in[:, o:o + KV_WIDTH]; o += KV_WIDTH
    av = w_in[:, o:o + KV_WIDTH]; o += KV_WIDTH
    mq = w_in[:, o:o + M_WIDTH]; o += M_WIDTH
    mk = w_in[:, o:o + M_WIDTH]; o += M_WIDTH
    mv = w_in[:, o:o + M_WIDTH]; o += M_WIDTH
    mo = w_in[:, o:o + M_WIDTH]; o += M_WIDTH
    gates = w_in[:, o:o + N_GATES]

    def dup(a):
        a = a.reshape(D, N_KV_HEADS, 1, HEAD_DIM)
        return jnp.broadcast_to(a, (D, N_KV_HEADS, 2, HEAD_DIM)).reshape(D, K2_WIDTH)

    pad = jnp.zeros((D, LANES - N_GATES), w_in.dtype)
    return jnp.concatenate([aq, dup(ak), dup(av), mq, mk, mv, mo, gates, pad], -1).astype(bf16)


def _gate_columns(g):
    H = M_HEADS
    li_c = jnp.concatenate([g[..., 0:H], g[..., 2 * H:3 * H]], -1)
    lf_c = jnp.concatenate([g[..., H:2 * H], g[..., 3 * H:4 * H]], -1)
    return li_c, lf_c


def _chunk_rows(a):
    B, N, S = a.shape
    return a.reshape(B, N // CHUNK, CHUNK, S).transpose(0, 1, 3, 2).reshape(B, (N // CHUNK) * S, CHUNK)


def _layer(x, ctx, mod_x, mod_c, norm1_w, norm2_w, w_in, b_gates, conv_qk, q_norm_w, k_norm_w, sink,
           mlstm_norm_w, w_out, w_router, w_gate, w_up, w_down):
    B, N, D = x.shape
    L = ctx.shape[1]
    assert D == SUBLANES * LANES, "token rows are handled as one (8, 128) register tile"
    tm = min(512, N)
    sh1, sc1, g1, sh2, sc2, g2 = [m[:, None, :] for m in jnp.split(mod_x, 6, -1)]
    csh1, csc1 = [jnp.broadcast_to(m[None, None, :], (B, 1, D)) for m in jnp.split(mod_c, 6, -1)[:2]]

    w_all = _pack_w_in(w_in)
    qk_w = jnp.concatenate([jnp.tile(q_norm_w, N_HEADS), jnp.tile(k_norm_w, 2 * N_KV_HEADS)])[None, :]
    bg = jnp.concatenate([b_gates, jnp.zeros((LANES - N_GATES,), f32)])[None, :]
    nw1 = norm1_w[None, :]
    cos_t, sin_t = _rope_tables(N)
    ones_t, zeros_t = jnp.ones((L, LANES), f32), jnp.zeros((L, LANES), f32)

    aq, k2, v2, mq, _, mkt, mv, mo, gts = _in_projection(x, sh1, sc1, nw1, w_all, cos_t, sin_t, qk_w, conv_qk, bg,
                                                         tm=tm)
    _, ck2, cv2, _, cmk, _, cmv, _, cgts = _in_projection(ctx, csh1, csc1, nw1, w_all, ones_t, zeros_t, qk_w,
                                                          conv_qk, bg, tm=L)

    att = _attention(sink, aq, k2, v2, ck2, cv2)

    cli_c, clf_c = _gate_columns(cgts)
    cn0, m0 = _ctx_states(cmk, cmv, cli_c, clf_c)
    m0 = jnp.broadcast_to(jnp.swapaxes(m0, 1, 2), (B, 2 * M_HEADS, CHUNK))
    li_c, lf_c = _gate_columns(gts)
    gstats = _gate_prep(_chunk_rows(li_c), _chunk_rows(lf_c))
    hf, hb = _mlstm(mq, mkt, mv, gstats, cn0, m0)

    wr_t = w_router.T
    wr_hi = wr_t.astype(bf16)
    wr_lo = (wr_t - wr_hi.astype(f32)).astype(bf16)
    x1, h2_slab, aff_t, aff_rows = _out_projection(x, att, hf, hb, mo, mlstm_norm_w[None, :], w_out.astype(bf16),
                                                   g1, norm2_w[None, :], sh2, sc2, wr_hi, wr_lo, tm=tm)

    cap = CAPACITY * N // N_EXPERTS
    idx = _route(aff_t, cap)[..., 0]
    acc = _expert_mixture(idx, h2_slab, aff_rows, w_gate.astype(bf16), w_up.astype(bf16), w_down.astype(bf16))
    return _final_residual(x1, g2, acc, tm=tm)


def kernel(x, c, ctx, c_ctx, w_mod, b_mod, norm1_w, norm2_w, w_in, b_gates, conv_qk, q_norm_w, k_norm_w, sink,
           mlstm_norm_w, w_out, w_router, w_gate, w_up, w_down):
    depth = w_mod.shape[0]
    assert depth == 1, "only the final-layer (no context update) form of the block is implemented"
    B = x.shape[0]
    pad_rows = (-(B + 1)) % SUBLANES
    c_all = jnp.concatenate([c, c_ctx[None, :], jnp.zeros((pad_rows, c.shape[1]), c.dtype)], 0)
    mod = _modulation(c_all, w_mod[0], b_mod[0])
    return _layer(x, ctx, mod[:B], mod[B], norm1_w[0], norm2_w[0], w_in[0], b_gates[0], conv_qk[0], q_norm_w[0],
                  k_norm_w[0], sink[0], mlstm_norm_w[0], w_out[0], w_router[0], w_gate[0], w_up[0], w_down[0])
```

```python
import functools

import jax
import jax.numpy as jnp
from jax import lax
from jax.experimental import pallas as pl
from jax.experimental.pallas import tpu as pltpu

f32 = jnp.float32
bf16 = jnp.bfloat16

GRID_W = 64
N_HEADS = 8
N_KV_HEADS = 2
HEAD_DIM = 64
WINDOW = 128
BLOCK = 128
ROPE_BASE = 10000.0
M_HEADS = 4
M_DIM = 128
CHUNK = 128
CONV_W = 3
ATT_WIDTH = N_HEADS * HEAD_DIM
KV_WIDTH = N_KV_HEADS * HEAD_DIM
M_WIDTH = M_HEADS * M_DIM
N_EXPERTS = 16
CAPACITY = 2
EPS = 1e-6
NEG = -1e30
LOG2E = 1.4426950408889634
Q_SCALE = HEAD_DIM ** -0.5 * LOG2E

LANES = 128
SUBLANES = 8
VMEM_LIMIT_BYTES = 56 * 1024 * 1024

K2_WIDTH = 2 * KV_WIDTH
C_Q = 0
C_K = C_Q + ATT_WIDTH
C_V = C_K + K2_WIDTH
C_MQK = C_V + KV_WIDTH
C_MV = C_MQK + 2 * M_WIDTH
C_MO = C_MV + M_WIDTH
C_G = C_MO + M_WIDTH
W_COLS = C_G + LANES
N_GATES = 4 * M_HEADS


def _dot(a, b):
    return jnp.dot(a, b, preferred_element_type=f32)


def _dot_nt(a, b):
    return lax.dot_general(a, b, (((1,), (1,)), ((), ())), preferred_element_type=f32)


def _dot_tn(a, b):
    return lax.dot_general(a, b, (((0,), (0,)), ((), ())), preferred_element_type=f32)


def _ones_cols(v):
    return jnp.concatenate([v, jnp.ones((v.shape[0], LANES), f32).astype(bf16)], axis=1)


def _split3(x):
    h = x.astype(bf16)
    r = x - h.astype(f32)
    m = r.astype(bf16)
    l = (r - m.astype(f32)).astype(bf16)
    return h, m, l


def _log_sigmoid(x):
    return jnp.minimum(x, 0.0) - jnp.log1p(jnp.exp(-jnp.abs(x)))


def _silu(x):
    return x / (1.0 + jnp.exp(-x))


def _mod_kernel(c_ref, w_ref, b_ref, o_ref):
    h, m, l = _split3(_silu(c_ref[...]))
    w = w_ref[...]
    wh = w.astype(bf16)
    wl = (w - wh.astype(f32)).astype(bf16)
    acc = _dot(h, wh) + _dot(m, wh) + _dot(h, wl) + _dot(l, wh) + _dot(m, wl)
    o_ref[...] = acc + b_ref[...]


def _modulation(c_all, w_mod, b_mod):
    rows, d = c_all.shape
    cols = w_mod.shape[1]
    tn = 512
    return pl.pallas_call(
        _mod_kernel,
        grid=(cols // tn,),
        in_specs=[pl.BlockSpec((rows, d), lambda j: (0, 0)),
                  pl.BlockSpec((d, tn), lambda j: (0, j)),
                  pl.BlockSpec((1, tn), lambda j: (0, j))],
        out_specs=pl.BlockSpec((rows, tn), lambda j: (0, j)),
        out_shape=jax.ShapeDtypeStruct((rows, cols), f32),
        name="adaln_mod",
    )(c_all, w_mod, b_mod.reshape(1, cols))


def _inproj_kernel(xp_ref, x_ref, xn_ref, shift_ref, scale_ref, nw_ref, w_ref, wvt_ref, cos_ref, sin_ref, qkw_ref,
                   conv_ref, bg_ref,
                   aq_ref, k2_ref, vt_ref, mq_ref, mk_ref, mkt_ref, mv_ref, mo_ref, g_ref, conv_scr, *, tm, nt):
    i = pl.program_id(1)
    nw = nw_ref[...]
    sc = 1.0 + scale_ref[0]
    sh = shift_ref[0]

    def prep(xv):
        ms = jnp.mean(xv * xv, axis=-1, keepdims=True)
        return (xv * lax.rsqrt(ms + EPS) * nw) * sc + sh

    hm = prep(x_ref[0])
    lhs = hm.astype(bf16)
    lhs_halo = jnp.concatenate([prep(xp_ref[0]), hm, prep(xn_ref[0])], axis=0).astype(bf16)

    lane = lax.broadcasted_iota(jnp.int32, (1, LANES), 1)
    lo = lane < HEAD_DIM
    first_half = (lane % HEAD_DIM) < (HEAD_DIM // 2)
    cos = cos_ref[...]
    sin = sin_ref[...]
    n_qk = (ATT_WIDTH + K2_WIDTH) // LANES
    vqk = _dot(lhs, w_ref[:, C_Q:C_Q + ATT_WIDTH + K2_WIDTH])
    for g in range(n_qk):
        c0 = g * LANES
        v = vqk[:, c0:c0 + LANES]
        sq = v * v
        s_all = jnp.sum(sq, axis=-1, keepdims=True)
        s_lo = jnp.sum(jnp.where(lo, sq, 0.0), axis=-1, keepdims=True)
        ms = jnp.where(lo, s_lo, s_all - s_lo) * (1.0 / HEAD_DIM)
        nv = v * lax.rsqrt(ms + EPS) * qkw_ref[:, c0:c0 + LANES]
        swapped = jnp.where(first_half, pltpu.roll(nv, LANES - HEAD_DIM // 2, 1), pltpu.roll(nv, HEAD_DIM // 2, 1))
        r = nv * cos + swapped * sin
        if c0 < ATT_WIDTH:
            aq_ref[0, :, c0:c0 + LANES] = (r * Q_SCALE).astype(bf16)
        else:
            k2_ref[0, :, c0 - ATT_WIDTH:c0 - ATT_WIDTH + LANES] = r.astype(bf16)

    vt_ref[0] = _dot_nt(wvt_ref[...], lhs).astype(bf16)
    mv_ref[0] = _dot(lhs, w_ref[:, C_MV:C_MV + M_WIDTH]).astype(bf16)
    mo_ref[0] = _dot(lhs, w_ref[:, C_MO:C_MO + M_WIDTH]).astype(bf16)

    g = _dot(lhs, w_ref[:, C_G:C_G + LANES]) + bg_ref[...]
    is_forget = ((lane // M_HEADS) % 2) == 1
    g_ref[0] = jnp.where(is_forget, _log_sigmoid(g), g)[:, :N_GATES]

    conv_scr[...] = _dot(lhs_halo, w_ref[:, C_MQK:C_MQK + 2 * M_WIDTH])
    row = lax.broadcasted_iota(jnp.int32, (tm, 1), 0)
    prev = conv_scr[SUBLANES - 1:SUBLANES - 1 + tm, :]
    prev = jnp.where((row == 0) & (i == 0), 0.0, prev)
    nxt = conv_scr[SUBLANES + 1:SUBLANES + 1 + tm, :]
    nxt = jnp.where((row == tm - 1) & (i == nt - 1), 0.0, nxt)
    cur = conv_scr[SUBLANES:SUBLANES + tm, :]
    u = prev * conv_ref[0:1, :] + cur * conv_ref[1:2, :] + nxt * conv_ref[2:3, :]
    u = _silu(u)
    mq_ref[0] = (u[:, :M_WIDTH] * (M_DIM ** -0.5)).astype(bf16)
    mk = u[:, M_WIDTH:]
    mk_ref[0] = mk.astype(bf16)
    mkt_ref[0] = mk.T.astype(bf16)


def _in_projection(x, shift, scale, norm_w, w_all, cos_t, sin_t, qk_w, conv_qk, bg, *, tm):
    B, N, D = x.shape
    nt = N // tm
    hb = tm // SUBLANES
    nblk8 = N // SUBLANES
    kern = functools.partial(_inproj_kernel, tm=tm, nt=nt)
    row_map = lambda b, i: (b, i, 0)
    const2 = lambda b, i: (0, 0)
    outs = pl.pallas_call(
        kern,
        grid=(B, nt),
        in_specs=[
            pl.BlockSpec((1, SUBLANES, D), lambda b, i: (b, jnp.maximum(i * hb - 1, 0), 0)),
            pl.BlockSpec((1, tm, D), row_map),
            pl.BlockSpec((1, SUBLANES, D), lambda b, i: (b, jnp.minimum((i + 1) * hb, nblk8 - 1), 0)),
            pl.BlockSpec((1, 1, D), lambda b, i: (b, 0, 0)),
            pl.BlockSpec((1, 1, D), lambda b, i: (b, 0, 0)),
            pl.BlockSpec((1, D), const2),
            pl.BlockSpec((D, W_COLS), const2),
            pl.BlockSpec((KV_WIDTH, D), const2),
            pl.BlockSpec((tm, LANES), lambda b, i: (i, 0)),
            pl.BlockSpec((tm, LANES), lambda b, i: (i, 0)),
            pl.BlockSpec((1, ATT_WIDTH + K2_WIDTH), const2),
            pl.BlockSpec((CONV_W, 2 * M_WIDTH), const2),
            pl.BlockSpec((1, LANES), const2),
        ],
        out_specs=[
            pl.BlockSpec((1, tm, ATT_WIDTH), row_map),
            pl.BlockSpec((1, tm, K2_WIDTH), row_map),
            pl.BlockSpec((1, KV_WIDTH, tm), lambda b, i: (b, 0, i)),
            pl.BlockSpec((1, tm, M_WIDTH), row_map),
            pl.BlockSpec((1, tm, M_WIDTH), row_map),
            pl.BlockSpec((1, M_WIDTH, tm), lambda b, i: (b, 0, i)),
            pl.BlockSpec((1, tm, M_WIDTH), row_map),
            pl.BlockSpec((1, tm, M_WIDTH), row_map),
            pl.BlockSpec((1, tm, N_GATES), row_map),
        ],
        out_shape=[
            jax.ShapeDtypeStruct((B, N, ATT_WIDTH), bf16),
            jax.ShapeDtypeStruct((B, N, K2_WIDTH), bf16),
            jax.ShapeDtypeStruct((B, KV_WIDTH, N), bf16),
            jax.ShapeDtypeStruct((B, N, M_WIDTH), bf16),
            jax.ShapeDtypeStruct((B, N, M_WIDTH), bf16),
            jax.ShapeDtypeStruct((B, M_WIDTH, N), bf16),
            jax.ShapeDtypeStruct((B, N, M_WIDTH), bf16),
            jax.ShapeDtypeStruct((B, N, M_WIDTH), bf16),
            jax.ShapeDtypeStruct((B, N, N_GATES), f32),
        ],
        scratch_shapes=[pltpu.VMEM((tm + 2 * SUBLANES, 2 * M_WIDTH), f32)],
        compiler_params=pltpu.CompilerParams(
            dimension_semantics=("parallel", "parallel"), vmem_limit_bytes=VMEM_LIMIT_BYTES),
        name="in_projection",
    )(x, x, x, shift, scale, norm_w, w_all, w_all[:, C_V:C_V + KV_WIDTH].T, cos_t, sin_t, qk_w, conv_qk, bg)
    return outs


def _attn_kernel(sink_ref, q_ref, kp_ref, kc_ref, kn_ref, vp_ref, vc_ref, vn_ref, ck_ref, cv_ref, o_ref, *, nb):
    n = pl.program_id(1)
    span = BLOCK + 2 * WINDOW
    grp = N_HEADS // N_KV_HEADS
    lane = lax.broadcasted_iota(jnp.int32, (1, LANES), 1)
    lo = lane < HEAD_DIM
    kj = lax.broadcasted_iota(jnp.int32, (span, BLOCK), 0)
    qi = lax.broadcasted_iota(jnp.int32, (span, BLOCK), 1)
    rel = kj - qi
    valid = (rel >= 0) & (rel <= 2 * WINDOW)
    valid = valid & ((kj >= WINDOW) | (n > 0)) & ((kj < WINDOW + BLOCK) | (n < nb - 1))
    bias = jnp.where(valid, 0.0, NEG)
    bias4 = jnp.concatenate([bias] * grp, axis=1)
    col = lax.broadcasted_iota(jnp.int32, (1, grp * BLOCK), 1)
    n_ctx = ck_ref.shape[1]
    ones_rows = jnp.ones((HEAD_DIM, span + n_ctx), f32).astype(bf16)
    zero = jnp.zeros((), bf16)
    scores = []
    for kh in range(N_KV_HEADS):
        ks = slice(kh * LANES, (kh + 1) * LANES)
        keys = jnp.concatenate([kp_ref[0, :, ks], kc_ref[0, :, ks], kn_ref[0, :, ks], ck_ref[0, :, ks]], axis=0)
        q0 = kh * grp * HEAD_DIM
        q2 = jnp.concatenate([q_ref[0, :, q0:q0 + LANES], q_ref[0, :, q0 + LANES:q0 + 2 * LANES]], axis=0)
        q4 = jnp.concatenate([jnp.where(lo, q2, zero), jnp.where(lo, zero, q2)], axis=0)
        scores.append(_dot_nt(keys, q4))
    probs = []
    for kh in range(N_KV_HEADS):
        heads = (0, 2, 1, 3)
        sk = sink_ref[kh * grp + heads[0]]
        for c, g in enumerate(heads[1:], start=1):
            sk = jnp.where(col >= c * BLOCK, sink_ref[kh * grp + g], sk)
        sk = sk * LOG2E
        s_t = scores[kh]
        s_loc = s_t[:span] + bias4
        s_ctx = s_t[span:]
        m = jnp.maximum(jnp.maximum(jnp.max(s_loc, axis=0, keepdims=True),
                                    jnp.max(s_ctx, axis=0, keepdims=True)), sk)
        p_t = jnp.concatenate([jnp.exp2(s_loc - m), jnp.exp2(s_ctx - m)], axis=0).astype(bf16)
        probs.append((p_t, jnp.exp2(sk - m)))
    for kh in range(N_KV_HEADS):
        hs = slice(kh * HEAD_DIM, (kh + 1) * HEAD_DIM)
        vals_t = jnp.concatenate([vp_ref[0, hs, :], vc_ref[0, hs, :], vn_ref[0, hs, :], cv_ref[0, hs, :]], axis=1)
        vals_t = jnp.concatenate([vals_t, ones_rows], axis=0)
        q0 = kh * grp * HEAD_DIM
        p_t, p_sink = probs[kh]
        o_t = _dot(vals_t, p_t)
        out_t = o_t[:HEAD_DIM] / (o_t[HEAD_DIM:] + p_sink)
        for pair in range(2):
            even = out_t[:, pair * BLOCK:(pair + 1) * BLOCK]
            odd = out_t[:, (2 + pair) * BLOCK:(3 + pair) * BLOCK]
            both = jnp.concatenate([even, odd], axis=0)
            o_ref[0, :, q0 + pair * LANES:q0 + (pair + 1) * LANES] = both.T.astype(o_ref.dtype)


def _attention(sink, q, k2, vt, ck2, cvt):
    B, N, _ = q.shape
    L = ck2.shape[1]
    nb = N // BLOCK
    kern = functools.partial(_attn_kernel, nb=nb)
    prev_blk = lambda n: jnp.maximum(n - 1, 0)
    next_blk = lambda n: jnp.minimum(n + 1, nb - 1)
    same = lambda n: n
    cur_map = lambda b, n, s: (b, n, 0)
    ctx_map = lambda b, n, s: (b, 0, 0)
    k_spec = lambda f: pl.BlockSpec((1, BLOCK, K2_WIDTH), lambda b, n, s: (b, f(n), 0))
    v_spec = lambda f: pl.BlockSpec((1, KV_WIDTH, BLOCK), lambda b, n, s: (b, 0, f(n)))
    return pl.pallas_call(
        kern,
        grid_spec=pltpu.PrefetchScalarGridSpec(
            num_scalar_prefetch=1,
            grid=(B, nb),
            in_specs=[pl.BlockSpec((1, BLOCK, ATT_WIDTH), cur_map),
                      k_spec(prev_blk), k_spec(same), k_spec(next_blk),
                      v_spec(prev_blk), v_spec(same), v_spec(next_blk),
                      pl.BlockSpec((1, L, K2_WIDTH), ctx_map),
                      pl.BlockSpec((1, KV_WIDTH, L), ctx_map)],
            out_specs=pl.BlockSpec((1, BLOCK, ATT_WIDTH), cur_map),
        ),
        out_shape=jax.ShapeDtypeStruct((B, N, ATT_WIDTH), bf16),
        compiler_params=pltpu.CompilerParams(dimension_semantics=("parallel", "parallel")),
        name="window_attention",
    )(sink, q, k2, k2, k2, vt, vt, vt, ck2, cvt)


def _tri_cumsum_cols(tri, x):
    h, m, l = _split3(x)
    return _dot(tri, h) + _dot(tri, m) + _dot(tri, l)


def _tri_cumsum_rows(x, tri_t):
    h, m, l = _split3(x)
    return _dot(h, tri_t) + _dot(m, tri_t) + _dot(l, tri_t)


def _tri(t):
    r = lax.broadcasted_iota(jnp.int32, (t, t), 0)
    c = lax.broadcasted_iota(jnp.int32, (t, t), 1)
    return jnp.where(c <= r, 1.0, 0.0).astype(bf16), jnp.where(r <= c, 1.0, 0.0).astype(bf16)


def _ctx_state_kernel(k_ref, v_ref, li_ref, lf_ref, c_ref, m_ref):
    L = k_ref.shape[1]
    tri, _ = _tri(L)
    li = li_ref[0]
    lf = lf_ref[0]
    lane = lax.broadcasted_iota(jnp.int32, (1, 2 * M_HEADS), 1)
    fwd = lane < M_HEADS
    cs = _tri_cumsum_cols(tri, lf)
    tot = cs[L - 1:L, :]
    b = jnp.where(fwd, cs, tot - cs + lf)
    w = tot - b + li
    m_new = jnp.maximum(tot, jnp.max(w, axis=0, keepdims=True))
    ws = jnp.exp(w - m_new)
    m_ref[0] = m_new
    for c in range(2 * M_HEADS):
        hs = slice((c % M_HEADS) * M_DIM, (c % M_HEADS + 1) * M_DIM)
        ks = k_ref[0, :, hs].astype(f32) * ws[:, c:c + 1]
        c_ref[0, c] = _dot_tn(ks.astype(bf16), _ones_cols(v_ref[0, :, hs]))


def _ctx_states(cmk, cmv, li_c, lf_c):
    B, L, _ = cmk.shape
    S = 2 * M_HEADS
    bmap = lambda b: (b, 0, 0)
    return pl.pallas_call(
        _ctx_state_kernel,
        grid=(B,),
        in_specs=[pl.BlockSpec((1, L, M_WIDTH), bmap), pl.BlockSpec((1, L, M_WIDTH), bmap),
                  pl.BlockSpec((1, L, S), bmap), pl.BlockSpec((1, L, S), bmap)],
        out_specs=[pl.BlockSpec((1, S, M_DIM, M_DIM + LANES), lambda b: (b, 0, 0, 0)),
                   pl.BlockSpec((1, 1, S), bmap)],
        out_shape=[jax.ShapeDtypeStruct((B, S, M_DIM, M_DIM + LANES), f32),
                   jax.ShapeDtypeStruct((B, 1, S), f32)],
        compiler_params=pltpu.CompilerParams(dimension_semantics=("parallel",)),
        name="mlstm_ctx_state",
    )(cmk, cmv, li_c, lf_c)


N_GATE_STATS = 6


def _gate_prep_kernel(li_ref, lf_ref, o_ref):
    S = 2 * M_HEADS
    T = CHUNK
    li = li_ref[0]
    lf = lf_ref[0]
    rows = li.shape[0]
    _, tri_t = _tri(T)
    fwd = (lax.broadcasted_iota(jnp.int32, (rows, 1), 0) % S) < M_HEADS
    lane = lax.broadcasted_iota(jnp.int32, (1, T), 1)
    cs = _tri_cumsum_rows(lf, tri_t)
    tot = cs[:, T - 1:T]
    b = jnp.where(fwd, cs, tot - cs + lf)
    e = li - b
    pm = e
    shift = 1
    while shift < T:
        from_left = jnp.where(lane >= shift, pltpu.roll(pm, shift, 1), -jnp.inf)
        from_right = jnp.where(lane < T - shift, pltpu.roll(pm, T - shift, 1), -jnp.inf)
        pm = jnp.maximum(pm, jnp.where(fwd, from_left, from_right))
        shift *= 2
    w = tot - b + li
    o_ref[0, 0] = e
    o_ref[0, 1] = pm
    o_ref[0, 2] = b
    o_ref[0, 3] = w
    o_ref[0, 4] = jnp.broadcast_to(tot, (rows, T))
    o_ref[0, 5] = jnp.broadcast_to(jnp.max(w, axis=-1, keepdims=True), (rows, T))


def _gate_prep(li_rc, lf_rc):
    B, rows, T = li_rc.shape
    bmap = lambda b: (b, 0, 0)
    return pl.pallas_call(
        _gate_prep_kernel,
        grid=(B,),
        in_specs=[pl.BlockSpec((1, rows, T), bmap), pl.BlockSpec((1, rows, T), bmap)],
        out_specs=pl.BlockSpec((1, N_GATE_STATS, rows, T), lambda b: (b, 0, 0, 0)),
        out_shape=jax.ShapeDtypeStruct((B, N_GATE_STATS, rows, T), f32),
        compiler_params=pltpu.CompilerParams(dimension_semantics=("parallel",)),
        name="mlstm_gate_prep",
    )(li_rc, lf_rc)


def _mlstm_kernel(qf_ref, ktf_ref, vf_ref, qb_ref, ktb_ref, vb_ref, gf_ref, gb_ref,
                  cn0_ref, m0_ref, hf_ref, hb_ref, cn_scr, m_scr):
    j = pl.program_id(1)
    S = 2 * M_HEADS
    T = CHUNK

    @pl.when(j == 0)
    def _():
        cn_scr[...] = cn0_ref[0]
        m_scr[...] = m0_ref[0]

    fwd_r = lax.broadcasted_iota(jnp.int32, (S, 1), 0) < M_HEADS
    stat = lambda k: jnp.where(fwd_r, gf_ref[0, k], gb_ref[0, k])
    e_r, pm, b_r, w_r, tot, w_max = [stat(k) for k in range(N_GATE_STATS)]

    m_prev = m_scr[...]
    g_hi = jnp.maximum(m_prev, pm).astype(bf16)
    g_used = g_hi.astype(f32)
    en_r = jnp.exp(-(b_r + g_used))
    en_hi = en_r.astype(bf16)
    en_lo = (en_r - en_hi.astype(f32)).astype(bf16)
    m_new = jnp.maximum(tot + m_prev, w_max)
    a_r = jnp.exp(tot + m_prev - m_new)
    ws_r = jnp.exp(w_r - m_new)
    m_scr[...] = m_new

    ti = lax.broadcasted_iota(jnp.int32, (T, T), 0)
    si = lax.broadcasted_iota(jnp.int32, (T, T), 1)
    eye = ti == si
    en_hi = en_hi.astype(f32)
    en_lo = en_lo.astype(f32)
    ones_tt = jnp.ones((T, T), f32).astype(bf16)
    zeros_tt = jnp.zeros((T, T), f32).astype(bf16)
    ones_2t = jnp.ones((2 * T, T), f32).astype(bf16)

    for c in range(S):
        is_fwd = c < M_HEADS
        hs = slice((c % M_HEADS) * M_DIM, (c % M_HEADS + 1) * M_DIM)
        row = lambda a: a[c:c + 1, :]
        diag = lambda a: jnp.where(eye, row(a), 0.0).astype(bf16)
        q = (qf_ref if is_fwd else qb_ref)[0, :, hs]
        kt = (ktf_ref if is_fwd else ktb_ref)[0, hs, :]
        v1 = _ones_cols((vf_ref if is_fwd else vb_ref)[0, :, hs])
        qk_g = _dot(jnp.concatenate([q, diag(g_used)], axis=1),
                    jnp.concatenate([jnp.concatenate([kt, zeros_tt], axis=1),
                                     jnp.concatenate([zeros_tt, ones_tt], axis=1)], axis=0))
        g_rep = qk_g[:, T:]
        en_rep = _dot(jnp.concatenate([diag(en_hi), diag(en_lo)], axis=1), ones_2t)
        within = (si <= ti) if is_fwd else (si >= ti)
        p_mat = jnp.exp(jnp.where(within, row(e_r) - g_rep, -jnp.inf))
        wts = p_mat * qk_g[:, :T]
        decay = jnp.exp(row(m_prev) - g_rep)
        cn = cn_scr[c]
        nd = _dot(wts.astype(bf16), v1)
        qc = _dot(q, cn.astype(bf16))
        num = nd[:, :M_DIM] + decay * qc[:, :M_DIM]
        den = nd[:, M_DIM:] + decay * qc[:, M_DIM:]
        h = num / jnp.maximum(jnp.abs(den), en_rep)
        (hf_ref if is_fwd else hb_ref)[0, :, hs] = h.astype(hf_ref.dtype)
        kst = (kt.astype(f32) * row(ws_r)).astype(bf16)
        a_c = row(a_r)
        cn_scr[c] = jnp.concatenate([a_c, a_c], axis=1) * cn + _dot(kst, v1)


def _mlstm(mq, mkt, mv, gstats, cn0, m0):
    B, N, _ = mq.shape
    nc = N // CHUNK
    S = 2 * M_HEADS
    fmap = lambda b, j: (b, j, 0)
    bmap = lambda b, j: (b, nc - 1 - j, 0)
    frow = lambda b, j: (b, 0, j)
    brow = lambda b, j: (b, 0, nc - 1 - j)
    seq = lambda m: pl.BlockSpec((1, CHUNK, M_WIDTH), m)
    seq_t = lambda m: pl.BlockSpec((1, M_WIDTH, CHUNK), m)
    gspec = lambda m: pl.BlockSpec((1, N_GATE_STATS, S, CHUNK), m)
    return pl.pallas_call(
        _mlstm_kernel,
        grid=(B, nc),
        in_specs=[seq(fmap), seq_t(frow), seq(fmap), seq(bmap), seq_t(brow), seq(bmap),
                  gspec(lambda b, j: (b, 0, j, 0)), gspec(lambda b, j: (b, 0, nc - 1 - j, 0)),
                  pl.BlockSpec((1, S, M_DIM, M_DIM + LANES), lambda b, j: (b, 0, 0, 0)),
                  pl.BlockSpec((1, S, CHUNK), lambda b, j: (b, 0, 0))],
        out_specs=[seq(fmap), seq(bmap)],
        out_shape=[jax.ShapeDtypeStruct((B, N, M_WIDTH), bf16), jax.ShapeDtypeStruct((B, N, M_WIDTH), bf16)],
        scratch_shapes=[pltpu.VMEM((S, M_DIM, M_DIM + LANES), f32), pltpu.VMEM((S, CHUNK), f32)],
        compiler_params=pltpu.CompilerParams(dimension_semantics=("parallel", "arbitrary")),
        name="mlstm_scan",
    )(mq, mkt, mv, mq, mkt, mv, gstats, gstats, cn0, m0)


def _outproj_kernel(x_ref, att_ref, hf_ref, hb_ref, mo_ref, mnw_ref, wo_ref, g1_ref, n2w_ref, sh2_ref, sc2_ref,
                    wrh_ref, wrl_ref, x1_ref, h2_ref, aff_ref, affr_ref, *, tm):
    parts = []
    for h in range(M_HEADS):
        hs = slice(h * M_DIM, (h + 1) * M_DIM)
        s = hf_ref[0, :, hs].astype(f32) + hb_ref[0, :, hs].astype(f32)
        ms = jnp.mean(s * s, axis=-1, keepdims=True)
        hn = s * lax.rsqrt(ms + EPS) * mnw_ref[:, hs]
        gate = 1.0 / (1.0 + jnp.exp(-mo_ref[0, :, hs].astype(f32)))
        parts.append((hn * gate).astype(bf16))
    ml = jnp.concatenate(parts, axis=-1)
    proj = _dot(att_ref[0], wo_ref[0:ATT_WIDTH, :]) + _dot(ml, wo_ref[ATT_WIDTH:, :])
    x1 = x_ref[0] + g1_ref[0] * proj
    x1_ref[0] = x1
    ms = jnp.mean(x1 * x1, axis=-1, keepdims=True)
    h2 = (x1 * lax.rsqrt(ms + EPS) * n2w_ref[...]) * (1.0 + sc2_ref[0]) + sh2_ref[0]
    for j in range(SUBLANES):
        h2_ref[0, pl.ds(j, tm, stride=SUBLANES), :] = h2[:, j * LANES:(j + 1) * LANES]
    hh = h2.astype(bf16)
    hl = (h2 - hh.astype(f32)).astype(bf16)
    wrh = wrh_ref[...]
    logits = _dot_nt(wrh, hh) + _dot_nt(wrh, hl) + _dot_nt(wrl_ref[...], hh)
    mx = jnp.max(logits, axis=0, keepdims=True)
    e = jnp.exp(logits - mx)
    aff = e / jnp.sum(e, axis=0, keepdims=True)
    aff_ref[0] = aff
    n_exp = aff.shape[0]
    affr_ref[0] = jnp.concatenate([aff.T, jnp.zeros((tm, LANES - n_exp), f32)], axis=1)


def _out_projection(x, att, hf, hb, mo, mnw, w_out, g1, n2w, sh2, sc2, wr_hi, wr_lo, *, tm):
    B, N, D = x.shape
    E = wr_hi.shape[0]
    row_map = lambda b, i: (b, i, 0)
    const2 = lambda b, i: (0, 0)
    bvec = lambda b, i: (b, 0, 0)
    return pl.pallas_call(
        functools.partial(_outproj_kernel, tm=tm),
        grid=(B, N // tm),
        in_specs=[pl.BlockSpec((1, tm, D), row_map),
                  pl.BlockSpec((1, tm, ATT_WIDTH), row_map),
                  pl.BlockSpec((1, tm, M_WIDTH), row_map),
                  pl.BlockSpec((1, tm, M_WIDTH), row_map),
                  pl.BlockSpec((1, tm, M_WIDTH), row_map),
                  pl.BlockSpec((1, M_WIDTH), const2),
                  pl.BlockSpec((ATT_WIDTH + M_WIDTH, D), const2),
                  pl.BlockSpec((1, 1, D), bvec),
                  pl.BlockSpec((1, D), const2),
                  pl.BlockSpec((1, 1, D), bvec),
                  pl.BlockSpec((1, 1, D), bvec),
                  pl.BlockSpec((E, D), const2),
                  pl.BlockSpec((E, D), const2)],
        out_specs=[pl.BlockSpec((1, tm, D), row_map),
                   pl.BlockSpec((1, tm * SUBLANES, LANES), row_map),
                   pl.BlockSpec((1, E, tm), lambda b, i: (b, 0, i)),
                   pl.BlockSpec((1, tm, LANES), row_map)],
        out_shape=[jax.ShapeDtypeStruct((B, N, D), f32),
                   jax.ShapeDtypeStruct((B, N * SUBLANES, LANES), f32),
                   jax.ShapeDtypeStruct((B, E, N), f32),
                   jax.ShapeDtypeStruct((B, N, LANES), f32)],
        compiler_params=pltpu.CompilerParams(
            dimension_semantics=("parallel", "parallel"), vmem_limit_bytes=VMEM_LIMIT_BYTES),
        name="out_projection",
    )(x, att, hf, hb, mo, mnw, w_out, g1, n2w, sh2, sc2, wr_hi, wr_lo)


def _chunk_stride(cap):
    return cap + SUBLANES


ROUTE_EXPONENT_BITS = (64, 32, 16, 8, 4, 2, 1)
ROUTE_REFINE_STEPS = 25
ROUTE_TINY = 1e-30


def _route_kernel(aff_ref, idx_ref, cl_scr, *, cap):
    E, N = aff_ref.shape[1], aff_ref.shape[2]
    aff = aff_ref[0]

    def count_above(t):
        return jnp.sum(jnp.where(aff > t, 1.0, 0.0), axis=-1, keepdims=True)

    hi = jnp.full((E, 1), 2.0, f32)
    for bit in ROUTE_EXPONENT_BITS:
        cand = hi * (2.0 ** -bit)
        hi = jnp.where(count_above(cand) < cap, cand, hi)
    lo = jnp.where(hi < ROUTE_TINY, -1.0, 0.5 * hi)
    for _ in range(ROUTE_REFINE_STEPS):
        q = 0.25 * (hi - lo)
        m1, m2, m3 = lo + q, lo + 2.0 * q, lo + 3.0 * q
        ok1, ok2, ok3 = [count_above(m) >= cap for m in (m1, m2, m3)]
        lo, hi = (jnp.where(ok3, m3, jnp.where(ok2, m2, jnp.where(ok1, m1, lo))),
                  jnp.where(ok3, hi, jnp.where(ok2, m3, jnp.where(ok1, m2, m1))))
    above = jnp.where(aff > hi, 1.0, 0.0)
    equal = jnp.where(aff > lo, 1.0, 0.0) - above
    need = cap - jnp.sum(above, axis=-1, keepdims=True)
    n_chunks = N // LANES
    assert n_chunks <= LANES
    _, tri_t = _tri(LANES)
    lane = lax.broadcasted_iota(jnp.int32, (1, LANES), 1)
    both = jnp.concatenate([above, equal], axis=0).astype(bf16)
    run = jnp.zeros((2 * E, 1), f32)
    start = jnp.zeros((E, 1), f32)
    ends = jnp.full((E, LANES), float(2 * N), f32)
    cl_scr[...] = jnp.zeros_like(cl_scr)
    for k in range(n_chunks):
        ck = _dot(both[:, k * LANES:(k + 1) * LANES], tri_t)
        count = (ck[:E] + run[:E]) + jnp.minimum(ck[E:] + run[E:], need)
        local = count - start
        for e in range(E):
            cl_scr[e, k:k + 1, :] = local[e:e + 1, :]
        start = count[:, LANES - 1:LANES]
        ends = jnp.where(lane == k, start, ends)
        run = run + ck[:, LANES - 1:LANES]

    slot = lax.broadcasted_iota(jnp.int32, (cap, 1), 0).astype(f32)
    for e in range(E):
        ends_e = ends[e:e + 1, :]
        before = ends_e <= slot
        chunk = jnp.sum(jnp.where(before, 1.0, 0.0), axis=-1, keepdims=True)
        chunk_start = jnp.max(jnp.where(before, ends_e, 0.0), axis=-1, keepdims=True)
        pick = jnp.where(lane.astype(f32) == chunk, 1.0, 0.0).astype(bf16)
        local = _dot(pick, cl_scr[e].astype(bf16))
        pos = jnp.sum(jnp.where(local <= slot - chunk_start, 1.0, 0.0), axis=-1, keepdims=True)
        idx_ref[0, e] = (chunk * float(LANES) + pos).astype(jnp.int32)


def _route(aff_t, cap):
    B, E, N = aff_t.shape
    return pl.pallas_call(
        functools.partial(_route_kernel, cap=cap),
        grid=(B,),
        in_specs=[pl.BlockSpec((1, E, N), lambda b: (b, 0, 0))],
        out_specs=pl.BlockSpec((1, E, cap, 1), lambda b: (b, 0, 0, 0)),
        out_shape=jax.ShapeDtypeStruct((B, E, cap, 1), jnp.int32),
        scratch_shapes=[pltpu.VMEM((E, LANES, LANES), f32)],
        compiler_params=pltpu.CompilerParams(dimension_semantics=("parallel",)),
        name="expert_choice_route",
    )(aff_t)


SCATTER_UNROLL = 8
EXPERT_VMEM_LIMIT_BYTES = 60 * 1024 * 1024


def _expert_kernel(idx_prev_ref, idx_ref, idx_next_ref, src_ref, affr_ref, wg_ref, wu_ref, wd_ref, acc_ref,
                   x_even, x_odd, g_even, g_odd, y_even, y_odd, *, cap, n_experts):
    e = pl.program_id(1)
    stride = _chunk_stride(cap)

    def gather(ids_ref, x_scr, g_scr):
        for i in range(cap):
            t = ids_ref[0, 0, i]
            x_scr[pl.ds(i, SUBLANES, stride=stride), :] = (
                src_ref[0, pl.ds(pl.multiple_of(t * SUBLANES, SUBLANES), SUBLANES), :])
            g_scr[pl.ds(i, 1), :] = affr_ref[0, pl.ds(t, 1), :]

    def ffn(x_scr, g_scr, y_scr):
        xg = jnp.concatenate([x_scr[j * stride:j * stride + cap, :].astype(bf16) for j in range(SUBLANES)], axis=1)
        lane = lax.broadcasted_iota(jnp.int32, (1, LANES), 1)
        gate = jnp.sum(jnp.where(lane == e, g_scr[...], 0.0), axis=-1, keepdims=True)
        g = _dot(xg, wg_ref[0])
        u = _dot(xg, wu_ref[0])
        hmid = (_silu(g) * u).astype(bf16)
        y = _dot(hmid, wd_ref[0]) * gate
        for j in range(y.shape[1] // LANES):
            y_scr[j * stride:j * stride + cap, :] = y[:, j * LANES:(j + 1) * LANES]

    def scatter(ids_ref, y_scr):
        for i0 in range(0, cap, SCATTER_UNROLL):
            rows = [pl.multiple_of(ids_ref[0, 0, i0 + u] * SUBLANES, SUBLANES) for u in range(SCATTER_UNROLL)]
            vals = [acc_ref[0, pl.ds(rows[u], SUBLANES), :] + y_scr[pl.ds(i0 + u, SUBLANES, stride=stride), :]
                    for u in range(SCATTER_UNROLL)]
            for u in range(SCATTER_UNROLL):
                acc_ref[0, pl.ds(rows[u], SUBLANES), :] = vals[u]

    @pl.when(e == 0)
    def _():
        acc_ref[...] = jnp.zeros_like(acc_ref)
        y_odd[...] = jnp.zeros_like(y_odd)
        gather(idx_ref, x_even, g_even)

    @pl.when(e % 2 == 0)
    def _():
        gather(idx_next_ref, x_odd, g_odd)
        ffn(x_even, g_even, y_even)
        scatter(idx_prev_ref, y_odd)

    @pl.when(e % 2 == 1)
    def _():
        gather(idx_next_ref, x_even, g_even)
        ffn(x_odd, g_odd, y_odd)
        scatter(idx_prev_ref, y_even)

    @pl.when(e == n_experts - 1)
    def _():
        scatter(idx_ref, y_odd if (n_experts - 1) % 2 else y_even)


def _expert_mixture(idx, h2_slab, aff_rows, wg, wu, wd):
    B, E, cap = idx.shape
    D, FF = wg.shape[1], wg.shape[2]
    rows = (D // LANES) * _chunk_stride(cap)
    wmap = lambda b, e: (e, 0, 0)
    bmap = lambda b, e: (b, 0, 0)
    ids = idx.reshape(B * E, 1, cap)
    ids_spec = lambda m: pl.BlockSpec((1, 1, cap), m, memory_space=pltpu.SMEM)
    return pl.pallas_call(
        functools.partial(_expert_kernel, cap=cap, n_experts=E),
        grid=(B, E),
        in_specs=[ids_spec(lambda b, e: (b * E + jnp.maximum(e - 1, 0), 0, 0)),
                  ids_spec(lambda b, e: (b * E + e, 0, 0)),
                  ids_spec(lambda b, e: (b * E + jnp.minimum(e + 1, E - 1), 0, 0)),
                  pl.BlockSpec((1,) + h2_slab.shape[1:], bmap, pipeline_mode=pl.Buffered(1)),
                  pl.BlockSpec((1,) + aff_rows.shape[1:], bmap, pipeline_mode=pl.Buffered(1)),
                  pl.BlockSpec((1, D, FF), wmap),
                  pl.BlockSpec((1, D, FF), wmap),
                  pl.BlockSpec((1, FF, D), wmap)],
        out_specs=pl.BlockSpec((1,) + h2_slab.shape[1:], bmap, pipeline_mode=pl.Buffered(1)),
        out_shape=jax.ShapeDtypeStruct(h2_slab.shape, f32),
        scratch_shapes=[pltpu.VMEM((rows, LANES), f32), pltpu.VMEM((rows, LANES), f32),
                        pltpu.VMEM((cap, LANES), f32), pltpu.VMEM((cap, LANES), f32),
                        pltpu.VMEM((rows, LANES), f32), pltpu.VMEM((rows, LANES), f32)],
        compiler_params=pltpu.CompilerParams(
            dimension_semantics=("parallel", "arbitrary"), vmem_limit_bytes=EXPERT_VMEM_LIMIT_BYTES),
        name="expert_mixture",
    )(ids, ids, ids, h2_slab, aff_rows, wg, wu, wd)


def _final_kernel(x1_ref, g2_ref, acc_ref, o_ref, *, tm):
    ffn = jnp.concatenate([acc_ref[0, pl.ds(j, tm, stride=SUBLANES), :] for j in range(SUBLANES)], axis=-1)
    o_ref[0] = x1_ref[0] + g2_ref[0] * ffn


def _final_residual(x1, g2, acc, *, tm):
    B, N, D = x1.shape
    row_map = lambda b, i: (b, i, 0)
    return pl.pallas_call(
        functools.partial(_final_kernel, tm=tm),
        grid=(B, N // tm),
        in_specs=[pl.BlockSpec((1, tm, D), row_map),
                  pl.BlockSpec((1, 1, D), lambda b, i: (b, 0, 0)),
                  pl.BlockSpec((1, tm * SUBLANES, LANES), row_map)],
        out_specs=pl.BlockSpec((1, tm, D), row_map),
        out_shape=jax.ShapeDtypeStruct((B, N, D), f32),
        compiler_params=pltpu.CompilerParams(dimension_semantics=("parallel", "parallel")),
        name="final_residual",
    )(x1, g2, acc)


def _rope_tables(n):
    rows = n // GRID_W
    row, col = jnp.meshgrid(jnp.arange(rows), jnp.arange(GRID_W), indexing='ij')
    n_freq = HEAD_DIM // 4
    freqs = ROPE_BASE ** (-jnp.arange(n_freq, dtype=f32) / n_freq)
    ang = jnp.concatenate([row.reshape(-1, 1).astype(f32) * freqs, col.reshape(-1, 1).astype(f32) * freqs], -1)
    cos, sin = jnp.cos(ang), jnp.sin(ang)
    reps = LANES // HEAD_DIM
    return (jnp.tile(jnp.concatenate([cos, cos], -1), (1, reps)),
            jnp.tile(jnp.concatenate([-sin, sin], -1), (1, reps)))


def _pack_w_in(w_in):
    D = w_in.shape[0]
    o = 0
    aq = w_in[:, o:o + ATT_WIDTH]; o += ATT_WIDTH
    ak = w_in[:, o:o + KV_WIDTH]; o += KV_WIDTH
    av = w_in[:, o:o + KV_WIDTH]; o += KV_WIDTH
    mq = w_in[:, o:o + M_WIDTH]; o += M_WIDTH
    mk = w_in[:, o:o + M_WIDTH]; o += M_WIDTH
    mv = w_in[:, o:o + M_WIDTH]; o += M_WIDTH
    mo = w_in[:, o:o + M_WIDTH]; o += M_WIDTH
    gates = w_in[:, o:o + N_GATES]

    def dup(a):
        a = a.reshape(D, N_KV_HEADS, 1, HEAD_DIM)
        return jnp.broadcast_to(a, (D, N_KV_HEADS, 2, HEAD_DIM)).reshape(D, K2_WIDTH)

    pad = jnp.zeros((D, LANES - N_GATES), w_in.dtype)
    return jnp.concatenate([aq, dup(ak), av, mq, mk, mv, mo, gates, pad], -1).astype(bf16)


def _gate_columns(g):
    H = M_HEADS
    li_c = jnp.concatenate([g[..., 0:H], g[..., 2 * H:3 * H]], -1)
    lf_c = jnp.concatenate([g[..., H:2 * H], g[..., 3 * H:4 * H]], -1)
    return li_c, lf_c


def _chunk_rows(a):
    B, N, S = a.shape
    return a.reshape(B, N // CHUNK, CHUNK, S).transpose(0, 1, 3, 2).reshape(B, (N // CHUNK) * S, CHUNK)


def _layer(x, ctx, mod_x, mod_c, norm1_w, norm2_w, w_in, b_gates, conv_qk, q_norm_w, k_norm_w, sink,
           mlstm_norm_w, w_out, w_router, w_gate, w_up, w_down):
    B, N, D = x.shape
    L = ctx.shape[1]
    assert D == SUBLANES * LANES, "token rows are handled as one (8, 128) register tile"
    tm = min(512, N)
    sh1, sc1, g1, sh2, sc2, g2 = [m[:, None, :] for m in jnp.split(mod_x, 6, -1)]
    csh1, csc1 = [jnp.broadcast_to(m[None, None, :], (B, 1, D)) for m in jnp.split(mod_c, 6, -1)[:2]]

    w_all = _pack_w_in(w_in)
    qk_w = jnp.concatenate([jnp.tile(q_norm_w, N_HEADS), jnp.tile(k_norm_w, 2 * N_KV_HEADS)])[None, :]
    bg = jnp.concatenate([b_gates, jnp.zeros((LANES - N_GATES,), f32)])[None, :]
    nw1 = norm1_w[None, :]
    cos_t, sin_t = _rope_tables(N)
    ones_t, zeros_t = jnp.ones((L, LANES), f32), jnp.zeros((L, LANES), f32)

    aq, k2, vt, mq, _, mkt, mv, mo, gts = _in_projection(x, sh1, sc1, nw1, w_all, cos_t, sin_t, qk_w, conv_qk, bg,
                                                         tm=tm)
    _, ck2, cvt, _, cmk, _, cmv, _, cgts = _in_projection(ctx, csh1, csc1, nw1, w_all, ones_t, zeros_t, qk_w,
                                                          conv_qk, bg, tm=L)

    att = _attention(sink, aq, k2, vt, ck2, cvt)

    cli_c, clf_c = _gate_columns(cgts)
    cn0, m0 = _ctx_states(cmk, cmv, cli_c, clf_c)
    m0 = jnp.broadcast_to(jnp.swapaxes(m0, 1, 2), (B, 2 * M_HEADS, CHUNK))
    li_c, lf_c = _gate_columns(gts)
    gstats = _gate_prep(_chunk_rows(li_c), _chunk_rows(lf_c))
    hf, hb = _mlstm(mq, mkt, mv, gstats, cn0, m0)

    wr_t = w_router.T
    wr_hi = wr_t.astype(bf16)
    wr_lo = (wr_t - wr_hi.astype(f32)).astype(bf16)
    x1, h2_slab, aff_t, aff_rows = _out_projection(x, att, hf, hb, mo, mlstm_norm_w[None, :], w_out.astype(bf16),
                                                   g1, norm2_w[None, :], sh2, sc2, wr_hi, wr_lo, tm=tm)

    cap = CAPACITY * N // N_EXPERTS
    idx = _route(aff_t, cap)[..., 0]
    acc = _expert_mixture(idx, h2_slab, aff_rows, w_gate.astype(bf16), w_up.astype(bf16), w_down.astype(bf16))
    return _final_residual(x1, g2, acc, tm=tm)


def kernel(x, c, ctx, c_ctx, w_mod, b_mod, norm1_w, norm2_w, w_in, b_gates, conv_qk, q_norm_w, k_norm_w, sink,
           mlstm_norm_w, w_out, w_router, w_gate, w_up, w_down):
    depth = w_mod.shape[0]
    assert depth == 1, "only the final-layer (no context update) form of the block is implemented"
    B = x.shape[0]
    pad_rows = (-(B + 1)) % SUBLANES
    c_all = jnp.concatenate([c, c_ctx[None, :], jnp.zeros((pad_rows, c.shape[1]), c.dtype)], 0)
    mod = _modulation(c_all, w_mod[0], b_mod[0])
    return _layer(x, ctx, mod[:B], mod[B], norm1_w[0], norm2_w[0], w_in[0], b_gates[0], conv_qk[0], q_norm_w[0],
                  k_norm_w[0], sink[0], mlstm_norm_w[0], w_out[0], w_router[0], w_gate[0], w_up[0], w_down[0])
```

```python
import functools

import jax
import jax.numpy as jnp
from jax import lax
from jax.experimental import pallas as pl
from jax.experimental.pallas import tpu as pltpu

f32 = jnp.float32
bf16 = jnp.bfloat16

GRID_W = 64
N_HEADS = 8
N_KV_HEADS = 2
HEAD_DIM = 64
WINDOW = 128
BLOCK = 128
ROPE_BASE = 10000.0
M_HEADS = 4
M_DIM = 128
CHUNK = 128
CONV_W = 3
ATT_WIDTH = N_HEADS * HEAD_DIM
KV_WIDTH = N_KV_HEADS * HEAD_DIM
M_WIDTH = M_HEADS * M_DIM
N_EXPERTS = 16
CAPACITY = 2
EPS = 1e-6
NEG = -1e30
LOG2E = 1.4426950408889634
Q_SCALE = HEAD_DIM ** -0.5 * LOG2E

LANES = 128
SUBLANES = 8
VMEM_LIMIT_BYTES = 56 * 1024 * 1024

K2_WIDTH = 2 * KV_WIDTH
C_Q = 0
C_K = C_Q + ATT_WIDTH
C_V = C_K + K2_WIDTH
C_MQK = C_V + KV_WIDTH
C_MV = C_MQK + 2 * M_WIDTH
C_MO = C_MV + M_WIDTH
C_G = C_MO + M_WIDTH
W_COLS = C_G + LANES
N_GATES = 4 * M_HEADS


def _dot(a, b):
    return jnp.dot(a, b, preferred_element_type=f32)


def _dot_nt(a, b):
    return lax.dot_general(a, b, (((1,), (1,)), ((), ())), preferred_element_type=f32)


def _dot_tn(a, b):
    return lax.dot_general(a, b, (((0,), (0,)), ((), ())), preferred_element_type=f32)


def _ones_cols(v):
    return jnp.concatenate([v, jnp.ones((v.shape[0], LANES), f32).astype(bf16)], axis=1)


def _split3(x):
    h = x.astype(bf16)
    r = x - h.astype(f32)
    m = r.astype(bf16)
    l = (r - m.astype(f32)).astype(bf16)
    return h, m, l


def _log_sigmoid(x):
    return jnp.minimum(x, 0.0) - jnp.log1p(jnp.exp(-jnp.abs(x)))


def _silu(x):
    return x / (1.0 + jnp.exp(-x))


def _mod_kernel(c_ref, w_ref, b_ref, o_ref):
    h, m, l = _split3(_silu(c_ref[...]))
    w = w_ref[...]
    wh = w.astype(bf16)
    wl = (w - wh.astype(f32)).astype(bf16)
    acc = _dot(h, wh) + _dot(m, wh) + _dot(h, wl) + _dot(l, wh) + _dot(m, wl)
    o_ref[...] = acc + b_ref[...]


def _modulation(c_all, w_mod, b_mod):
    rows, d = c_all.shape
    cols = w_mod.shape[1]
    tn = 512
    return pl.pallas_call(
        _mod_kernel,
        grid=(cols // tn,),
        in_specs=[pl.BlockSpec((rows, d), lambda j: (0, 0)),
                  pl.BlockSpec((d, tn), lambda j: (0, j)),
                  pl.BlockSpec((1, tn), lambda j: (0, j))],
        out_specs=pl.BlockSpec((rows, tn), lambda j: (0, j)),
        out_shape=jax.ShapeDtypeStruct((rows, cols), f32),
        name="adaln_mod",
    )(c_all, w_mod, b_mod.reshape(1, cols))


def _inproj_kernel(xp_ref, x_ref, xn_ref, shift_ref, scale_ref, nw_ref, w_ref, wvt_ref, cos_ref, sin_ref, qkw_ref,
                   conv_ref, bg_ref,
                   aq_ref, k2_ref, vt_ref, mq_ref, mk_ref, mkt_ref, mv_ref, mo_ref, g_ref, li_ref, lf_ref,
                   conv_scr, *, tm, nt):
    i = pl.program_id(1)
    nw = nw_ref[...]
    sc = 1.0 + scale_ref[0]
    sh = shift_ref[0]

    def prep(xv):
        ms = jnp.mean(xv * xv, axis=-1, keepdims=True)
        return (xv * lax.rsqrt(ms + EPS) * nw) * sc + sh

    hm = prep(x_ref[0])
    lhs = hm.astype(bf16)
    lhs_halo = jnp.concatenate([prep(xp_ref[0]), hm, prep(xn_ref[0])], axis=0).astype(bf16)

    lane = lax.broadcasted_iota(jnp.int32, (1, LANES), 1)
    lo = lane < HEAD_DIM
    first_half = (lane % HEAD_DIM) < (HEAD_DIM // 2)
    cos = cos_ref[...]
    sin = sin_ref[...]
    n_qk = (ATT_WIDTH + K2_WIDTH) // LANES
    vqk = _dot(lhs, w_ref[:, C_Q:C_Q + ATT_WIDTH + K2_WIDTH])
    for grp_i in range(n_qk):
        c0 = grp_i * LANES
        v = vqk[:, c0:c0 + LANES]
        sq = v * v
        s_all = jnp.sum(sq, axis=-1, keepdims=True)
        s_lo = jnp.sum(jnp.where(lo, sq, 0.0), axis=-1, keepdims=True)
        ms = jnp.where(lo, s_lo, s_all - s_lo) * (1.0 / HEAD_DIM)
        nv = v * lax.rsqrt(ms + EPS) * qkw_ref[:, c0:c0 + LANES]
        swapped = jnp.where(first_half, pltpu.roll(nv, LANES - HEAD_DIM // 2, 1), pltpu.roll(nv, HEAD_DIM // 2, 1))
        r = nv * cos + swapped * sin
        if c0 < ATT_WIDTH:
            aq_ref[0, :, c0:c0 + LANES] = (r * Q_SCALE).astype(bf16)
        else:
            k2_ref[0, :, c0 - ATT_WIDTH:c0 - ATT_WIDTH + LANES] = r.astype(bf16)

    vt_ref[0] = _dot_nt(wvt_ref[...], lhs).astype(bf16)
    mv_ref[0] = _dot(lhs, w_ref[:, C_MV:C_MV + M_WIDTH]).astype(bf16)
    mo_ref[0] = _dot(lhs, w_ref[:, C_MO:C_MO + M_WIDTH]).astype(bf16)

    gates = _dot(lhs, w_ref[:, C_G:C_G + LANES]) + bg_ref[...]
    n_streams = 2 * M_HEADS
    is_forget = (lane >= n_streams) & (lane < N_GATES)
    gates = jnp.where(is_forget, _log_sigmoid(gates), gates)
    g_ref[0] = gates[:, :N_GATES]
    for c in range(tm // CHUNK):
        t_c = gates[c * CHUNK:(c + 1) * CHUNK, :].T
        li_ref[0, c * n_streams:(c + 1) * n_streams, :] = t_c[0:n_streams]
        lf_ref[0, c * n_streams:(c + 1) * n_streams, :] = t_c[n_streams:2 * n_streams]

    conv_scr[...] = _dot(lhs_halo, w_ref[:, C_MQK:C_MQK + 2 * M_WIDTH])
    row = lax.broadcasted_iota(jnp.int32, (tm, 1), 0)
    prev = conv_scr[SUBLANES - 1:SUBLANES - 1 + tm, :]
    prev = jnp.where((row == 0) & (i == 0), 0.0, prev)
    nxt = conv_scr[SUBLANES + 1:SUBLANES + 1 + tm, :]
    nxt = jnp.where((row == tm - 1) & (i == nt - 1), 0.0, nxt)
    cur = conv_scr[SUBLANES:SUBLANES + tm, :]
    u = prev * conv_ref[0:1, :] + cur * conv_ref[1:2, :] + nxt * conv_ref[2:3, :]
    u = _silu(u)
    mq_ref[0] = (u[:, :M_WIDTH] * (M_DIM ** -0.5)).astype(bf16)
    mk = u[:, M_WIDTH:]
    mk_ref[0] = mk.astype(bf16)
    mkt_ref[0] = mk.T.astype(bf16)


def _in_projection(x, shift, scale, norm_w, w_all, cos_t, sin_t, qk_w, conv_qk, bg, *, tm):
    B, N, D = x.shape
    nt = N // tm
    hb = tm // SUBLANES
    nblk8 = N // SUBLANES
    kern = functools.partial(_inproj_kernel, tm=tm, nt=nt)
    gate_rows = (tm // CHUNK) * 2 * M_HEADS
    row_map = lambda b, i: (b, i, 0)
    const2 = lambda b, i: (0, 0)
    outs = pl.pallas_call(
        kern,
        grid=(B, nt),
        in_specs=[
            pl.BlockSpec((1, SUBLANES, D), lambda b, i: (b, jnp.maximum(i * hb - 1, 0), 0)),
            pl.BlockSpec((1, tm, D), row_map),
            pl.BlockSpec((1, SUBLANES, D), lambda b, i: (b, jnp.minimum((i + 1) * hb, nblk8 - 1), 0)),
            pl.BlockSpec((1, 1, D), lambda b, i: (b, 0, 0)),
            pl.BlockSpec((1, 1, D), lambda b, i: (b, 0, 0)),
            pl.BlockSpec((1, D), const2),
            pl.BlockSpec((D, W_COLS), const2),
            pl.BlockSpec((KV_WIDTH, D), const2),
            pl.BlockSpec((tm, LANES), lambda b, i: (i, 0)),
            pl.BlockSpec((tm, LANES), lambda b, i: (i, 0)),
            pl.BlockSpec((1, ATT_WIDTH + K2_WIDTH), const2),
            pl.BlockSpec((CONV_W, 2 * M_WIDTH), const2),
            pl.BlockSpec((1, LANES), const2),
        ],
        out_specs=[
            pl.BlockSpec((1, tm, ATT_WIDTH), row_map),
            pl.BlockSpec((1, tm, K2_WIDTH), row_map),
            pl.BlockSpec((1, KV_WIDTH, tm), lambda b, i: (b, 0, i)),
            pl.BlockSpec((1, tm, M_WIDTH), row_map),
            pl.BlockSpec((1, tm, M_WIDTH), row_map),
            pl.BlockSpec((1, M_WIDTH, tm), lambda b, i: (b, 0, i)),
            pl.BlockSpec((1, tm, M_WIDTH), row_map),
            pl.BlockSpec((1, tm, M_WIDTH), row_map),
            pl.BlockSpec((1, tm, N_GATES), row_map),
            pl.BlockSpec((1, gate_rows, CHUNK), row_map),
            pl.BlockSpec((1, gate_rows, CHUNK), row_map),
        ],
        out_shape=[
            jax.ShapeDtypeStruct((B, N, ATT_WIDTH), bf16),
            jax.ShapeDtypeStruct((B, N, K2_WIDTH), bf16),
            jax.ShapeDtypeStruct((B, KV_WIDTH, N), bf16),
            jax.ShapeDtypeStruct((B, N, M_WIDTH), bf16),
            jax.ShapeDtypeStruct((B, N, M_WIDTH), bf16),
            jax.ShapeDtypeStruct((B, M_WIDTH, N), bf16),
            jax.ShapeDtypeStruct((B, N, M_WIDTH), bf16),
            jax.ShapeDtypeStruct((B, N, M_WIDTH), bf16),
            jax.ShapeDtypeStruct((B, N, N_GATES), f32),
            jax.ShapeDtypeStruct((B, nt * gate_rows, CHUNK), f32),
            jax.ShapeDtypeStruct((B, nt * gate_rows, CHUNK), f32),
        ],
        scratch_shapes=[pltpu.VMEM((tm + 2 * SUBLANES, 2 * M_WIDTH), f32)],
        compiler_params=pltpu.CompilerParams(
            dimension_semantics=("parallel", "parallel"), vmem_limit_bytes=VMEM_LIMIT_BYTES),
        name="in_projection",
    )(x, x, x, shift, scale, norm_w, w_all, w_all[:, C_V:C_V + KV_WIDTH].T, cos_t, sin_t, qk_w, conv_qk, bg)
    return outs


def _attn_kernel(sink_ref, q_ref, kp_ref, kc_ref, kn_ref, vp_ref, vc_ref, vn_ref, ck_ref, cv_ref, o_ref, *, nb):
    n = pl.program_id(1)
    span = BLOCK + 2 * WINDOW
    grp = N_HEADS // N_KV_HEADS
    lane = lax.broadcasted_iota(jnp.int32, (1, LANES), 1)
    lo = lane < HEAD_DIM
    kj = lax.broadcasted_iota(jnp.int32, (span, BLOCK), 0)
    qi = lax.broadcasted_iota(jnp.int32, (span, BLOCK), 1)
    rel = kj - qi
    valid = (rel >= 0) & (rel <= 2 * WINDOW)
    valid = valid & ((kj >= WINDOW) | (n > 0)) & ((kj < WINDOW + BLOCK) | (n < nb - 1))
    bias = jnp.where(valid, 0.0, NEG)
    bias4 = jnp.concatenate([bias] * grp, axis=1)
    col = lax.broadcasted_iota(jnp.int32, (1, grp * BLOCK), 1)
    n_ctx = ck_ref.shape[1]
    ones_rows = jnp.ones((HEAD_DIM, span + n_ctx), f32).astype(bf16)
    zero = jnp.zeros((), bf16)
    scores = []
    for kh in range(N_KV_HEADS):
        ks = slice(kh * LANES, (kh + 1) * LANES)
        keys = jnp.concatenate([kp_ref[0, :, ks], kc_ref[0, :, ks], kn_ref[0, :, ks], ck_ref[0, :, ks]], axis=0)
        q0 = kh * grp * HEAD_DIM
        q2 = jnp.concatenate([q_ref[0, :, q0:q0 + LANES], q_ref[0, :, q0 + LANES:q0 + 2 * LANES]], axis=0)
        q4 = jnp.concatenate([jnp.where(lo, q2, zero), jnp.where(lo, zero, q2)], axis=0)
        scores.append(_dot_nt(keys, q4))
    probs = []
    for kh in range(N_KV_HEADS):
        heads = (0, 2, 1, 3)
        sk = sink_ref[kh * grp + heads[0]]
        for c, g in enumerate(heads[1:], start=1):
            sk = jnp.where(col >= c * BLOCK, sink_ref[kh * grp + g], sk)
        sk = sk * LOG2E
        s_t = scores[kh]
        s_loc = s_t[:span] + bias4
        s_ctx = s_t[span:]
        m = jnp.maximum(jnp.maximum(jnp.max(s_loc, axis=0, keepdims=True),
                                    jnp.max(s_ctx, axis=0, keepdims=True)), sk)
        p_t = jnp.concatenate([jnp.exp2(s_loc - m), jnp.exp2(s_ctx - m)], axis=0).astype(bf16)
        probs.append((p_t, jnp.exp2(sk - m)))
    for kh in range(N_KV_HEADS):
        hs = slice(kh * HEAD_DIM, (kh + 1) * HEAD_DIM)
        vals_t = jnp.concatenate([vp_ref[0, hs, :], vc_ref[0, hs, :], vn_ref[0, hs, :], cv_ref[0, hs, :]], axis=1)
        vals_t = jnp.concatenate([vals_t, ones_rows], axis=0)
        q0 = kh * grp * HEAD_DIM
        p_t, p_sink = probs[kh]
        o_t = _dot(vals_t, p_t)
        out_t = o_t[:HEAD_DIM] / (o_t[HEAD_DIM:] + p_sink)
        for pair in range(2):
            even = out_t[:, pair * BLOCK:(pair + 1) * BLOCK]
            odd = out_t[:, (2 + pair) * BLOCK:(3 + pair) * BLOCK]
            both = jnp.concatenate([even, odd], axis=0)
            o_ref[0, :, q0 + pair * LANES:q0 + (pair + 1) * LANES] = both.T.astype(o_ref.dtype)


def _attention(sink, q, k2, vt, ck2, cvt):
    B, N, _ = q.shape
    L = ck2.shape[1]
    nb = N // BLOCK
    kern = functools.partial(_attn_kernel, nb=nb)
    prev_blk = lambda n: jnp.maximum(n - 1, 0)
    next_blk = lambda n: jnp.minimum(n + 1, nb - 1)
    same = lambda n: n
    cur_map = lambda b, n, s: (b, n, 0)
    ctx_map = lambda b, n, s: (b, 0, 0)
    k_spec = lambda f: pl.BlockSpec((1, BLOCK, K2_WIDTH), lambda b, n, s: (b, f(n), 0))
    v_spec = lambda f: pl.BlockSpec((1, KV_WIDTH, BLOCK), lambda b, n, s: (b, 0, f(n)))
    return pl.pallas_call(
        kern,
        grid_spec=pltpu.PrefetchScalarGridSpec(
            num_scalar_prefetch=1,
            grid=(B, nb),
            in_specs=[pl.BlockSpec((1, BLOCK, ATT_WIDTH), cur_map),
                      k_spec(prev_blk), k_spec(same), k_spec(next_blk),
                      v_spec(prev_blk), v_spec(same), v_spec(next_blk),
                      pl.BlockSpec((1, L, K2_WIDTH), ctx_map),
                      pl.BlockSpec((1, KV_WIDTH, L), ctx_map)],
            out_specs=pl.BlockSpec((1, BLOCK, ATT_WIDTH), cur_map),
        ),
        out_shape=jax.ShapeDtypeStruct((B, N, ATT_WIDTH), bf16),
        compiler_params=pltpu.CompilerParams(dimension_semantics=("parallel", "parallel")),
        name="window_attention",
    )(sink, q, k2, k2, k2, vt, vt, vt, ck2, cvt)


def _tri_cumsum_cols(tri, x):
    h, m, l = _split3(x)
    return _dot(tri, h) + _dot(tri, m) + _dot(tri, l)


def _tri_cumsum_rows(x, tri_t):
    h, m, l = _split3(x)
    return _dot(h, tri_t) + _dot(m, tri_t) + _dot(l, tri_t)


def _tri(t):
    r = lax.broadcasted_iota(jnp.int32, (t, t), 0)
    c = lax.broadcasted_iota(jnp.int32, (t, t), 1)
    return jnp.where(c <= r, 1.0, 0.0).astype(bf16), jnp.where(r <= c, 1.0, 0.0).astype(bf16)


def _ctx_state_kernel(k_ref, v_ref, li_ref, lf_ref, c_ref, m_ref):
    L = k_ref.shape[1]
    tri, _ = _tri(L)
    li = li_ref[0]
    lf = lf_ref[0]
    lane = lax.broadcasted_iota(jnp.int32, (1, 2 * M_HEADS), 1)
    fwd = lane < M_HEADS
    cs = _tri_cumsum_cols(tri, lf)
    tot = cs[L - 1:L, :]
    b = jnp.where(fwd, cs, tot - cs + lf)
    w = tot - b + li
    m_new = jnp.maximum(tot, jnp.max(w, axis=0, keepdims=True))
    ws = jnp.exp(w - m_new)
    m_ref[0] = m_new
    for c in range(2 * M_HEADS):
        hs = slice((c % M_HEADS) * M_DIM, (c % M_HEADS + 1) * M_DIM)
        ks = k_ref[0, :, hs].astype(f32) * ws[:, c:c + 1]
        c_ref[0, c] = _dot_tn(ks.astype(bf16), _ones_cols(v_ref[0, :, hs]))


def _ctx_states(cmk, cmv, li_c, lf_c):
    B, L, _ = cmk.shape
    S = 2 * M_HEADS
    bmap = lambda b: (b, 0, 0)
    return pl.pallas_call(
        _ctx_state_kernel,
        grid=(B,),
        in_specs=[pl.BlockSpec((1, L, M_WIDTH), bmap), pl.BlockSpec((1, L, M_WIDTH), bmap),
                  pl.BlockSpec((1, L, S), bmap), pl.BlockSpec((1, L, S), bmap)],
        out_specs=[pl.BlockSpec((1, S, M_DIM, M_DIM + LANES), lambda b: (b, 0, 0, 0)),
                   pl.BlockSpec((1, 1, S), bmap)],
        out_shape=[jax.ShapeDtypeStruct((B, S, M_DIM, M_DIM + LANES), f32),
                   jax.ShapeDtypeStruct((B, 1, S), f32)],
        compiler_params=pltpu.CompilerParams(dimension_semantics=("parallel",)),
        name="mlstm_ctx_state",
    )(cmk, cmv, li_c, lf_c)


N_GATE_STATS = 6


def _gate_prep_kernel(li_ref, lf_ref, o_ref):
    S = 2 * M_HEADS
    T = CHUNK
    li = li_ref[0]
    lf = lf_ref[0]
    rows = li.shape[0]
    _, tri_t = _tri(T)
    fwd = (lax.broadcasted_iota(jnp.int32, (rows, 1), 0) % S) < M_HEADS
    lane = lax.broadcasted_iota(jnp.int32, (1, T), 1)
    cs = _tri_cumsum_rows(lf, tri_t)
    tot = cs[:, T - 1:T]
    b = jnp.where(fwd, cs, tot - cs + lf)
    e = li - b
    pm = e
    shift = 1
    while shift < T:
        from_left = jnp.where(lane >= shift, pltpu.roll(pm, shift, 1), -jnp.inf)
        from_right = jnp.where(lane < T - shift, pltpu.roll(pm, T - shift, 1), -jnp.inf)
        pm = jnp.maximum(pm, jnp.where(fwd, from_left, from_right))
        shift *= 2
    w = tot - b + li
    o_ref[0, 0] = e
    o_ref[0, 1] = pm
    o_ref[0, 2] = b
    o_ref[0, 3] = w
    o_ref[0, 4] = jnp.broadcast_to(tot, (rows, T))
    o_ref[0, 5] = jnp.broadcast_to(jnp.max(w, axis=-1, keepdims=True), (rows, T))


def _gate_prep(li_rc, lf_rc):
    B, rows, T = li_rc.shape
    bmap = lambda b: (b, 0, 0)
    return pl.pallas_call(
        _gate_prep_kernel,
        grid=(B,),
        in_specs=[pl.BlockSpec((1, rows, T), bmap), pl.BlockSpec((1, rows, T), bmap)],
        out_specs=pl.BlockSpec((1, N_GATE_STATS, rows, T), lambda b: (b, 0, 0, 0)),
        out_shape=jax.ShapeDtypeStruct((B, N_GATE_STATS, rows, T), f32),
        compiler_params=pltpu.CompilerParams(dimension_semantics=("parallel",)),
        name="mlstm_gate_prep",
    )(li_rc, lf_rc)


def _mlstm_kernel(qf_ref, ktf_ref, vf_ref, qb_ref, ktb_ref, vb_ref, gf_ref, gb_ref,
                  cn0_ref, m0_ref, hf_ref, hb_ref, cn_scr, m_scr):
    j = pl.program_id(1)
    S = 2 * M_HEADS
    T = CHUNK

    @pl.when(j == 0)
    def _():
        cn_scr[...] = cn0_ref[0]
        m_scr[...] = m0_ref[0]

    fwd_r = lax.broadcasted_iota(jnp.int32, (S, 1), 0) < M_HEADS
    stat = lambda k: jnp.where(fwd_r, gf_ref[0, k], gb_ref[0, k])
    e_r, pm, b_r, w_r, tot, w_max = [stat(k) for k in range(N_GATE_STATS)]

    m_prev = m_scr[...]
    g_hi = jnp.maximum(m_prev, pm).astype(bf16)
    g_used = g_hi.astype(f32)
    en_r = jnp.exp(-(b_r + g_used))
    en_hi = en_r.astype(bf16)
    en_lo = (en_r - en_hi.astype(f32)).astype(bf16)
    m_new = jnp.maximum(tot + m_prev, w_max)
    a_r = jnp.exp(tot + m_prev - m_new)
    ws_r = jnp.exp(w_r - m_new)
    m_scr[...] = m_new

    ti = lax.broadcasted_iota(jnp.int32, (T, T), 0)
    si = lax.broadcasted_iota(jnp.int32, (T, T), 1)
    eye = ti == si
    en_hi = en_hi.astype(f32)
    en_lo = en_lo.astype(f32)
    ones_tt = jnp.ones((T, T), f32).astype(bf16)
    zeros_tt = jnp.zeros((T, T), f32).astype(bf16)
    ones_2t = jnp.ones((2 * T, T), f32).astype(bf16)

    def refs(c):
        is_fwd = c < M_HEADS
        hs = slice((c % M_HEADS) * M_DIM, (c % M_HEADS + 1) * M_DIM)
        return ((qf_ref if is_fwd else qb_ref), (ktf_ref if is_fwd else ktb_ref), (vf_ref if is_fwd else vb_ref),
                (hf_ref if is_fwd else hb_ref), hs, is_fwd)

    stage1 = []
    for c in range(S):
        q_ref, kt_ref, _, _, hs, _ = refs(c)
        row = lambda a: a[c:c + 1, :]
        diag = lambda a: jnp.where(eye, row(a), 0.0).astype(bf16)
        q = q_ref[0, :, hs]
        kt = kt_ref[0, hs, :]
        qk_g = _dot(jnp.concatenate([q, diag(g_used)], axis=1),
                    jnp.concatenate([jnp.concatenate([kt, zeros_tt], axis=1),
                                     jnp.concatenate([zeros_tt, ones_tt], axis=1)], axis=0))
        en_rep = _dot(jnp.concatenate([diag(en_hi), diag(en_lo)], axis=1), ones_2t)
        qc = _dot(q, cn_scr[c].astype(bf16))
        stage1.append((qk_g, en_rep, qc))
    stage2 = []
    for c in range(S):
        _, _, v_ref, _, hs, is_fwd = refs(c)
        row = lambda a: a[c:c + 1, :]
        qk_g, en_rep, qc = stage1[c]
        g_rep = qk_g[:, T:]
        within = (si <= ti) if is_fwd else (si >= ti)
        p_mat = jnp.exp(jnp.where(within, row(e_r) - g_rep, -jnp.inf))
        wts = p_mat * qk_g[:, :T]
        decay = jnp.exp(row(m_prev) - g_rep)
        nd = _dot(wts.astype(bf16), _ones_cols(v_ref[0, :, hs]))
        stage2.append((nd, decay))
    for c in range(S):
        _, _, _, h_ref, hs, _ = refs(c)
        _, en_rep, qc = stage1[c]
        nd, decay = stage2[c]
        num = nd[:, :M_DIM] + decay * qc[:, :M_DIM]
        den = nd[:, M_DIM:] + decay * qc[:, M_DIM:]
        h = num / jnp.maximum(jnp.abs(den), en_rep)
        h_ref[0, :, hs] = h.astype(h_ref.dtype)
    for c in range(S):
        _, kt_ref, v_ref, _, hs, _ = refs(c)
        row = lambda a: a[c:c + 1, :]
        kst = (kt_ref[0, hs, :].astype(f32) * row(ws_r)).astype(bf16)
        a_c = row(a_r)
        cn_scr[c] = jnp.concatenate([a_c, a_c], axis=1) * cn_scr[c] + _dot(kst, _ones_cols(v_ref[0, :, hs]))


def _mlstm(mq, mkt, mv, gstats, cn0, m0):
    B, N, _ = mq.shape
    nc = N // CHUNK
    S = 2 * M_HEADS
    fmap = lambda b, j: (b, j, 0)
    bmap = lambda b, j: (b, nc - 1 - j, 0)
    frow = lambda b, j: (b, 0, j)
    brow = lambda b, j: (b, 0, nc - 1 - j)
    seq = lambda m: pl.BlockSpec((1, CHUNK, M_WIDTH), m)
    seq_t = lambda m: pl.BlockSpec((1, M_WIDTH, CHUNK), m)
    gspec = lambda m: pl.BlockSpec((1, N_GATE_STATS, S, CHUNK), m)
    return pl.pallas_call(
        _mlstm_kernel,
        grid=(B, nc),
        in_specs=[seq(fmap), seq_t(frow), seq(fmap), seq(bmap), seq_t(brow), seq(bmap),
                  gspec(lambda b, j: (b, 0, j, 0)), gspec(lambda b, j: (b, 0, nc - 1 - j, 0)),
                  pl.BlockSpec((1, S, M_DIM, M_DIM + LANES), lambda b, j: (b, 0, 0, 0)),
                  pl.BlockSpec((1, S, CHUNK), lambda b, j: (b, 0, 0))],
        out_specs=[seq(fmap), seq(bmap)],
        out_shape=[jax.ShapeDtypeStruct((B, N, M_WIDTH), bf16), jax.ShapeDtypeStruct((B, N, M_WIDTH), bf16)],
        scratch_shapes=[pltpu.VMEM((S, M_DIM, M_DIM + LANES), f32), pltpu.VMEM((S, CHUNK), f32)],
        compiler_params=pltpu.CompilerParams(dimension_semantics=("parallel", "arbitrary")),
        name="mlstm_scan",
    )(mq, mkt, mv, mq, mkt, mv, gstats, gstats, cn0, m0)


def _outproj_kernel(x_ref, att_ref, hf_ref, hb_ref, mo_ref, mnw_ref, wo_ref, g1_ref, n2w_ref, sh2_ref, sc2_ref,
                    wrh_ref, wrl_ref, x1_ref, h2_ref, aff_ref, affr_ref, *, tm):
    parts = []
    for h in range(M_HEADS):
        hs = slice(h * M_DIM, (h + 1) * M_DIM)
        s = hf_ref[0, :, hs].astype(f32) + hb_ref[0, :, hs].astype(f32)
        ms = jnp.mean(s * s, axis=-1, keepdims=True)
        hn = s * lax.rsqrt(ms + EPS) * mnw_ref[:, hs]
        gate = 1.0 / (1.0 + jnp.exp(-mo_ref[0, :, hs].astype(f32)))
        parts.append((hn * gate).astype(bf16))
    ml = jnp.concatenate(parts, axis=-1)
    proj = _dot(att_ref[0], wo_ref[0:ATT_WIDTH, :]) + _dot(ml, wo_ref[ATT_WIDTH:, :])
    x1 = x_ref[0] + g1_ref[0] * proj
    x1_ref[0] = x1
    ms = jnp.mean(x1 * x1, axis=-1, keepdims=True)
    h2 = (x1 * lax.rsqrt(ms + EPS) * n2w_ref[...]) * (1.0 + sc2_ref[0]) + sh2_ref[0]
    for j in range(SUBLANES):
        h2_ref[0, pl.ds(j, tm, stride=SUBLANES), :] = h2[:, j * LANES:(j + 1) * LANES]
    hh = h2.astype(bf16)
    hl = (h2 - hh.astype(f32)).astype(bf16)
    wrh = wrh_ref[...]
    logits = _dot_nt(wrh, hh) + _dot_nt(wrh, hl) + _dot_nt(wrl_ref[...], hh)
    mx = jnp.max(logits, axis=0, keepdims=True)
    e = jnp.exp(logits - mx)
    aff = e / jnp.sum(e, axis=0, keepdims=True)
    aff_ref[0] = aff
    n_exp = aff.shape[0]
    affr_ref[0] = jnp.concatenate([aff.T, jnp.zeros((tm, LANES - n_exp), f32)], axis=1)


def _out_projection(x, att, hf, hb, mo, mnw, w_out, g1, n2w, sh2, sc2, wr_hi, wr_lo, *, tm):
    B, N, D = x.shape
    E = wr_hi.shape[0]
    row_map = lambda b, i: (b, i, 0)
    const2 = lambda b, i: (0, 0)
    bvec = lambda b, i: (b, 0, 0)
    return pl.pallas_call(
        functools.partial(_outproj_kernel, tm=tm),
        grid=(B, N // tm),
        in_specs=[pl.BlockSpec((1, tm, D), row_map),
                  pl.BlockSpec((1, tm, ATT_WIDTH), row_map),
                  pl.BlockSpec((1, tm, M_WIDTH), row_map),
                  pl.BlockSpec((1, tm, M_WIDTH), row_map),
                  pl.BlockSpec((1, tm, M_WIDTH), row_map),
                  pl.BlockSpec((1, M_WIDTH), const2),
                  pl.BlockSpec((ATT_WIDTH + M_WIDTH, D), const2),
                  pl.BlockSpec((1, 1, D), bvec),
                  pl.BlockSpec((1, D), const2),
                  pl.BlockSpec((1, 1, D), bvec),
                  pl.BlockSpec((1, 1, D), bvec),
                  pl.BlockSpec((E, D), const2),
                  pl.BlockSpec((E, D), const2)],
        out_specs=[pl.BlockSpec((1, tm, D), row_map),
                   pl.BlockSpec((1, tm * SUBLANES, LANES), row_map),
                   pl.BlockSpec((1, E, tm), lambda b, i: (b, 0, i)),
                   pl.BlockSpec((1, tm, LANES), row_map)],
        out_shape=[jax.ShapeDtypeStruct((B, N, D), f32),
                   jax.ShapeDtypeStruct((B, N * SUBLANES, LANES), f32),
                   jax.ShapeDtypeStruct((B, E, N), f32),
                   jax.ShapeDtypeStruct((B, N, LANES), f32)],
        compiler_params=pltpu.CompilerParams(
            dimension_semantics=("parallel", "parallel"), vmem_limit_bytes=VMEM_LIMIT_BYTES),
        name="out_projection",
    )(x, att, hf, hb, mo, mnw, w_out, g1, n2w, sh2, sc2, wr_hi, wr_lo)


def _chunk_stride(cap):
    return cap + SUBLANES


ROUTE_EXPONENT_BITS = (64, 32, 16, 8, 4, 2, 1)
ROUTE_REFINE_STEPS = 25
ROUTE_TINY = 1e-30


def _route_kernel(aff_ref, idx_ref, cl_scr, *, cap):
    E, N = aff_ref.shape[1], aff_ref.shape[2]
    aff = aff_ref[0]

    def count_above(t):
        return jnp.sum(jnp.where(aff > t, 1.0, 0.0), axis=-1, keepdims=True)

    hi = jnp.full((E, 1), 2.0, f32)
    for bit in ROUTE_EXPONENT_BITS:
        cand = hi * (2.0 ** -bit)
        hi = jnp.where(count_above(cand) < cap, cand, hi)
    lo = jnp.where(hi < ROUTE_TINY, -1.0, 0.5 * hi)
    for _ in range(ROUTE_REFINE_STEPS):
        q = 0.25 * (hi - lo)
        m1, m2, m3 = lo + q, lo + 2.0 * q, lo + 3.0 * q
        ok1, ok2, ok3 = [count_above(m) >= cap for m in (m1, m2, m3)]
        lo, hi = (jnp.where(ok3, m3, jnp.where(ok2, m2, jnp.where(ok1, m1, lo))),
                  jnp.where(ok3, hi, jnp.where(ok2, m3, jnp.where(ok1, m2, m1))))
    above = jnp.where(aff > hi, 1.0, 0.0)
    equal = jnp.where(aff > lo, 1.0, 0.0) - above
    need = cap - jnp.sum(above, axis=-1, keepdims=True)
    n_chunks = N // LANES
    assert n_chunks <= LANES
    _, tri_t = _tri(LANES)
    lane = lax.broadcasted_iota(jnp.int32, (1, LANES), 1)
    both = jnp.concatenate([above, equal], axis=0).astype(bf16)
    run = jnp.zeros((2 * E, 1), f32)
    start = jnp.zeros((E, 1), f32)
    ends = jnp.full((E, LANES), float(2 * N), f32)
    cl_scr[...] = jnp.zeros_like(cl_scr)
    for k in range(n_chunks):
        ck = _dot(both[:, k * LANES:(k + 1) * LANES], tri_t)
        count = (ck[:E] + run[:E]) + jnp.minimum(ck[E:] + run[E:], need)
        local = count - start
        for e in range(E):
            cl_scr[e, k:k + 1, :] = local[e:e + 1, :]
        start = count[:, LANES - 1:LANES]
        ends = jnp.where(lane == k, start, ends)
        run = run + ck[:, LANES - 1:LANES]

    slot = lax.broadcasted_iota(jnp.int32, (cap, 1), 0).astype(f32)
    for e in range(E):
        ends_e = ends[e:e + 1, :]
        before = ends_e <= slot
        chunk = jnp.sum(jnp.where(before, 1.0, 0.0), axis=-1, keepdims=True)
        chunk_start = jnp.max(jnp.where(before, ends_e, 0.0), axis=-1, keepdims=True)
        pick = jnp.where(lane.astype(f32) == chunk, 1.0, 0.0).astype(bf16)
        local = _dot(pick, cl_scr[e].astype(bf16))
        pos = jnp.sum(jnp.where(local <= slot - chunk_start, 1.0, 0.0), axis=-1, keepdims=True)
        idx_ref[0, e] = (chunk * float(LANES) + pos).astype(jnp.int32)


def _route(aff_t, cap):
    B, E, N = aff_t.shape
    return pl.pallas_call(
        functools.partial(_route_kernel, cap=cap),
        grid=(B,),
        in_specs=[pl.BlockSpec((1, E, N), lambda b: (b, 0, 0))],
        out_specs=pl.BlockSpec((1, E, cap, 1), lambda b: (b, 0, 0, 0)),
        out_shape=jax.ShapeDtypeStruct((B, E, cap, 1), jnp.int32),
        scratch_shapes=[pltpu.VMEM((E, LANES, LANES), f32)],
        compiler_params=pltpu.CompilerParams(dimension_semantics=("parallel",)),
        name="expert_choice_route",
    )(aff_t)


SCATTER_UNROLL = 8
EXPERT_VMEM_LIMIT_BYTES = 60 * 1024 * 1024


def _expert_kernel(idx_prev_ref, idx_ref, idx_next_ref, src_ref, affr_ref, wg_ref, wu_ref, wd_ref, acc_ref,
                   x_even, x_odd, g_even, g_odd, y_even, y_odd, *, cap, n_experts):
    e = pl.program_id(1)
    stride = _chunk_stride(cap)

    def gather(ids_ref, x_scr, g_scr):
        for i in range(cap):
            t = ids_ref[0, 0, i]
            x_scr[pl.ds(i, SUBLANES, stride=stride), :] = (
                src_ref[0, pl.ds(pl.multiple_of(t * SUBLANES, SUBLANES), SUBLANES), :])
            g_scr[pl.ds(i, 1), :] = affr_ref[0, pl.ds(t, 1), :]

    def ffn(x_scr, g_scr, y_scr):
        xg = jnp.concatenate([x_scr[j * stride:j * stride + cap, :].astype(bf16) for j in range(SUBLANES)], axis=1)
        lane = lax.broadcasted_iota(jnp.int32, (1, LANES), 1)
        gate = jnp.sum(jnp.where(lane == e, g_scr[...], 0.0), axis=-1, keepdims=True)
        g = _dot(xg, wg_ref[0])
        u = _dot(xg, wu_ref[0])
        hmid = (_silu(g) * u).astype(bf16)
        y = _dot(hmid, wd_ref[0]) * gate
        for j in range(y.shape[1] // LANES):
            y_scr[j * stride:j * stride + cap, :] = y[:, j * LANES:(j + 1) * LANES]

    def scatter(ids_ref, y_scr):
        for i0 in range(0, cap, SCATTER_UNROLL):
            rows = [pl.multiple_of(ids_ref[0, 0, i0 + u] * SUBLANES, SUBLANES) for u in range(SCATTER_UNROLL)]
            vals = [acc_ref[0, pl.ds(rows[u], SUBLANES), :] + y_scr[pl.ds(i0 + u, SUBLANES, stride=stride), :]
                    for u in range(SCATTER_UNROLL)]
            for u in range(SCATTER_UNROLL):
                acc_ref[0, pl.ds(rows[u], SUBLANES), :] = vals[u]

    @pl.when(e == 0)
    def _():
        acc_ref[...] = jnp.zeros_like(acc_ref)
        y_odd[...] = jnp.zeros_like(y_odd)
        gather(idx_ref, x_even, g_even)

    @pl.when(e % 2 == 0)
    def _():
        gather(idx_next_ref, x_odd, g_odd)
        ffn(x_even, g_even, y_even)
        scatter(idx_prev_ref, y_odd)

    @pl.when(e % 2 == 1)
    def _():
        gather(idx_next_ref, x_even, g_even)
        ffn(x_odd, g_odd, y_odd)
        scatter(idx_prev_ref, y_even)

    @pl.when(e == n_experts - 1)
    def _():
        scatter(idx_ref, y_odd if (n_experts - 1) % 2 else y_even)


def _expert_mixture(idx, h2_slab, aff_rows, wg, wu, wd):
    B, E, cap = idx.shape
    D, FF = wg.shape[1], wg.shape[2]
    rows = (D // LANES) * _chunk_stride(cap)
    wmap = lambda b, e: (e, 0, 0)
    bmap = lambda b, e: (b, 0, 0)
    ids = idx.reshape(B * E, 1, cap)
    ids_spec = lambda m: pl.BlockSpec((1, 1, cap), m, memory_space=pltpu.SMEM)
    return pl.pallas_call(
        functools.partial(_expert_kernel, cap=cap, n_experts=E),
        grid=(B, E),
        in_specs=[ids_spec(lambda b, e: (b * E + jnp.maximum(e - 1, 0), 0, 0)),
                  ids_spec(lambda b, e: (b * E + e, 0, 0)),
                  ids_spec(lambda b, e: (b * E + jnp.minimum(e + 1, E - 1), 0, 0)),
                  pl.BlockSpec((1,) + h2_slab.shape[1:], bmap, pipeline_mode=pl.Buffered(1)),
                  pl.BlockSpec((1,) + aff_rows.shape[1:], bmap, pipeline_mode=pl.Buffered(1)),
                  pl.BlockSpec((1, D, FF), wmap),
                  pl.BlockSpec((1, D, FF), wmap),
                  pl.BlockSpec((1, FF, D), wmap)],
        out_specs=pl.BlockSpec((1,) + h2_slab.shape[1:], bmap, pipeline_mode=pl.Buffered(1)),
        out_shape=jax.ShapeDtypeStruct(h2_slab.shape, f32),
        scratch_shapes=[pltpu.VMEM((rows, LANES), f32), pltpu.VMEM((rows, LANES), f32),
                        pltpu.VMEM((cap, LANES), f32), pltpu.VMEM((cap, LANES), f32),
                        pltpu.VMEM((rows, LANES), f32), pltpu.VMEM((rows, LANES), f32)],
        compiler_params=pltpu.CompilerParams(
            dimension_semantics=("parallel", "arbitrary"), vmem_limit_bytes=EXPERT_VMEM_LIMIT_BYTES),
        name="expert_mixture",
    )(ids, ids, ids, h2_slab, aff_rows, wg, wu, wd)


def _final_kernel(x1_ref, g2_ref, acc_ref, o_ref, *, tm):
    ffn = jnp.concatenate([acc_ref[0, pl.ds(j, tm, stride=SUBLANES), :] for j in range(SUBLANES)], axis=-1)
    o_ref[0] = x1_ref[0] + g2_ref[0] * ffn


def _final_residual(x1, g2, acc, *, tm):
    B, N, D = x1.shape
    row_map = lambda b, i: (b, i, 0)
    return pl.pallas_call(
        functools.partial(_final_kernel, tm=tm),
        grid=(B, N // tm),
        in_specs=[pl.BlockSpec((1, tm, D), row_map),
                  pl.BlockSpec((1, 1, D), lambda b, i: (b, 0, 0)),
                  pl.BlockSpec((1, tm * SUBLANES, LANES), row_map)],
        out_specs=pl.BlockSpec((1, tm, D), row_map),
        out_shape=jax.ShapeDtypeStruct((B, N, D), f32),
        compiler_params=pltpu.CompilerParams(dimension_semantics=("parallel", "parallel")),
        name="final_residual",
    )(x1, g2, acc)


def _rope_tables(n):
    rows = n // GRID_W
    row, col = jnp.meshgrid(jnp.arange(rows), jnp.arange(GRID_W), indexing='ij')
    n_freq = HEAD_DIM // 4
    freqs = ROPE_BASE ** (-jnp.arange(n_freq, dtype=f32) / n_freq)
    ang = jnp.concatenate([row.reshape(-1, 1).astype(f32) * freqs, col.reshape(-1, 1).astype(f32) * freqs], -1)
    cos, sin = jnp.cos(ang), jnp.sin(ang)
    reps = LANES // HEAD_DIM
    return (jnp.tile(jnp.concatenate([cos, cos], -1), (1, reps)),
            jnp.tile(jnp.concatenate([-sin, sin], -1), (1, reps)))


def _pack_w_in(w_in):
    D = w_in.shape[0]
    o = 0
    aq = w_in[:, o:o + ATT_WIDTH]; o += ATT_WIDTH
    ak = w_in[:, o:o + KV_WIDTH]; o += KV_WIDTH
    av = w_in[:, o:o + KV_WIDTH]; o += KV_WIDTH
    mq = w_in[:, o:o + M_WIDTH]; o += M_WIDTH
    mk = w_in[:, o:o + M_WIDTH]; o += M_WIDTH
    mv = w_in[:, o:o + M_WIDTH]; o += M_WIDTH
    mo = w_in[:, o:o + M_WIDTH]; o += M_WIDTH
    gates = _gate_order(w_in[:, o:o + N_GATES])

    def dup(a):
        a = a.reshape(D, N_KV_HEADS, 1, HEAD_DIM)
        return jnp.broadcast_to(a, (D, N_KV_HEADS, 2, HEAD_DIM)).reshape(D, K2_WIDTH)

    pad = jnp.zeros((D, LANES - N_GATES), w_in.dtype)
    return jnp.concatenate([aq, dup(ak), av, mq, mk, mv, mo, gates, pad], -1).astype(bf16)


def _gate_order(g):
    H = M_HEADS
    return jnp.concatenate([g[..., 0:H], g[..., 2 * H:3 * H], g[..., H:2 * H], g[..., 3 * H:4 * H]], -1)


def _layer(x, ctx, mod_x, mod_c, norm1_w, norm2_w, w_in, b_gates, conv_qk, q_norm_w, k_norm_w, sink,
           mlstm_norm_w, w_out, w_router, w_gate, w_up, w_down):
    B, N, D = x.shape
    L = ctx.shape[1]
    assert D == SUBLANES * LANES, "token rows are handled as one (8, 128) register tile"
    tm = min(512, N)
    sh1, sc1, g1, sh2, sc2, g2 = [m[:, None, :] for m in jnp.split(mod_x, 6, -1)]
    csh1, csc1 = [jnp.broadcast_to(m[None, None, :], (B, 1, D)) for m in jnp.split(mod_c, 6, -1)[:2]]

    w_all = _pack_w_in(w_in)
    qk_w = jnp.concatenate([jnp.tile(q_norm_w, N_HEADS), jnp.tile(k_norm_w, 2 * N_KV_HEADS)])[None, :]
    bg = jnp.concatenate([_gate_order(b_gates), jnp.zeros((LANES - N_GATES,), f32)])[None, :]
    nw1 = norm1_w[None, :]
    cos_t, sin_t = _rope_tables(N)
    ones_t, zeros_t = jnp.ones((L, LANES), f32), jnp.zeros((L, LANES), f32)

    aq, k2, vt, mq, _, mkt, mv, mo, _, li_rc, lf_rc = _in_projection(x, sh1, sc1, nw1, w_all, cos_t, sin_t, qk_w,
                                                                     conv_qk, bg, tm=tm)
    _, ck2, cvt, _, cmk, _, cmv, _, cgts, _, _ = _in_projection(ctx, csh1, csc1, nw1, w_all, ones_t, zeros_t, qk_w,
                                                                conv_qk, bg, tm=L)

    att = _attention(sink, aq, k2, vt, ck2, cvt)

    n_streams = 2 * M_HEADS
    cn0, m0 = _ctx_states(cmk, cmv, cgts[..., :n_streams], cgts[..., n_streams:])
    m0 = jnp.broadcast_to(jnp.swapaxes(m0, 1, 2), (B, n_streams, CHUNK))
    gstats = _gate_prep(li_rc, lf_rc)
    hf, hb = _mlstm(mq, mkt, mv, gstats, cn0, m0)

    wr_t = w_router.T
    wr_hi = wr_t.astype(bf16)
    wr_lo = (wr_t - wr_hi.astype(f32)).astype(bf16)
    x1, h2_slab, aff_t, aff_rows = _out_projection(x, att, hf, hb, mo, mlstm_norm_w[None, :], w_out.astype(bf16),
                                                   g1, norm2_w[None, :], sh2, sc2, wr_hi, wr_lo, tm=tm)

    cap = CAPACITY * N // N_EXPERTS
    idx = _route(aff_t, cap).reshape(B, N_EXPERTS, cap)
    acc = _expert_mixture(idx, h2_slab, aff_rows, w_gate.astype(bf16), w_up.astype(bf16), w_down.astype(bf16))
    return _final_residual(x1, g2, acc, tm=tm)


def kernel(x, c, ctx, c_ctx, w_mod, b_mod, norm1_w, norm2_w, w_in, b_gates, conv_qk, q_norm_w, k_norm_w, sink,
           mlstm_norm_w, w_out, w_router, w_gate, w_up, w_down):
    depth = w_mod.shape[0]
    assert depth == 1, "only the final-layer (no context update) form of the block is implemented"
    B = x.shape[0]
    pad_rows = (-(B + 1)) % SUBLANES
    c_all = jnp.concatenate([c, c_ctx[None, :], jnp.zeros((pad_rows, c.shape[1]), c.dtype)], 0)
    mod = _modulation(c_all, w_mod[0], b_mod[0])
    return _layer(x, ctx, mod[:B], mod[B], norm1_w[0], norm2_w[0], w_in[0], b_gates[0], conv_qk[0], q_norm_w[0],
                  k_norm_w[0], sink[0], mlstm_norm_w[0], w_out[0], w_router[0], w_gate[0], w_up[0], w_down[0])
```

```python
import functools

import jax
import jax.numpy as jnp
from jax import lax
from jax.experimental import pallas as pl
from jax.experimental.pallas import tpu as pltpu

f32 = jnp.float32
bf16 = jnp.bfloat16

GRID_W = 64
N_HEADS = 8
N_KV_HEADS = 2
HEAD_DIM = 64
WINDOW = 128
BLOCK = 128
ROPE_BASE = 10000.0
M_HEADS = 4
M_DIM = 128
CHUNK = 128
CONV_W = 3
ATT_WIDTH = N_HEADS * HEAD_DIM
KV_WIDTH = N_KV_HEADS * HEAD_DIM
M_WIDTH = M_HEADS * M_DIM
N_EXPERTS = 16
CAPACITY = 2
EPS = 1e-6
NEG = -1e30
LOG2E = 1.4426950408889634
Q_SCALE = HEAD_DIM ** -0.5 * LOG2E

LANES = 128
SUBLANES = 8
VMEM_LIMIT_BYTES = 56 * 1024 * 1024
ROW_TILE = 512
MOD_COL_TILE = 512

K2_WIDTH = 2 * KV_WIDTH
C_Q = 0
C_K = C_Q + ATT_WIDTH
C_V = C_K + K2_WIDTH
C_MQK = C_V + KV_WIDTH
C_MV = C_MQK + 2 * M_WIDTH
C_MO = C_MV + M_WIDTH
C_G = C_MO + M_WIDTH
W_COLS = C_G + LANES
N_GATES = 4 * M_HEADS


def _dot(a, b):
    return jnp.dot(a, b, preferred_element_type=f32)


def _dot_nt(a, b):
    return lax.dot_general(a, b, (((1,), (1,)), ((), ())), preferred_element_type=f32)


def _dot_tn(a, b):
    return lax.dot_general(a, b, (((0,), (0,)), ((), ())), preferred_element_type=f32)


def _ones_cols(v):
    return jnp.concatenate([v, jnp.ones((v.shape[0], LANES), f32).astype(bf16)], axis=1)


def _split3(x):
    h = x.astype(bf16)
    r = x - h.astype(f32)
    m = r.astype(bf16)
    l = (r - m.astype(f32)).astype(bf16)
    return h, m, l


def _log_sigmoid(x):
    return jnp.minimum(x, 0.0) - jnp.log1p(jnp.exp(-jnp.abs(x)))


def _silu(x):
    return x / (1.0 + jnp.exp(-x))


def _mod_kernel(c_ref, w_ref, b_ref, o_ref):
    h, m, l = _split3(_silu(c_ref[...]))
    w = w_ref[...]
    wh = w.astype(bf16)
    wl = (w - wh.astype(f32)).astype(bf16)
    acc = _dot(h, wh) + _dot(m, wh) + _dot(h, wl) + _dot(l, wh) + _dot(m, wl)
    o_ref[...] = acc + b_ref[...]


def _modulation(c_all, w_mod, b_mod):
    rows, d = c_all.shape
    cols = w_mod.shape[1]
    tn = MOD_COL_TILE
    return pl.pallas_call(
        _mod_kernel,
        grid=(cols // tn,),
        in_specs=[pl.BlockSpec((rows, d), lambda j: (0, 0)),
                  pl.BlockSpec((d, tn), lambda j: (0, j)),
                  pl.BlockSpec((1, tn), lambda j: (0, j))],
        out_specs=pl.BlockSpec((rows, tn), lambda j: (0, j)),
        out_shape=jax.ShapeDtypeStruct((rows, cols), f32),
        name="adaln_mod",
    )(c_all, w_mod, b_mod.reshape(1, cols))


def _inproj_kernel(xp_ref, x_ref, xn_ref, shift_ref, scale_ref, nw_ref, w_ref, wvt_ref, cos_ref, sin_ref, qkw_ref,
                   conv_ref, bg_ref,
                   aq_ref, k2_ref, vt_ref, mq_ref, mk_ref, mkt_ref, mv_ref, mo_ref, g_ref, li_ref, lf_ref,
                   conv_scr, *, tm, nt):
    i = pl.program_id(1)
    nw = nw_ref[...]
    sc = 1.0 + scale_ref[0]
    sh = shift_ref[0]

    def prep(xv):
        ms = jnp.mean(xv * xv, axis=-1, keepdims=True)
        return (xv * lax.rsqrt(ms + EPS) * nw) * sc + sh

    hm = prep(x_ref[0])
    lhs = hm.astype(bf16)
    lhs_halo = jnp.concatenate([prep(xp_ref[0]), hm, prep(xn_ref[0])], axis=0).astype(bf16)

    lane = lax.broadcasted_iota(jnp.int32, (1, LANES), 1)
    lo = lane < HEAD_DIM
    first_half = (lane % HEAD_DIM) < (HEAD_DIM // 2)
    cos = cos_ref[...]
    sin = sin_ref[...]
    n_qk = (ATT_WIDTH + K2_WIDTH) // LANES
    vqk = _dot(lhs, w_ref[:, C_Q:C_Q + ATT_WIDTH + K2_WIDTH])
    for grp_i in range(n_qk):
        c0 = grp_i * LANES
        v = vqk[:, c0:c0 + LANES]
        sq = v * v
        s_all = jnp.sum(sq, axis=-1, keepdims=True)
        s_lo = jnp.sum(jnp.where(lo, sq, 0.0), axis=-1, keepdims=True)
        ms = jnp.where(lo, s_lo, s_all - s_lo) * (1.0 / HEAD_DIM)
        nv = v * lax.rsqrt(ms + EPS) * qkw_ref[:, c0:c0 + LANES]
        swapped = jnp.where(first_half, pltpu.roll(nv, LANES - HEAD_DIM // 2, 1), pltpu.roll(nv, HEAD_DIM // 2, 1))
        r = nv * cos + swapped * sin
        if c0 < ATT_WIDTH:
            aq_ref[0, :, c0:c0 + LANES] = (r * Q_SCALE).astype(bf16)
        else:
            k2_ref[0, :, c0 - ATT_WIDTH:c0 - ATT_WIDTH + LANES] = r.astype(bf16)

    vt_ref[0] = _dot_nt(wvt_ref[...], lhs).astype(bf16)
    mv_ref[0] = _dot(lhs, w_ref[:, C_MV:C_MV + M_WIDTH]).astype(bf16)
    mo_ref[0] = _dot(lhs, w_ref[:, C_MO:C_MO + M_WIDTH]).astype(bf16)

    gates = _dot(lhs, w_ref[:, C_G:C_G + LANES]) + bg_ref[...]
    n_streams = 2 * M_HEADS
    is_forget = (lane >= n_streams) & (lane < N_GATES)
    gates = jnp.where(is_forget, _log_sigmoid(gates), gates)
    g_ref[0] = gates[:, :N_GATES]
    for c in range(tm // CHUNK):
        t_c = gates[c * CHUNK:(c + 1) * CHUNK, :].T
        li_ref[0, c * n_streams:(c + 1) * n_streams, :] = t_c[0:n_streams]
        lf_ref[0, c * n_streams:(c + 1) * n_streams, :] = t_c[n_streams:2 * n_streams]

    conv_scr[...] = _dot(lhs_halo, w_ref[:, C_MQK:C_MQK + 2 * M_WIDTH])
    row = lax.broadcasted_iota(jnp.int32, (tm, 1), 0)
    prev = conv_scr[SUBLANES - 1:SUBLANES - 1 + tm, :]
    prev = jnp.where((row == 0) & (i == 0), 0.0, prev)
    nxt = conv_scr[SUBLANES + 1:SUBLANES + 1 + tm, :]
    nxt = jnp.where((row == tm - 1) & (i == nt - 1), 0.0, nxt)
    cur = conv_scr[SUBLANES:SUBLANES + tm, :]
    u = prev * conv_ref[0:1, :] + cur * conv_ref[1:2, :] + nxt * conv_ref[2:3, :]
    u = _silu(u)
    mq_ref[0] = (u[:, :M_WIDTH] * (M_DIM ** -0.5)).astype(bf16)
    mk = u[:, M_WIDTH:]
    mk_ref[0] = mk.astype(bf16)
    mkt_ref[0] = mk.T.astype(bf16)


def _in_projection(x, shift, scale, norm_w, w_all, cos_t, sin_t, qk_w, conv_qk, bg, *, tm):
    B, N, D = x.shape
    nt = N // tm
    hb = tm // SUBLANES
    nblk8 = N // SUBLANES
    kern = functools.partial(_inproj_kernel, tm=tm, nt=nt)
    gate_rows = (tm // CHUNK) * 2 * M_HEADS
    row_map = lambda b, i: (b, i, 0)
    const2 = lambda b, i: (0, 0)
    outs = pl.pallas_call(
        kern,
        grid=(B, nt),
        in_specs=[
            pl.BlockSpec((1, SUBLANES, D), lambda b, i: (b, jnp.maximum(i * hb - 1, 0), 0)),
            pl.BlockSpec((1, tm, D), row_map),
            pl.BlockSpec((1, SUBLANES, D), lambda b, i: (b, jnp.minimum((i + 1) * hb, nblk8 - 1), 0)),
            pl.BlockSpec((1, 1, D), lambda b, i: (b, 0, 0)),
            pl.BlockSpec((1, 1, D), lambda b, i: (b, 0, 0)),
            pl.BlockSpec((1, D), const2),
            pl.BlockSpec((D, W_COLS), const2),
            pl.BlockSpec((KV_WIDTH, D), const2),
            pl.BlockSpec((tm, LANES), lambda b, i: (i, 0)),
            pl.BlockSpec((tm, LANES), lambda b, i: (i, 0)),
            pl.BlockSpec((1, ATT_WIDTH + K2_WIDTH), const2),
            pl.BlockSpec((CONV_W, 2 * M_WIDTH), const2),
            pl.BlockSpec((1, LANES), const2),
        ],
        out_specs=[
            pl.BlockSpec((1, tm, ATT_WIDTH), row_map),
            pl.BlockSpec((1, tm, K2_WIDTH), row_map),
            pl.BlockSpec((1, KV_WIDTH, tm), lambda b, i: (b, 0, i)),
            pl.BlockSpec((1, tm, M_WIDTH), row_map),
            pl.BlockSpec((1, tm, M_WIDTH), row_map),
            pl.BlockSpec((1, M_WIDTH, tm), lambda b, i: (b, 0, i)),
            pl.BlockSpec((1, tm, M_WIDTH), row_map),
            pl.BlockSpec((1, tm, M_WIDTH), row_map),
            pl.BlockSpec((1, tm, N_GATES), row_map),
            pl.BlockSpec((1, gate_rows, CHUNK), row_map),
            pl.BlockSpec((1, gate_rows, CHUNK), row_map),
        ],
        out_shape=[
            jax.ShapeDtypeStruct((B, N, ATT_WIDTH), bf16),
            jax.ShapeDtypeStruct((B, N, K2_WIDTH), bf16),
            jax.ShapeDtypeStruct((B, KV_WIDTH, N), bf16),
            jax.ShapeDtypeStruct((B, N, M_WIDTH), bf16),
            jax.ShapeDtypeStruct((B, N, M_WIDTH), bf16),
            jax.ShapeDtypeStruct((B, M_WIDTH, N), bf16),
            jax.ShapeDtypeStruct((B, N, M_WIDTH), bf16),
            jax.ShapeDtypeStruct((B, N, M_WIDTH), bf16),
            jax.ShapeDtypeStruct((B, N, N_GATES), f32),
            jax.ShapeDtypeStruct((B, nt * gate_rows, CHUNK), f32),
            jax.ShapeDtypeStruct((B, nt * gate_rows, CHUNK), f32),
        ],
        scratch_shapes=[pltpu.VMEM((tm + 2 * SUBLANES, 2 * M_WIDTH), f32)],
        compiler_params=pltpu.CompilerParams(
            dimension_semantics=("parallel", "parallel"), vmem_limit_bytes=VMEM_LIMIT_BYTES),
        name="in_projection",
    )(x, x, x, shift, scale, norm_w, w_all, w_all[:, C_V:C_V + KV_WIDTH].T, cos_t, sin_t, qk_w, conv_qk, bg)
    return outs


def _attn_kernel(sink_ref, q_ref, kp_ref, kc_ref, kn_ref, vp_ref, vc_ref, vn_ref, ck_ref, cv_ref, o_ref, *, nb):
    n = pl.program_id(1)
    span = BLOCK + 2 * WINDOW
    grp = N_HEADS // N_KV_HEADS
    lane = lax.broadcasted_iota(jnp.int32, (1, LANES), 1)
    lo = lane < HEAD_DIM
    kj = lax.broadcasted_iota(jnp.int32, (span, BLOCK), 0)
    qi = lax.broadcasted_iota(jnp.int32, (span, BLOCK), 1)
    rel = kj - qi
    valid = (rel >= 0) & (rel <= 2 * WINDOW)
    valid = valid & ((kj >= WINDOW) | (n > 0)) & ((kj < WINDOW + BLOCK) | (n < nb - 1))
    bias = jnp.where(valid, 0.0, NEG)
    bias4 = jnp.concatenate([bias] * grp, axis=1)
    col = lax.broadcasted_iota(jnp.int32, (1, grp * BLOCK), 1)
    n_ctx = ck_ref.shape[1]
    ones_rows = jnp.ones((HEAD_DIM, span + n_ctx), f32).astype(bf16)
    zero = jnp.zeros((), bf16)
    scores = []
    for kh in range(N_KV_HEADS):
        ks = slice(kh * LANES, (kh + 1) * LANES)
        keys = jnp.concatenate([kp_ref[0, :, ks], kc_ref[0, :, ks], kn_ref[0, :, ks], ck_ref[0, :, ks]], axis=0)
        q0 = kh * grp * HEAD_DIM
        q2 = jnp.concatenate([q_ref[0, :, q0:q0 + LANES], q_ref[0, :, q0 + LANES:q0 + 2 * LANES]], axis=0)
        q4 = jnp.concatenate([jnp.where(lo, q2, zero), jnp.where(lo, zero, q2)], axis=0)
        scores.append(_dot_nt(keys, q4))
    probs = []
    for kh in range(N_KV_HEADS):
        heads = (0, 2, 1, 3)
        sk = sink_ref[kh * grp + heads[0]]
        for c, g in enumerate(heads[1:], start=1):
            sk = jnp.where(col >= c * BLOCK, sink_ref[kh * grp + g], sk)
        sk = sk * LOG2E
        s_t = scores[kh]
        s_loc = s_t[:span] + bias4
        s_ctx = s_t[span:]
        m = jnp.maximum(jnp.maximum(jnp.max(s_loc, axis=0, keepdims=True),
                                    jnp.max(s_ctx, axis=0, keepdims=True)), sk)
        p_t = jnp.concatenate([jnp.exp2(s_loc - m), jnp.exp2(s_ctx - m)], axis=0).astype(bf16)
        probs.append((p_t, jnp.exp2(sk - m)))
    for kh in range(N_KV_HEADS):
        hs = slice(kh * HEAD_DIM, (kh + 1) * HEAD_DIM)
        vals_t = jnp.concatenate([vp_ref[0, hs, :], vc_ref[0, hs, :], vn_ref[0, hs, :], cv_ref[0, hs, :]], axis=1)
        vals_t = jnp.concatenate([vals_t, ones_rows], axis=0)
        q0 = kh * grp * HEAD_DIM
        p_t, p_sink = probs[kh]
        o_t = _dot(vals_t, p_t)
        out_t = o_t[:HEAD_DIM] / (o_t[HEAD_DIM:] + p_sink)
        for pair in range(2):
            even = out_t[:, pair * BLOCK:(pair + 1) * BLOCK]
            odd = out_t[:, (2 + pair) * BLOCK:(3 + pair) * BLOCK]
            both = jnp.concatenate([even, odd], axis=0)
            o_ref[0, :, q0 + pair * LANES:q0 + (pair + 1) * LANES] = both.T.astype(o_ref.dtype)


def _attention(sink, q, k2, vt, ck2, cvt):
    B, N, _ = q.shape
    L = ck2.shape[1]
    nb = N // BLOCK
    kern = functools.partial(_attn_kernel, nb=nb)
    prev_blk = lambda n: jnp.maximum(n - 1, 0)
    next_blk = lambda n: jnp.minimum(n + 1, nb - 1)
    same = lambda n: n
    cur_map = lambda b, n, s: (b, n, 0)
    ctx_map = lambda b, n, s: (b, 0, 0)
    k_spec = lambda f: pl.BlockSpec((1, BLOCK, K2_WIDTH), lambda b, n, s: (b, f(n), 0))
    v_spec = lambda f: pl.BlockSpec((1, KV_WIDTH, BLOCK), lambda b, n, s: (b, 0, f(n)))
    return pl.pallas_call(
        kern,
        grid_spec=pltpu.PrefetchScalarGridSpec(
            num_scalar_prefetch=1,
            grid=(B, nb),
            in_specs=[pl.BlockSpec((1, BLOCK, ATT_WIDTH), cur_map),
                      k_spec(prev_blk), k_spec(same), k_spec(next_blk),
                      v_spec(prev_blk), v_spec(same), v_spec(next_blk),
                      pl.BlockSpec((1, L, K2_WIDTH), ctx_map),
                      pl.BlockSpec((1, KV_WIDTH, L), ctx_map)],
            out_specs=pl.BlockSpec((1, BLOCK, ATT_WIDTH), cur_map),
        ),
        out_shape=jax.ShapeDtypeStruct((B, N, ATT_WIDTH), bf16),
        compiler_params=pltpu.CompilerParams(dimension_semantics=("parallel", "parallel")),
        name="window_attention",
    )(sink, q, k2, k2, k2, vt, vt, vt, ck2, cvt)


def _tri_cumsum_cols(tri, x):
    h, m, l = _split3(x)
    return _dot(tri, h) + _dot(tri, m) + _dot(tri, l)


def _tri_cumsum_rows(x, tri_t):
    h, m, l = _split3(x)
    return _dot(h, tri_t) + _dot(m, tri_t) + _dot(l, tri_t)


def _tri(t):
    r = lax.broadcasted_iota(jnp.int32, (t, t), 0)
    c = lax.broadcasted_iota(jnp.int32, (t, t), 1)
    return jnp.where(c <= r, 1.0, 0.0).astype(bf16), jnp.where(r <= c, 1.0, 0.0).astype(bf16)


def _ctx_state_kernel(k_ref, v_ref, li_ref, lf_ref, c_ref, m_ref):
    L = k_ref.shape[1]
    tri, _ = _tri(L)
    li = li_ref[0]
    lf = lf_ref[0]
    lane = lax.broadcasted_iota(jnp.int32, (1, 2 * M_HEADS), 1)
    fwd = lane < M_HEADS
    cs = _tri_cumsum_cols(tri, lf)
    tot = cs[L - 1:L, :]
    b = jnp.where(fwd, cs, tot - cs + lf)
    w = tot - b + li
    m_new = jnp.maximum(tot, jnp.max(w, axis=0, keepdims=True))
    ws = jnp.exp(w - m_new)
    m_ref[0] = m_new
    for c in range(2 * M_HEADS):
        hs = slice((c % M_HEADS) * M_DIM, (c % M_HEADS + 1) * M_DIM)
        ks = k_ref[0, :, hs].astype(f32) * ws[:, c:c + 1]
        c_ref[0, c] = _dot_tn(ks.astype(bf16), _ones_cols(v_ref[0, :, hs]))


def _ctx_states(cmk, cmv, li_c, lf_c):
    B, L, _ = cmk.shape
    S = 2 * M_HEADS
    bmap = lambda b: (b, 0, 0)
    return pl.pallas_call(
        _ctx_state_kernel,
        grid=(B,),
        in_specs=[pl.BlockSpec((1, L, M_WIDTH), bmap), pl.BlockSpec((1, L, M_WIDTH), bmap),
                  pl.BlockSpec((1, L, S), bmap), pl.BlockSpec((1, L, S), bmap)],
        out_specs=[pl.BlockSpec((1, S, M_DIM, M_DIM + LANES), lambda b: (b, 0, 0, 0)),
                   pl.BlockSpec((1, 1, S), bmap)],
        out_shape=[jax.ShapeDtypeStruct((B, S, M_DIM, M_DIM + LANES), f32),
                   jax.ShapeDtypeStruct((B, 1, S), f32)],
        compiler_params=pltpu.CompilerParams(dimension_semantics=("parallel",)),
        name="mlstm_ctx_state",
    )(cmk, cmv, li_c, lf_c)


N_GATE_STATS = 6


def _gate_prep_kernel(li_ref, lf_ref, o_ref):
    S = 2 * M_HEADS
    T = CHUNK
    li = li_ref[0]
    lf = lf_ref[0]
    rows = li.shape[0]
    _, tri_t = _tri(T)
    fwd = (lax.broadcasted_iota(jnp.int32, (rows, 1), 0) % S) < M_HEADS
    lane = lax.broadcasted_iota(jnp.int32, (1, T), 1)
    cs = _tri_cumsum_rows(lf, tri_t)
    tot = cs[:, T - 1:T]
    b = jnp.where(fwd, cs, tot - cs + lf)
    e = li - b
    pm = e
    shift = 1
    while shift < T:
        from_left = jnp.where(lane >= shift, pltpu.roll(pm, shift, 1), -jnp.inf)
        from_right = jnp.where(lane < T - shift, pltpu.roll(pm, T - shift, 1), -jnp.inf)
        pm = jnp.maximum(pm, jnp.where(fwd, from_left, from_right))
        shift *= 2
    w = tot - b + li
    o_ref[0, 0] = e
    o_ref[0, 1] = pm
    o_ref[0, 2] = b
    o_ref[0, 3] = w
    o_ref[0, 4] = jnp.broadcast_to(tot, (rows, T))
    o_ref[0, 5] = jnp.broadcast_to(jnp.max(w, axis=-1, keepdims=True), (rows, T))


def _gate_prep(li_rc, lf_rc):
    B, rows, T = li_rc.shape
    bmap = lambda b: (b, 0, 0)
    return pl.pallas_call(
        _gate_prep_kernel,
        grid=(B,),
        in_specs=[pl.BlockSpec((1, rows, T), bmap), pl.BlockSpec((1, rows, T), bmap)],
        out_specs=pl.BlockSpec((1, N_GATE_STATS, rows, T), lambda b: (b, 0, 0, 0)),
        out_shape=jax.ShapeDtypeStruct((B, N_GATE_STATS, rows, T), f32),
        compiler_params=pltpu.CompilerParams(dimension_semantics=("parallel",)),
        name="mlstm_gate_prep",
    )(li_rc, lf_rc)


def _mlstm_kernel(qf_ref, ktf_ref, vf_ref, qb_ref, ktb_ref, vb_ref, gf_ref, gb_ref,
                  cn0_ref, m0_ref, hf_ref, hb_ref, cn_scr, m_scr):
    j = pl.program_id(1)
    S = 2 * M_HEADS
    T = CHUNK

    @pl.when(j == 0)
    def _():
        cn_scr[...] = cn0_ref[0]
        m_scr[...] = m0_ref[0]

    fwd_r = lax.broadcasted_iota(jnp.int32, (S, 1), 0) < M_HEADS
    stat = lambda k: jnp.where(fwd_r, gf_ref[0, k], gb_ref[0, k])
    e_r, pm, b_r, w_r, tot, w_max = [stat(k) for k in range(N_GATE_STATS)]

    m_prev = m_scr[...]
    g_hi = jnp.maximum(m_prev, pm).astype(bf16)
    g_used = g_hi.astype(f32)
    en_r = jnp.exp(-(b_r + g_used))
    en_hi = en_r.astype(bf16)
    en_lo = (en_r - en_hi.astype(f32)).astype(bf16)
    m_new = jnp.maximum(tot + m_prev, w_max)
    a_r = jnp.exp(tot + m_prev - m_new)
    ws_r = jnp.exp(w_r - m_new)
    m_scr[...] = m_new

    ti = lax.broadcasted_iota(jnp.int32, (T, T), 0)
    si = lax.broadcasted_iota(jnp.int32, (T, T), 1)
    eye = ti == si
    en_hi = en_hi.astype(f32)
    en_lo = en_lo.astype(f32)
    ones_tt = jnp.ones((T, T), f32).astype(bf16)
    zeros_tt = jnp.zeros((T, T), f32).astype(bf16)
    ones_2t = jnp.ones((2 * T, T), f32).astype(bf16)

    def refs(c):
        is_fwd = c < M_HEADS
        hs = slice((c % M_HEADS) * M_DIM, (c % M_HEADS + 1) * M_DIM)
        return ((qf_ref if is_fwd else qb_ref), (ktf_ref if is_fwd else ktb_ref), (vf_ref if is_fwd else vb_ref),
                (hf_ref if is_fwd else hb_ref), hs, is_fwd)

    stage1 = []
    for c in range(S):
        q_ref, kt_ref, _, _, hs, _ = refs(c)
        row = lambda a: a[c:c + 1, :]
        diag = lambda a: jnp.where(eye, row(a), 0.0).astype(bf16)
        q = q_ref[0, :, hs]
        kt = kt_ref[0, hs, :]
        qk_g = _dot(jnp.concatenate([q, diag(g_used)], axis=1),
                    jnp.concatenate([jnp.concatenate([kt, zeros_tt], axis=1),
                                     jnp.concatenate([zeros_tt, ones_tt], axis=1)], axis=0))
        en_rep = _dot(jnp.concatenate([diag(en_hi), diag(en_lo)], axis=1), ones_2t)
        qc = _dot(q, cn_scr[c].astype(bf16))
        stage1.append((qk_g, en_rep, qc))
    stage2 = []
    for c in range(S):
        _, _, v_ref, _, hs, is_fwd = refs(c)
        row = lambda a: a[c:c + 1, :]
        qk_g, en_rep, qc = stage1[c]
        g_rep = qk_g[:, T:]
        within = (si <= ti) if is_fwd else (si >= ti)
        p_mat = jnp.exp(jnp.where(within, row(e_r) - g_rep, -jnp.inf))
        wts = p_mat * qk_g[:, :T]
        decay = jnp.exp(row(m_prev) - g_rep)
        nd = _dot(wts.astype(bf16), _ones_cols(v_ref[0, :, hs]))
        stage2.append((nd, decay))
    for c in range(S):
        _, _, _, h_ref, hs, _ = refs(c)
        _, en_rep, qc = stage1[c]
        nd, decay = stage2[c]
        num = nd[:, :M_DIM] + decay * qc[:, :M_DIM]
        den = nd[:, M_DIM:] + decay * qc[:, M_DIM:]
        h = num / jnp.maximum(jnp.abs(den), en_rep)
        h_ref[0, :, hs] = h.astype(h_ref.dtype)
    for c in range(S):
        _, kt_ref, v_ref, _, hs, _ = refs(c)
        row = lambda a: a[c:c + 1, :]
        kst = (kt_ref[0, hs, :].astype(f32) * row(ws_r)).astype(bf16)
        a_c = row(a_r)
        cn_scr[c] = jnp.concatenate([a_c, a_c], axis=1) * cn_scr[c] + _dot(kst, _ones_cols(v_ref[0, :, hs]))


def _mlstm(mq, mkt, mv, gstats, cn0, m0):
    B, N, _ = mq.shape
    nc = N // CHUNK
    S = 2 * M_HEADS
    fmap = lambda b, j: (b, j, 0)
    bmap = lambda b, j: (b, nc - 1 - j, 0)
    frow = lambda b, j: (b, 0, j)
    brow = lambda b, j: (b, 0, nc - 1 - j)
    seq = lambda m: pl.BlockSpec((1, CHUNK, M_WIDTH), m)
    seq_t = lambda m: pl.BlockSpec((1, M_WIDTH, CHUNK), m)
    gspec = lambda m: pl.BlockSpec((1, N_GATE_STATS, S, CHUNK), m)
    return pl.pallas_call(
        _mlstm_kernel,
        grid=(B, nc),
        in_specs=[seq(fmap), seq_t(frow), seq(fmap), seq(bmap), seq_t(brow), seq(bmap),
                  gspec(lambda b, j: (b, 0, j, 0)), gspec(lambda b, j: (b, 0, nc - 1 - j, 0)),
                  pl.BlockSpec((1, S, M_DIM, M_DIM + LANES), lambda b, j: (b, 0, 0, 0)),
                  pl.BlockSpec((1, S, CHUNK), lambda b, j: (b, 0, 0))],
        out_specs=[seq(fmap), seq(bmap)],
        out_shape=[jax.ShapeDtypeStruct((B, N, M_WIDTH), bf16), jax.ShapeDtypeStruct((B, N, M_WIDTH), bf16)],
        scratch_shapes=[pltpu.VMEM((S, M_DIM, M_DIM + LANES), f32), pltpu.VMEM((S, CHUNK), f32)],
        compiler_params=pltpu.CompilerParams(dimension_semantics=("parallel", "arbitrary")),
        name="mlstm_scan",
    )(mq, mkt, mv, mq, mkt, mv, gstats, gstats, cn0, m0)


def _outproj_kernel(x_ref, att_ref, hf_ref, hb_ref, mo_ref, mnw_ref, wo_ref, g1_ref, n2w_ref, sh2_ref, sc2_ref,
                    wr_ref, x1_ref, h2_ref, aff_ref, affr_ref, *, tm):
    parts = []
    for h in range(M_HEADS):
        hs = slice(h * M_DIM, (h + 1) * M_DIM)
        s = hf_ref[0, :, hs].astype(f32) + hb_ref[0, :, hs].astype(f32)
        ms = jnp.mean(s * s, axis=-1, keepdims=True)
        hn = s * lax.rsqrt(ms + EPS) * mnw_ref[:, hs]
        gate = 1.0 / (1.0 + jnp.exp(-mo_ref[0, :, hs].astype(f32)))
        parts.append((hn * gate).astype(bf16))
    ml = jnp.concatenate(parts, axis=-1)
    proj = _dot(att_ref[0], wo_ref[0:ATT_WIDTH, :]) + _dot(ml, wo_ref[ATT_WIDTH:, :])
    x1 = x_ref[0] + g1_ref[0] * proj
    x1_ref[0] = x1
    ms = jnp.mean(x1 * x1, axis=-1, keepdims=True)
    h2 = (x1 * lax.rsqrt(ms + EPS) * n2w_ref[...]) * (1.0 + sc2_ref[0]) + sh2_ref[0]
    for j in range(SUBLANES):
        h2_ref[0, pl.ds(j, tm, stride=SUBLANES), :] = h2[:, j * LANES:(j + 1) * LANES]
    hh = h2.astype(bf16)
    hl = (h2 - hh.astype(f32)).astype(bf16)
    n_exp = aff_ref.shape[1]
    both = _dot(hh, wr_ref[...])
    logits_rows = both[:, :LANES] + both[:, LANES:] + _dot(hl, wr_ref[:, :LANES])
    logits = logits_rows.T[:n_exp]
    mx = jnp.max(logits, axis=0, keepdims=True)
    e = jnp.exp(logits - mx)
    aff = e / jnp.sum(e, axis=0, keepdims=True)
    aff_ref[0] = aff
    affr_ref[0] = aff.T


def _out_projection(x, att, hf, hb, mo, mnw, w_out, g1, n2w, sh2, sc2, wr_split, n_experts, *, tm):
    B, N, D = x.shape
    E = n_experts
    row_map = lambda b, i: (b, i, 0)
    const2 = lambda b, i: (0, 0)
    bvec = lambda b, i: (b, 0, 0)
    return pl.pallas_call(
        functools.partial(_outproj_kernel, tm=tm),
        grid=(B, N // tm),
        in_specs=[pl.BlockSpec((1, tm, D), row_map),
                  pl.BlockSpec((1, tm, ATT_WIDTH), row_map),
                  pl.BlockSpec((1, tm, M_WIDTH), row_map),
                  pl.BlockSpec((1, tm, M_WIDTH), row_map),
                  pl.BlockSpec((1, tm, M_WIDTH), row_map),
                  pl.BlockSpec((1, M_WIDTH), const2),
                  pl.BlockSpec((ATT_WIDTH + M_WIDTH, D), const2),
                  pl.BlockSpec((1, 1, D), bvec),
                  pl.BlockSpec((1, D), const2),
                  pl.BlockSpec((1, 1, D), bvec),
                  pl.BlockSpec((1, 1, D), bvec),
                  pl.BlockSpec((D, 2 * LANES), const2)],
        out_specs=[pl.BlockSpec((1, tm, D), row_map),
                   pl.BlockSpec((1, tm * SUBLANES, LANES), row_map),
                   pl.BlockSpec((1, E, tm), lambda b, i: (b, 0, i)),
                   pl.BlockSpec((1, tm, E), row_map)],
        out_shape=[jax.ShapeDtypeStruct((B, N, D), f32),
                   jax.ShapeDtypeStruct((B, N * SUBLANES, LANES), f32),
                   jax.ShapeDtypeStruct((B, E, N), f32),
                   jax.ShapeDtypeStruct((B, N, E), f32)],
        compiler_params=pltpu.CompilerParams(
            dimension_semantics=("parallel", "parallel"), vmem_limit_bytes=VMEM_LIMIT_BYTES),
        name="out_projection",
    )(x, att, hf, hb, mo, mnw, w_out, g1, n2w, sh2, sc2, wr_split)


def _chunk_stride(cap):
    return cap + SUBLANES


ROUTE_EXPONENT_BITS = (64, 32, 16, 8, 4, 2, 1)
ROUTE_REFINE_STEPS = 25
ROUTE_TINY = 1e-30


def _route_kernel(aff_ref, idx_ref, cl_scr, *, cap):
    E, N = aff_ref.shape[1], aff_ref.shape[2]
    aff = aff_ref[0]

    def count_above(t):
        return jnp.sum(jnp.where(aff > t, 1.0, 0.0), axis=-1, keepdims=True)

    hi = jnp.full((E, 1), 2.0, f32)
    for bit in ROUTE_EXPONENT_BITS:
        cand = hi * (2.0 ** -bit)
        hi = jnp.where(count_above(cand) < cap, cand, hi)
    lo = jnp.where(hi < ROUTE_TINY, -1.0, 0.5 * hi)
    for _ in range(ROUTE_REFINE_STEPS):
        q = 0.25 * (hi - lo)
        m1, m2, m3 = lo + q, lo + 2.0 * q, lo + 3.0 * q
        ok1, ok2, ok3 = [count_above(m) >= cap for m in (m1, m2, m3)]
        lo, hi = (jnp.where(ok3, m3, jnp.where(ok2, m2, jnp.where(ok1, m1, lo))),
                  jnp.where(ok3, hi, jnp.where(ok2, m3, jnp.where(ok1, m2, m1))))
    above = jnp.where(aff > hi, 1.0, 0.0)
    equal = jnp.where(aff > lo, 1.0, 0.0) - above
    need = cap - jnp.sum(above, axis=-1, keepdims=True)
    n_chunks = N // LANES
    assert n_chunks <= LANES
    _, tri_t = _tri(LANES)
    lane = lax.broadcasted_iota(jnp.int32, (1, LANES), 1)
    both = jnp.concatenate([above, equal], axis=0).astype(bf16)
    run = jnp.zeros((2 * E, 1), f32)
    start = jnp.zeros((E, 1), f32)
    ends = jnp.full((E, LANES), float(2 * N), f32)
    cl_scr[...] = jnp.zeros_like(cl_scr)
    for k in range(n_chunks):
        ck = _dot(both[:, k * LANES:(k + 1) * LANES], tri_t)
        count = (ck[:E] + run[:E]) + jnp.minimum(ck[E:] + run[E:], need)
        local = count - start
        for e in range(E):
            cl_scr[e, k:k + 1, :] = local[e:e + 1, :]
        start = count[:, LANES - 1:LANES]
        ends = jnp.where(lane == k, start, ends)
        run = run + ck[:, LANES - 1:LANES]

    slot = lax.broadcasted_iota(jnp.int32, (cap, 1), 0).astype(f32)
    for e in range(E):
        ends_e = ends[e:e + 1, :]
        before = ends_e <= slot
        chunk = jnp.sum(jnp.where(before, 1.0, 0.0), axis=-1, keepdims=True)
        chunk_start = jnp.max(jnp.where(before, ends_e, 0.0), axis=-1, keepdims=True)
        pick = jnp.where(lane.astype(f32) == chunk, 1.0, 0.0).astype(bf16)
        local = _dot(pick, cl_scr[e].astype(bf16))
        pos = jnp.sum(jnp.where(local <= slot - chunk_start, 1.0, 0.0), axis=-1, keepdims=True)
        idx_ref[0, e] = (chunk * float(LANES) + pos).astype(jnp.int32)


def _route(aff_t, cap):
    B, E, N = aff_t.shape
    return pl.pallas_call(
        functools.partial(_route_kernel, cap=cap),
        grid=(B,),
        in_specs=[pl.BlockSpec((1, E, N), lambda b: (b, 0, 0))],
        out_specs=pl.BlockSpec((1, E, cap, 1), lambda b: (b, 0, 0, 0)),
        out_shape=jax.ShapeDtypeStruct((B, E, cap, 1), jnp.int32),
        scratch_shapes=[pltpu.VMEM((E, LANES, LANES), f32)],
        compiler_params=pltpu.CompilerParams(dimension_semantics=("parallel",)),
        name="expert_choice_route",
    )(aff_t)


SCATTER_UNROLL = 8
EXPERT_VMEM_LIMIT_BYTES = 60 * 1024 * 1024


def _expert_kernel(idx_prev_ref, idx_ref, idx_next_ref, src_ref, affr_ref, wg_ref, wu_ref, wd_ref, acc_ref,
                   x_even, x_odd, g_even, g_odd, y_even, y_odd, *, cap, n_experts):
    e = pl.program_id(1)
    stride = _chunk_stride(cap)

    def gather(ids_ref, x_scr, g_scr):
        for i in range(cap):
            t = ids_ref[0, 0, i]
            x_scr[pl.ds(i, SUBLANES, stride=stride), :] = (
                src_ref[0, pl.ds(pl.multiple_of(t * SUBLANES, SUBLANES), SUBLANES), :])
            g_scr[pl.ds(i, 1), :] = affr_ref[0, pl.ds(t, 1), :]

    def ffn(x_scr, g_scr, y_scr):
        xg = jnp.concatenate([x_scr[j * stride:j * stride + cap, :].astype(bf16) for j in range(SUBLANES)], axis=1)
        lane = lax.broadcasted_iota(jnp.int32, (1, n_experts), 1)
        gate = jnp.sum(jnp.where(lane == e, g_scr[...], 0.0), axis=-1, keepdims=True)
        g = _dot(xg, wg_ref[0])
        u = _dot(xg, wu_ref[0])
        hmid = (_silu(g) * u).astype(bf16)
        y = _dot(hmid, wd_ref[0]) * gate
        for j in range(y.shape[1] // LANES):
            y_scr[j * stride:j * stride + cap, :] = y[:, j * LANES:(j + 1) * LANES]

    def scatter(ids_ref, y_scr):
        for i0 in range(0, cap, SCATTER_UNROLL):
            rows = [pl.multiple_of(ids_ref[0, 0, i0 + u] * SUBLANES, SUBLANES) for u in range(SCATTER_UNROLL)]
            vals = [acc_ref[0, pl.ds(rows[u], SUBLANES), :] + y_scr[pl.ds(i0 + u, SUBLANES, stride=stride), :]
                    for u in range(SCATTER_UNROLL)]
            for u in range(SCATTER_UNROLL):
                acc_ref[0, pl.ds(rows[u], SUBLANES), :] = vals[u]

    @pl.when(e == 0)
    def _():
        acc_ref[...] = jnp.zeros_like(acc_ref)
        y_odd[...] = jnp.zeros_like(y_odd)
        gather(idx_ref, x_even, g_even)

    @pl.when(e % 2 == 0)
    def _():
        gather(idx_next_ref, x_odd, g_odd)
        ffn(x_even, g_even, y_even)
        scatter(idx_prev_ref, y_odd)

    @pl.when(e % 2 == 1)
    def _():
        gather(idx_next_ref, x_even, g_even)
        ffn(x_odd, g_odd, y_odd)
        scatter(idx_prev_ref, y_even)

    @pl.when(e == n_experts - 1)
    def _():
        scatter(idx_ref, y_odd if (n_experts - 1) % 2 else y_even)


def _expert_mixture(idx, h2_slab, aff_rows, wg, wu, wd):
    B, E, cap = idx.shape
    D, FF = wg.shape[1], wg.shape[2]
    rows = (D // LANES) * _chunk_stride(cap)
    wmap = lambda b, e: (e, 0, 0)
    bmap = lambda b, e: (b, 0, 0)
    ids = idx.reshape(B * E, 1, cap)
    ids_spec = lambda m: pl.BlockSpec((1, 1, cap), m, memory_space=pltpu.SMEM)
    return pl.pallas_call(
        functools.partial(_expert_kernel, cap=cap, n_experts=E),
        grid=(B, E),
        in_specs=[ids_spec(lambda b, e: (b * E + jnp.maximum(e - 1, 0), 0, 0)),
                  ids_spec(lambda b, e: (b * E + e, 0, 0)),
                  ids_spec(lambda b, e: (b * E + jnp.minimum(e + 1, E - 1), 0, 0)),
                  pl.BlockSpec((1,) + h2_slab.shape[1:], bmap, pipeline_mode=pl.Buffered(1)),
                  pl.BlockSpec((1,) + aff_rows.shape[1:], bmap, pipeline_mode=pl.Buffered(1)),
                  pl.BlockSpec((1, D, FF), wmap),
                  pl.BlockSpec((1, D, FF), wmap),
                  pl.BlockSpec((1, FF, D), wmap)],
        out_specs=pl.BlockSpec((1,) + h2_slab.shape[1:], bmap, pipeline_mode=pl.Buffered(1)),
        out_shape=jax.ShapeDtypeStruct(h2_slab.shape, f32),
        scratch_shapes=[pltpu.VMEM((rows, LANES), f32), pltpu.VMEM((rows, LANES), f32),
                        pltpu.VMEM((cap, E), f32), pltpu.VMEM((cap, E), f32),
                        pltpu.VMEM((rows, LANES), f32), pltpu.VMEM((rows, LANES), f32)],
        compiler_params=pltpu.CompilerParams(
            dimension_semantics=("parallel", "arbitrary"), vmem_limit_bytes=EXPERT_VMEM_LIMIT_BYTES),
        name="expert_mixture",
    )(ids, ids, ids, h2_slab, aff_rows, wg, wu, wd)


def _final_kernel(x1_ref, g2_ref, acc_ref, o_ref, *, tm):
    ffn = jnp.concatenate([acc_ref[0, pl.ds(j, tm, stride=SUBLANES), :] for j in range(SUBLANES)], axis=-1)
    o_ref[0] = x1_ref[0] + g2_ref[0] * ffn


def _final_residual(x1, g2, acc, *, tm):
    B, N, D = x1.shape
    row_map = lambda b, i: (b, i, 0)
    return pl.pallas_call(
        functools.partial(_final_kernel, tm=tm),
        grid=(B, N // tm),
        in_specs=[pl.BlockSpec((1, tm, D), row_map),
                  pl.BlockSpec((1, 1, D), lambda b, i: (b, 0, 0)),
                  pl.BlockSpec((1, tm * SUBLANES, LANES), row_map)],
        out_specs=pl.BlockSpec((1, tm, D), row_map),
        out_shape=jax.ShapeDtypeStruct((B, N, D), f32),
        compiler_params=pltpu.CompilerParams(dimension_semantics=("parallel", "parallel")),
        name="final_residual",
    )(x1, g2, acc)


def _rope_tables(n):
    rows = n // GRID_W
    row, col = jnp.meshgrid(jnp.arange(rows), jnp.arange(GRID_W), indexing='ij')
    n_freq = HEAD_DIM // 4
    freqs = ROPE_BASE ** (-jnp.arange(n_freq, dtype=f32) / n_freq)
    ang = jnp.concatenate([row.reshape(-1, 1).astype(f32) * freqs, col.reshape(-1, 1).astype(f32) * freqs], -1)
    cos, sin = jnp.cos(ang), jnp.sin(ang)
    reps = LANES // HEAD_DIM
    return (jnp.tile(jnp.concatenate([cos, cos], -1), (1, reps)),
            jnp.tile(jnp.concatenate([-sin, sin], -1), (1, reps)))


def _pack_w_in(w_in):
    D = w_in.shape[0]
    o = 0
    aq = w_in[:, o:o + ATT_WIDTH]; o += ATT_WIDTH
    ak = w_in[:, o:o + KV_WIDTH]; o += KV_WIDTH
    av = w_in[:, o:o + KV_WIDTH]; o += KV_WIDTH
    mq = w_in[:, o:o + M_WIDTH]; o += M_WIDTH
    mk = w_in[:, o:o + M_WIDTH]; o += M_WIDTH
    mv = w_in[:, o:o + M_WIDTH]; o += M_WIDTH
    mo = w_in[:, o:o + M_WIDTH]; o += M_WIDTH
    gates = _gate_order(w_in[:, o:o + N_GATES])

    def dup(a):
        a = a.reshape(D, N_KV_HEADS, 1, HEAD_DIM)
        return jnp.broadcast_to(a, (D, N_KV_HEADS, 2, HEAD_DIM)).reshape(D, K2_WIDTH)

    pad = jnp.zeros((D, LANES - N_GATES), w_in.dtype)
    return jnp.concatenate([aq, dup(ak), av, mq, mk, mv, mo, gates, pad], -1).astype(bf16)


def _gate_order(g):
    H = M_HEADS
    return jnp.concatenate([g[..., 0:H], g[..., 2 * H:3 * H], g[..., H:2 * H], g[..., 3 * H:4 * H]], -1)


def _layer(x, ctx, mod_x, mod_c, norm1_w, norm2_w, w_in, b_gates, conv_qk, q_norm_w, k_norm_w, sink,
           mlstm_norm_w, w_out, w_router, w_gate, w_up, w_down):
    B, N, D = x.shape
    L = ctx.shape[1]
    assert D == SUBLANES * LANES, "token rows are handled as one (8, 128) register tile"
    tm = min(ROW_TILE, N)
    sh1, sc1, g1, sh2, sc2, g2 = [m[:, None, :] for m in jnp.split(mod_x, 6, -1)]
    csh1, csc1 = [jnp.broadcast_to(m[None, None, :], (B, 1, D)) for m in jnp.split(mod_c, 6, -1)[:2]]

    w_all = _pack_w_in(w_in)
    qk_w = jnp.concatenate([jnp.tile(q_norm_w, N_HEADS), jnp.tile(k_norm_w, 2 * N_KV_HEADS)])[None, :]
    bg = jnp.concatenate([_gate_order(b_gates), jnp.zeros((LANES - N_GATES,), f32)])[None, :]
    nw1 = norm1_w[None, :]
    cos_t, sin_t = _rope_tables(N)
    ones_t, zeros_t = jnp.ones((L, LANES), f32), jnp.zeros((L, LANES), f32)

    aq, k2, vt, mq, _, mkt, mv, mo, _, li_rc, lf_rc = _in_projection(x, sh1, sc1, nw1, w_all, cos_t, sin_t, qk_w,
                                                                     conv_qk, bg, tm=tm)
    _, ck2, cvt, _, cmk, _, cmv, _, cgts, _, _ = _in_projection(ctx, csh1, csc1, nw1, w_all, ones_t, zeros_t, qk_w,
                                                                conv_qk, bg, tm=L)

    att = _attention(sink, aq, k2, vt, ck2, cvt)

    n_streams = 2 * M_HEADS
    cn0, m0 = _ctx_states(cmk, cmv, cgts[..., :n_streams], cgts[..., n_streams:])
    m0 = jnp.broadcast_to(jnp.swapaxes(m0, 1, 2), (B, n_streams, CHUNK))
    gstats = _gate_prep(li_rc, lf_rc)
    hf, hb = _mlstm(mq, mkt, mv, gstats, cn0, m0)

    wr_pad = jnp.pad(w_router, ((0, 0), (0, LANES - N_EXPERTS)))
    wr_hi = wr_pad.astype(bf16)
    wr_split = jnp.concatenate([wr_hi, (wr_pad - wr_hi.astype(f32)).astype(bf16)], axis=1)
    x1, h2_slab, aff_t, aff_rows = _out_projection(x, att, hf, hb, mo, mlstm_norm_w[None, :], w_out.astype(bf16),
                                                   g1, norm2_w[None, :], sh2, sc2, wr_split, N_EXPERTS, tm=tm)

    cap = CAPACITY * N // N_EXPERTS
    idx = _route(aff_t, cap).reshape(B, N_EXPERTS, cap)
    acc = _expert_mixture(idx, h2_slab, aff_rows, w_gate.astype(bf16), w_up.astype(bf16), w_down.astype(bf16))
    return _final_residual(x1, g2, acc, tm=tm)


def kernel(x, c, ctx, c_ctx, w_mod, b_mod, norm1_w, norm2_w, w_in, b_gates, conv_qk, q_norm_w, k_norm_w, sink,
           mlstm_norm_w, w_out, w_router, w_gate, w_up, w_down):
    depth = w_mod.shape[0]
    assert depth == 1, "only the final-layer (no context update) form of the block is implemented"
    B = x.shape[0]
    pad_rows = (-(B + 1)) % SUBLANES
    c_all = jnp.concatenate([c, c_ctx[None, :], jnp.zeros((pad_rows, c.shape[1]), c.dtype)], 0)
    mod = _modulation(c_all, w_mod[0], b_mod[0])
    return _layer(x, ctx, mod[:B], mod[B], norm1_w[0], norm2_w[0], w_in[0], b_gates[0], conv_qk[0], q_norm_w[0],
                  k_norm_w[0], sink[0], mlstm_norm_w[0], w_out[0], w_router[0], w_gate[0], w_up[0], w_down[0])
```

```python
import functools

import jax
import jax.numpy as jnp
from jax import lax
from jax.experimental import pallas as pl
from jax.experimental.pallas import tpu as pltpu

f32 = jnp.float32
bf16 = jnp.bfloat16

GRID_W = 64
N_HEADS = 8
N_KV_HEADS = 2
HEAD_DIM = 64
WINDOW = 128
BLOCK = 128
ROPE_BASE = 10000.0
M_HEADS = 4
M_DIM = 128
CHUNK = 128
CONV_W = 3
ATT_WIDTH = N_HEADS * HEAD_DIM
KV_WIDTH = N_KV_HEADS * HEAD_DIM
M_WIDTH = M_HEADS * M_DIM
N_EXPERTS = 16
CAPACITY = 2
EPS = 1e-6
NEG = -1e30
LOG2E = 1.4426950408889634
Q_SCALE = HEAD_DIM ** -0.5 * LOG2E

LANES = 128
SUBLANES = 8
VMEM_LIMIT_BYTES = 56 * 1024 * 1024
ROW_TILE = 1024
MOD_COL_TILE = 512

K2_WIDTH = 2 * KV_WIDTH
C_Q = 0
C_K = C_Q + ATT_WIDTH
C_V = C_K + K2_WIDTH
C_MQK = C_V + KV_WIDTH
C_MV = C_MQK + 2 * M_WIDTH
C_MO = C_MV + M_WIDTH
C_G = C_MO + M_WIDTH
W_COLS = C_G + LANES
N_GATES = 4 * M_HEADS


def _dot(a, b):
    return jnp.dot(a, b, preferred_element_type=f32)


def _dot_nt(a, b):
    return lax.dot_general(a, b, (((1,), (1,)), ((), ())), preferred_element_type=f32)


def _dot_tn(a, b):
    return lax.dot_general(a, b, (((0,), (0,)), ((), ())), preferred_element_type=f32)


def _ones_cols(v):
    return jnp.concatenate([v, jnp.ones((v.shape[0], LANES), f32).astype(bf16)], axis=1)


def _split3(x):
    h = x.astype(bf16)
    r = x - h.astype(f32)
    m = r.astype(bf16)
    l = (r - m.astype(f32)).astype(bf16)
    return h, m, l


def _log_sigmoid(x):
    return jnp.minimum(x, 0.0) - jnp.log1p(jnp.exp(-jnp.abs(x)))


def _silu(x):
    return x / (1.0 + jnp.exp(-x))


def _mod_kernel(c_ref, w_ref, b_ref, o_ref):
    h, m, l = _split3(_silu(c_ref[...]))
    w = w_ref[...]
    wh = w.astype(bf16)
    wl = (w - wh.astype(f32)).astype(bf16)
    acc = _dot(h, wh) + _dot(m, wh) + _dot(h, wl) + _dot(l, wh) + _dot(m, wl)
    o_ref[...] = acc + b_ref[...]


def _modulation(c_all, w_mod, b_mod):
    rows, d = c_all.shape
    cols = w_mod.shape[1]
    tn = MOD_COL_TILE
    return pl.pallas_call(
        _mod_kernel,
        grid=(cols // tn,),
        in_specs=[pl.BlockSpec((rows, d), lambda j: (0, 0)),
                  pl.BlockSpec((d, tn), lambda j: (0, j)),
                  pl.BlockSpec((1, tn), lambda j: (0, j))],
        out_specs=pl.BlockSpec((rows, tn), lambda j: (0, j)),
        out_shape=jax.ShapeDtypeStruct((rows, cols), f32),
        name="adaln_mod",
    )(c_all, w_mod, b_mod.reshape(1, cols))


def _inproj_kernel(xp_ref, x_ref, xn_ref, shift_ref, scale_ref, nw_ref, w_ref, wvt_ref, cos_ref, sin_ref, qkw_ref,
                   conv_ref, bg_ref,
                   aq_ref, k2_ref, vt_ref, mq_ref, mk_ref, mkt_ref, mv_ref, mo_ref, g_ref, li_ref, lf_ref,
                   conv_scr, *, tm, nt):
    i = pl.program_id(1)
    nw = nw_ref[...]
    sc = 1.0 + scale_ref[0]
    sh = shift_ref[0]

    def prep(xv):
        ms = jnp.mean(xv * xv, axis=-1, keepdims=True)
        return (xv * lax.rsqrt(ms + EPS) * nw) * sc + sh

    hm = prep(x_ref[0])
    lhs = hm.astype(bf16)
    lhs_halo = jnp.concatenate([prep(xp_ref[0]), hm, prep(xn_ref[0])], axis=0).astype(bf16)

    lane = lax.broadcasted_iota(jnp.int32, (1, LANES), 1)
    lo = lane < HEAD_DIM
    first_half = (lane % HEAD_DIM) < (HEAD_DIM // 2)
    cos = cos_ref[...]
    sin = sin_ref[...]
    n_qk = (ATT_WIDTH + K2_WIDTH) // LANES
    vqk = _dot(lhs, w_ref[:, C_Q:C_Q + ATT_WIDTH + K2_WIDTH])
    for grp_i in range(n_qk):
        c0 = grp_i * LANES
        v = vqk[:, c0:c0 + LANES]
        sq = v * v
        s_all = jnp.sum(sq, axis=-1, keepdims=True)
        s_lo = jnp.sum(jnp.where(lo, sq, 0.0), axis=-1, keepdims=True)
        ms = jnp.where(lo, s_lo, s_all - s_lo) * (1.0 / HEAD_DIM)
        nv = v * lax.rsqrt(ms + EPS) * qkw_ref[:, c0:c0 + LANES]
        swapped = jnp.where(first_half, pltpu.roll(nv, LANES - HEAD_DIM // 2, 1), pltpu.roll(nv, HEAD_DIM // 2, 1))
        r = nv * cos + swapped * sin
        if c0 < ATT_WIDTH:
            aq_ref[0, :, c0:c0 + LANES] = (r * Q_SCALE).astype(bf16)
        else:
            k2_ref[0, :, c0 - ATT_WIDTH:c0 - ATT_WIDTH + LANES] = r.astype(bf16)

    vt_ref[0] = _dot_nt(wvt_ref[...], lhs).astype(bf16)
    mv_ref[0] = _dot(lhs, w_ref[:, C_MV:C_MV + M_WIDTH]).astype(bf16)
    mo_ref[0] = _dot(lhs, w_ref[:, C_MO:C_MO + M_WIDTH]).astype(bf16)

    gates = _dot(lhs, w_ref[:, C_G:C_G + LANES]) + bg_ref[...]
    n_streams = 2 * M_HEADS
    is_forget = (lane >= n_streams) & (lane < N_GATES)
    gates = jnp.where(is_forget, _log_sigmoid(gates), gates)
    g_ref[0] = gates[:, :N_GATES]
    for c in range(tm // CHUNK):
        t_c = gates[c * CHUNK:(c + 1) * CHUNK, :].T
        li_ref[0, c * n_streams:(c + 1) * n_streams, :] = t_c[0:n_streams]
        lf_ref[0, c * n_streams:(c + 1) * n_streams, :] = t_c[n_streams:2 * n_streams]

    conv_scr[...] = _dot(lhs_halo, w_ref[:, C_MQK:C_MQK + 2 * M_WIDTH])
    row = lax.broadcasted_iota(jnp.int32, (tm, 1), 0)
    prev = conv_scr[SUBLANES - 1:SUBLANES - 1 + tm, :]
    prev = jnp.where((row == 0) & (i == 0), 0.0, prev)
    nxt = conv_scr[SUBLANES + 1:SUBLANES + 1 + tm, :]
    nxt = jnp.where((row == tm - 1) & (i == nt - 1), 0.0, nxt)
    cur = conv_scr[SUBLANES:SUBLANES + tm, :]
    u = prev * conv_ref[0:1, :] + cur * conv_ref[1:2, :] + nxt * conv_ref[2:3, :]
    u = _silu(u)
    mq_ref[0] = (u[:, :M_WIDTH] * (M_DIM ** -0.5)).astype(bf16)
    mk = u[:, M_WIDTH:]
    mk_ref[0] = mk.astype(bf16)
    mkt_ref[0] = mk.T.astype(bf16)


def _in_projection(x, shift, scale, norm_w, w_all, cos_t, sin_t, qk_w, conv_qk, bg, *, tm):
    B, N, D = x.shape
    nt = N // tm
    hb = tm // SUBLANES
    nblk8 = N // SUBLANES
    kern = functools.partial(_inproj_kernel, tm=tm, nt=nt)
    gate_rows = (tm // CHUNK) * 2 * M_HEADS
    row_map = lambda b, i: (b, i, 0)
    const2 = lambda b, i: (0, 0)
    outs = pl.pallas_call(
        kern,
        grid=(B, nt),
        in_specs=[
            pl.BlockSpec((1, SUBLANES, D), lambda b, i: (b, jnp.maximum(i * hb - 1, 0), 0)),
            pl.BlockSpec((1, tm, D), row_map),
            pl.BlockSpec((1, SUBLANES, D), lambda b, i: (b, jnp.minimum((i + 1) * hb, nblk8 - 1), 0)),
            pl.BlockSpec((1, 1, D), lambda b, i: (b, 0, 0)),
            pl.BlockSpec((1, 1, D), lambda b, i: (b, 0, 0)),
            pl.BlockSpec((1, D), const2),
            pl.BlockSpec((D, W_COLS), const2),
            pl.BlockSpec((KV_WIDTH, D), const2),
            pl.BlockSpec((tm, LANES), lambda b, i: (i, 0)),
            pl.BlockSpec((tm, LANES), lambda b, i: (i, 0)),
            pl.BlockSpec((1, ATT_WIDTH + K2_WIDTH), const2),
            pl.BlockSpec((CONV_W, 2 * M_WIDTH), const2),
            pl.BlockSpec((1, LANES), const2),
        ],
        out_specs=[
            pl.BlockSpec((1, tm, ATT_WIDTH), row_map),
            pl.BlockSpec((1, tm, K2_WIDTH), row_map),
            pl.BlockSpec((1, KV_WIDTH, tm), lambda b, i: (b, 0, i)),
            pl.BlockSpec((1, tm, M_WIDTH), row_map),
            pl.BlockSpec((1, tm, M_WIDTH), row_map),
            pl.BlockSpec((1, M_WIDTH, tm), lambda b, i: (b, 0, i)),
            pl.BlockSpec((1, tm, M_WIDTH), row_map),
            pl.BlockSpec((1, tm, M_WIDTH), row_map),
            pl.BlockSpec((1, tm, N_GATES), row_map),
            pl.BlockSpec((1, gate_rows, CHUNK), row_map),
            pl.BlockSpec((1, gate_rows, CHUNK), row_map),
        ],
        out_shape=[
            jax.ShapeDtypeStruct((B, N, ATT_WIDTH), bf16),
            jax.ShapeDtypeStruct((B, N, K2_WIDTH), bf16),
            jax.ShapeDtypeStruct((B, KV_WIDTH, N), bf16),
            jax.ShapeDtypeStruct((B, N, M_WIDTH), bf16),
            jax.ShapeDtypeStruct((B, N, M_WIDTH), bf16),
            jax.ShapeDtypeStruct((B, M_WIDTH, N), bf16),
            jax.ShapeDtypeStruct((B, N, M_WIDTH), bf16),
            jax.ShapeDtypeStruct((B, N, M_WIDTH), bf16),
            jax.ShapeDtypeStruct((B, N, N_GATES), f32),
            jax.ShapeDtypeStruct((B, nt * gate_rows, CHUNK), f32),
            jax.ShapeDtypeStruct((B, nt * gate_rows, CHUNK), f32),
        ],
        scratch_shapes=[pltpu.VMEM((tm + 2 * SUBLANES, 2 * M_WIDTH), f32)],
        compiler_params=pltpu.CompilerParams(
            dimension_semantics=("parallel", "parallel"), vmem_limit_bytes=VMEM_LIMIT_BYTES),
        name="in_projection",
    )(x, x, x, shift, scale, norm_w, w_all, w_all[:, C_V:C_V + KV_WIDTH].T, cos_t, sin_t, qk_w, conv_qk, bg)
    return outs


def _attn_kernel(sink_ref, q_ref, kp_ref, kc_ref, kn_ref, vp_ref, vc_ref, vn_ref, ck_ref, cv_ref, o_ref, *, nb):
    n = pl.program_id(1)
    span = BLOCK + 2 * WINDOW
    grp = N_HEADS // N_KV_HEADS
    lane = lax.broadcasted_iota(jnp.int32, (1, LANES), 1)
    lo = lane < HEAD_DIM
    kj = lax.broadcasted_iota(jnp.int32, (span, BLOCK), 0)
    qi = lax.broadcasted_iota(jnp.int32, (span, BLOCK), 1)
    rel = kj - qi
    valid = (rel >= 0) & (rel <= 2 * WINDOW)
    valid = valid & ((kj >= WINDOW) | (n > 0)) & ((kj < WINDOW + BLOCK) | (n < nb - 1))
    bias = jnp.where(valid, 0.0, NEG)
    bias4 = jnp.concatenate([bias] * grp, axis=1)
    col = lax.broadcasted_iota(jnp.int32, (1, grp * BLOCK), 1)
    n_ctx = ck_ref.shape[1]
    ones_rows = jnp.ones((HEAD_DIM, span + n_ctx), f32).astype(bf16)
    zero = jnp.zeros((), bf16)
    scores = []
    for kh in range(N_KV_HEADS):
        ks = slice(kh * LANES, (kh + 1) * LANES)
        keys = jnp.concatenate([kp_ref[0, :, ks], kc_ref[0, :, ks], kn_ref[0, :, ks], ck_ref[0, :, ks]], axis=0)
        q0 = kh * grp * HEAD_DIM
        q2 = jnp.concatenate([q_ref[0, :, q0:q0 + LANES], q_ref[0, :, q0 + LANES:q0 + 2 * LANES]], axis=0)
        q4 = jnp.concatenate([jnp.where(lo, q2, zero), jnp.where(lo, zero, q2)], axis=0)
        scores.append(_dot_nt(keys, q4))
    probs = []
    for kh in range(N_KV_HEADS):
        heads = (0, 2, 1, 3)
        sk = sink_ref[kh * grp + heads[0]]
        for c, g in enumerate(heads[1:], start=1):
            sk = jnp.where(col >= c * BLOCK, sink_ref[kh * grp + g], sk)
        sk = sk * LOG2E
        s_t = scores[kh]
        s_loc = s_t[:span] + bias4
        s_ctx = s_t[span:]
        m = jnp.maximum(jnp.maximum(jnp.max(s_loc, axis=0, keepdims=True),
                                    jnp.max(s_ctx, axis=0, keepdims=True)), sk)
        p_t = jnp.concatenate([jnp.exp2(s_loc - m), jnp.exp2(s_ctx - m)], axis=0).astype(bf16)
        probs.append((p_t, jnp.exp2(sk - m)))
    for kh in range(N_KV_HEADS):
        hs = slice(kh * HEAD_DIM, (kh + 1) * HEAD_DIM)
        vals_t = jnp.concatenate([vp_ref[0, hs, :], vc_ref[0, hs, :], vn_ref[0, hs, :], cv_ref[0, hs, :]], axis=1)
        vals_t = jnp.concatenate([vals_t, ones_rows], axis=0)
        q0 = kh * grp * HEAD_DIM
        p_t, p_sink = probs[kh]
        o_t = _dot(vals_t, p_t)
        out_t = o_t[:HEAD_DIM] / (o_t[HEAD_DIM:] + p_sink)
        for pair in range(2):
            even = out_t[:, pair * BLOCK:(pair + 1) * BLOCK]
            odd = out_t[:, (2 + pair) * BLOCK:(3 + pair) * BLOCK]
            both = jnp.concatenate([even, odd], axis=0)
            o_ref[0, :, q0 + pair * LANES:q0 + (pair + 1) * LANES] = both.T.astype(o_ref.dtype)


def _attention(sink, q, k2, vt, ck2, cvt):
    B, N, _ = q.shape
    L = ck2.shape[1]
    nb = N // BLOCK
    kern = functools.partial(_attn_kernel, nb=nb)
    prev_blk = lambda n: jnp.maximum(n - 1, 0)
    next_blk = lambda n: jnp.minimum(n + 1, nb - 1)
    same = lambda n: n
    cur_map = lambda b, n, s: (b, n, 0)
    ctx_map = lambda b, n, s: (b, 0, 0)
    k_spec = lambda f: pl.BlockSpec((1, BLOCK, K2_WIDTH), lambda b, n, s: (b, f(n), 0))
    v_spec = lambda f: pl.BlockSpec((1, KV_WIDTH, BLOCK), lambda b, n, s: (b, 0, f(n)))
    return pl.pallas_call(
        kern,
        grid_spec=pltpu.PrefetchScalarGridSpec(
            num_scalar_prefetch=1,
            grid=(B, nb),
            in_specs=[pl.BlockSpec((1, BLOCK, ATT_WIDTH), cur_map),
                      k_spec(prev_blk), k_spec(same), k_spec(next_blk),
                      v_spec(prev_blk), v_spec(same), v_spec(next_blk),
                      pl.BlockSpec((1, L, K2_WIDTH), ctx_map),
                      pl.BlockSpec((1, KV_WIDTH, L), ctx_map)],
            out_specs=pl.BlockSpec((1, BLOCK, ATT_WIDTH), cur_map),
        ),
        out_shape=jax.ShapeDtypeStruct((B, N, ATT_WIDTH), bf16),
        compiler_params=pltpu.CompilerParams(dimension_semantics=("parallel", "parallel")),
        name="window_attention",
    )(sink, q, k2, k2, k2, vt, vt, vt, ck2, cvt)


def _tri_cumsum_cols(tri, x):
    h, m, l = _split3(x)
    return _dot(tri, h) + _dot(tri, m) + _dot(tri, l)


def _tri_cumsum_rows(x, tri_t):
    h, m, l = _split3(x)
    return _dot(h, tri_t) + _dot(m, tri_t) + _dot(l, tri_t)


def _tri(t):
    r = lax.broadcasted_iota(jnp.int32, (t, t), 0)
    c = lax.broadcasted_iota(jnp.int32, (t, t), 1)
    return jnp.where(c <= r, 1.0, 0.0).astype(bf16), jnp.where(r <= c, 1.0, 0.0).astype(bf16)


def _ctx_state_kernel(k_ref, v_ref, li_ref, lf_ref, c_ref, m_ref):
    L = k_ref.shape[1]
    tri, _ = _tri(L)
    li = li_ref[0]
    lf = lf_ref[0]
    lane = lax.broadcasted_iota(jnp.int32, (1, 2 * M_HEADS), 1)
    fwd = lane < M_HEADS
    cs = _tri_cumsum_cols(tri, lf)
    tot = cs[L - 1:L, :]
    b = jnp.where(fwd, cs, tot - cs + lf)
    w = tot - b + li
    m_new = jnp.maximum(tot, jnp.max(w, axis=0, keepdims=True))
    ws = jnp.exp(w - m_new)
    m_ref[0] = m_new
    for c in range(2 * M_HEADS):
        hs = slice((c % M_HEADS) * M_DIM, (c % M_HEADS + 1) * M_DIM)
        ks = k_ref[0, :, hs].astype(f32) * ws[:, c:c + 1]
        c_ref[0, c] = _dot_tn(ks.astype(bf16), _ones_cols(v_ref[0, :, hs]))


def _ctx_states(cmk, cmv, li_c, lf_c):
    B, L, _ = cmk.shape
    S = 2 * M_HEADS
    bmap = lambda b: (b, 0, 0)
    return pl.pallas_call(
        _ctx_state_kernel,
        grid=(B,),
        in_specs=[pl.BlockSpec((1, L, M_WIDTH), bmap), pl.BlockSpec((1, L, M_WIDTH), bmap),
                  pl.BlockSpec((1, L, S), bmap), pl.BlockSpec((1, L, S), bmap)],
        out_specs=[pl.BlockSpec((1, S, M_DIM, M_DIM + LANES), lambda b: (b, 0, 0, 0)),
                   pl.BlockSpec((1, 1, S), bmap)],
        out_shape=[jax.ShapeDtypeStruct((B, S, M_DIM, M_DIM + LANES), f32),
                   jax.ShapeDtypeStruct((B, 1, S), f32)],
        compiler_params=pltpu.CompilerParams(dimension_semantics=("parallel",)),
        name="mlstm_ctx_state",
    )(cmk, cmv, li_c, lf_c)


N_GATE_STATS = 6


def _gate_prep_kernel(li_ref, lf_ref, o_ref):
    S = 2 * M_HEADS
    T = CHUNK
    li = li_ref[0]
    lf = lf_ref[0]
    rows = li.shape[0]
    _, tri_t = _tri(T)
    fwd = (lax.broadcasted_iota(jnp.int32, (rows, 1), 0) % S) < M_HEADS
    lane = lax.broadcasted_iota(jnp.int32, (1, T), 1)
    cs = _tri_cumsum_rows(lf, tri_t)
    tot = cs[:, T - 1:T]
    b = jnp.where(fwd, cs, tot - cs + lf)
    e = li - b
    pm = e
    shift = 1
    while shift < T:
        from_left = jnp.where(lane >= shift, pltpu.roll(pm, shift, 1), -jnp.inf)
        from_right = jnp.where(lane < T - shift, pltpu.roll(pm, T - shift, 1), -jnp.inf)
        pm = jnp.maximum(pm, jnp.where(fwd, from_left, from_right))
        shift *= 2
    w = tot - b + li
    o_ref[0, 0] = e
    o_ref[0, 1] = pm
    o_ref[0, 2] = b
    o_ref[0, 3] = w
    o_ref[0, 4] = jnp.broadcast_to(tot, (rows, T))
    o_ref[0, 5] = jnp.broadcast_to(jnp.max(w, axis=-1, keepdims=True), (rows, T))


def _gate_prep(li_rc, lf_rc):
    B, rows, T = li_rc.shape
    bmap = lambda b: (b, 0, 0)
    return pl.pallas_call(
        _gate_prep_kernel,
        grid=(B,),
        in_specs=[pl.BlockSpec((1, rows, T), bmap), pl.BlockSpec((1, rows, T), bmap)],
        out_specs=pl.BlockSpec((1, N_GATE_STATS, rows, T), lambda b: (b, 0, 0, 0)),
        out_shape=jax.ShapeDtypeStruct((B, N_GATE_STATS, rows, T), f32),
        compiler_params=pltpu.CompilerParams(dimension_semantics=("parallel",)),
        name="mlstm_gate_prep",
    )(li_rc, lf_rc)


def _mlstm_kernel(qf_ref, ktf_ref, vf_ref, qb_ref, ktb_ref, vb_ref, gf_ref, gb_ref,
                  cn0_ref, m0_ref, hf_ref, hb_ref, cn_scr, m_scr):
    j = pl.program_id(1)
    S = 2 * M_HEADS
    T = CHUNK

    @pl.when(j == 0)
    def _():
        cn_scr[...] = cn0_ref[0]
        m_scr[...] = m0_ref[0]

    fwd_r = lax.broadcasted_iota(jnp.int32, (S, 1), 0) < M_HEADS
    stat = lambda k: jnp.where(fwd_r, gf_ref[0, k], gb_ref[0, k])
    e_r, pm, b_r, w_r, tot, w_max = [stat(k) for k in range(N_GATE_STATS)]

    m_prev = m_scr[...]
    g_hi = jnp.maximum(m_prev, pm).astype(bf16)
    g_used = g_hi.astype(f32)
    en_r = jnp.exp(-(b_r + g_used))
    en_hi = en_r.astype(bf16)
    en_lo = (en_r - en_hi.astype(f32)).astype(bf16)
    m_new = jnp.maximum(tot + m_prev, w_max)
    a_r = jnp.exp(tot + m_prev - m_new)
    ws_r = jnp.exp(w_r - m_new)
    m_scr[...] = m_new

    ti = lax.broadcasted_iota(jnp.int32, (T, T), 0)
    si = lax.broadcasted_iota(jnp.int32, (T, T), 1)
    eye = ti == si
    en_hi = en_hi.astype(f32)
    en_lo = en_lo.astype(f32)
    ones_tt = jnp.ones((T, T), f32).astype(bf16)
    zeros_tt = jnp.zeros((T, T), f32).astype(bf16)
    ones_2t = jnp.ones((2 * T, T), f32).astype(bf16)

    def refs(c):
        is_fwd = c < M_HEADS
        hs = slice((c % M_HEADS) * M_DIM, (c % M_HEADS + 1) * M_DIM)
        return ((qf_ref if is_fwd else qb_ref), (ktf_ref if is_fwd else ktb_ref), (vf_ref if is_fwd else vb_ref),
                (hf_ref if is_fwd else hb_ref), hs, is_fwd)

    stage1 = []
    for c in range(S):
        q_ref, kt_ref, _, _, hs, _ = refs(c)
        row = lambda a: a[c:c + 1, :]
        diag = lambda a: jnp.where(eye, row(a), 0.0).astype(bf16)
        q = q_ref[0, :, hs]
        kt = kt_ref[0, hs, :]
        qk_g = _dot(jnp.concatenate([q, diag(g_used)], axis=1),
                    jnp.concatenate([jnp.concatenate([kt, zeros_tt], axis=1),
                                     jnp.concatenate([zeros_tt, ones_tt], axis=1)], axis=0))
        en_rep = _dot(jnp.concatenate([diag(en_hi), diag(en_lo)], axis=1), ones_2t)
        qc = _dot(q, cn_scr[c].astype(bf16))
        stage1.append((qk_g, en_rep, qc))
    stage2 = []
    for c in range(S):
        _, _, v_ref, _, hs, is_fwd = refs(c)
        row = lambda a: a[c:c + 1, :]
        qk_g, en_rep, qc = stage1[c]
        g_rep = qk_g[:, T:]
        within = (si <= ti) if is_fwd else (si >= ti)
        p_mat = jnp.exp(jnp.where(within, row(e_r) - g_rep, -jnp.inf))
        wts = p_mat * qk_g[:, :T]
        decay = jnp.exp(row(m_prev) - g_rep)
        nd = _dot(wts.astype(bf16), _ones_cols(v_ref[0, :, hs]))
        stage2.append((nd, decay))
    for c in range(S):
        _, _, _, h_ref, hs, _ = refs(c)
        _, en_rep, qc = stage1[c]
        nd, decay = stage2[c]
        num = nd[:, :M_DIM] + decay * qc[:, :M_DIM]
        den = nd[:, M_DIM:] + decay * qc[:, M_DIM:]
        h = num / jnp.maximum(jnp.abs(den), en_rep)
        h_ref[0, :, hs] = h.astype(h_ref.dtype)
    for c in range(S):
        _, kt_ref, v_ref, _, hs, _ = refs(c)
        row = lambda a: a[c:c + 1, :]
        kst = (kt_ref[0, hs, :].astype(f32) * row(ws_r)).astype(bf16)
        a_c = row(a_r)
        cn_scr[c] = jnp.concatenate([a_c, a_c], axis=1) * cn_scr[c] + _dot(kst, _ones_cols(v_ref[0, :, hs]))


def _mlstm(mq, mkt, mv, gstats, cn0, m0):
    B, N, _ = mq.shape
    nc = N // CHUNK
    S = 2 * M_HEADS
    fmap = lambda b, j: (b, j, 0)
    bmap = lambda b, j: (b, nc - 1 - j, 0)
    frow = lambda b, j: (b, 0, j)
    brow = lambda b, j: (b, 0, nc - 1 - j)
    seq = lambda m: pl.BlockSpec((1, CHUNK, M_WIDTH), m)
    seq_t = lambda m: pl.BlockSpec((1, M_WIDTH, CHUNK), m)
    gspec = lambda m: pl.BlockSpec((1, N_GATE_STATS, S, CHUNK), m)
    return pl.pallas_call(
        _mlstm_kernel,
        grid=(B, nc),
        in_specs=[seq(fmap), seq_t(frow), seq(fmap), seq(bmap), seq_t(brow), seq(bmap),
                  gspec(lambda b, j: (b, 0, j, 0)), gspec(lambda b, j: (b, 0, nc - 1 - j, 0)),
                  pl.BlockSpec((1, S, M_DIM, M_DIM + LANES), lambda b, j: (b, 0, 0, 0)),
                  pl.BlockSpec((1, S, CHUNK), lambda b, j: (b, 0, 0))],
        out_specs=[seq(fmap), seq(bmap)],
        out_shape=[jax.ShapeDtypeStruct((B, N, M_WIDTH), bf16), jax.ShapeDtypeStruct((B, N, M_WIDTH), bf16)],
        scratch_shapes=[pltpu.VMEM((S, M_DIM, M_DIM + LANES), f32), pltpu.VMEM((S, CHUNK), f32)],
        compiler_params=pltpu.CompilerParams(dimension_semantics=("parallel", "arbitrary")),
        name="mlstm_scan",
    )(mq, mkt, mv, mq, mkt, mv, gstats, gstats, cn0, m0)


def _outproj_kernel(x_ref, att_ref, hf_ref, hb_ref, mo_ref, mnw_ref, wo_ref, g1_ref, n2w_ref, sh2_ref, sc2_ref,
                    wr_ref, x1_ref, h2_ref, aff_ref, affr_ref, *, tm):
    parts = []
    for h in range(M_HEADS):
        hs = slice(h * M_DIM, (h + 1) * M_DIM)
        s = hf_ref[0, :, hs].astype(f32) + hb_ref[0, :, hs].astype(f32)
        ms = jnp.mean(s * s, axis=-1, keepdims=True)
        hn = s * lax.rsqrt(ms + EPS) * mnw_ref[:, hs]
        gate = 1.0 / (1.0 + jnp.exp(-mo_ref[0, :, hs].astype(f32)))
        parts.append((hn * gate).astype(bf16))
    ml = jnp.concatenate(parts, axis=-1)
    proj = _dot(att_ref[0], wo_ref[0:ATT_WIDTH, :]) + _dot(ml, wo_ref[ATT_WIDTH:, :])
    x1 = x_ref[0] + g1_ref[0] * proj
    x1_ref[0] = x1
    ms = jnp.mean(x1 * x1, axis=-1, keepdims=True)
    h2 = (x1 * lax.rsqrt(ms + EPS) * n2w_ref[...]) * (1.0 + sc2_ref[0]) + sh2_ref[0]
    for j in range(SUBLANES):
        h2_ref[0, pl.ds(j, tm, stride=SUBLANES), :] = h2[:, j * LANES:(j + 1) * LANES]
    hh = h2.astype(bf16)
    hl = (h2 - hh.astype(f32)).astype(bf16)
    n_exp = aff_ref.shape[1]
    both = _dot(hh, wr_ref[...])
    logits_rows = both[:, :LANES] + both[:, LANES:] + _dot(hl, wr_ref[:, :LANES])
    logits = logits_rows.T[:n_exp]
    mx = jnp.max(logits, axis=0, keepdims=True)
    e = jnp.exp(logits - mx)
    aff = e / jnp.sum(e, axis=0, keepdims=True)
    aff_ref[0] = aff
    affr_ref[0] = aff.T


def _out_projection(x, att, hf, hb, mo, mnw, w_out, g1, n2w, sh2, sc2, wr_split, n_experts, *, tm):
    B, N, D = x.shape
    E = n_experts
    row_map = lambda b, i: (b, i, 0)
    const2 = lambda b, i: (0, 0)
    bvec = lambda b, i: (b, 0, 0)
    return pl.pallas_call(
        functools.partial(_outproj_kernel, tm=tm),
        grid=(B, N // tm),
        in_specs=[pl.BlockSpec((1, tm, D), row_map),
                  pl.BlockSpec((1, tm, ATT_WIDTH), row_map),
                  pl.BlockSpec((1, tm, M_WIDTH), row_map),
                  pl.BlockSpec((1, tm, M_WIDTH), row_map),
                  pl.BlockSpec((1, tm, M_WIDTH), row_map),
                  pl.BlockSpec((1, M_WIDTH), const2),
                  pl.BlockSpec((ATT_WIDTH + M_WIDTH, D), const2),
                  pl.BlockSpec((1, 1, D), bvec),
                  pl.BlockSpec((1, D), const2),
                  pl.BlockSpec((1, 1, D), bvec),
                  pl.BlockSpec((1, 1, D), bvec),
                  pl.BlockSpec((D, 2 * LANES), const2)],
        out_specs=[pl.BlockSpec((1, tm, D), row_map),
                   pl.BlockSpec((1, tm * SUBLANES, LANES), row_map),
                   pl.BlockSpec((1, E, tm), lambda b, i: (b, 0, i)),
                   pl.BlockSpec((1, tm, E), row_map)],
        out_shape=[jax.ShapeDtypeStruct((B, N, D), f32),
                   jax.ShapeDtypeStruct((B, N * SUBLANES, LANES), f32),
                   jax.ShapeDtypeStruct((B, E, N), f32),
                   jax.ShapeDtypeStruct((B, N, E), f32)],
        compiler_params=pltpu.CompilerParams(
            dimension_semantics=("parallel", "parallel"), vmem_limit_bytes=VMEM_LIMIT_BYTES),
        name="out_projection",
    )(x, att, hf, hb, mo, mnw, w_out, g1, n2w, sh2, sc2, wr_split)


def _chunk_stride(cap):
    return cap + SUBLANES


ROUTE_EXPONENT_BITS = (64, 32, 16, 8, 4, 2, 1)
ROUTE_REFINE_STEPS = 25
ROUTE_TINY = 1e-30


def _route_kernel(aff_ref, idx_ref, cl_scr, *, cap):
    E, N = aff_ref.shape[1], aff_ref.shape[2]
    aff = aff_ref[0]

    def count_above(t):
        return jnp.sum(jnp.where(aff > t, 1.0, 0.0), axis=-1, keepdims=True)

    hi = jnp.full((E, 1), 2.0, f32)
    for bit in ROUTE_EXPONENT_BITS:
        cand = hi * (2.0 ** -bit)
        hi = jnp.where(count_above(cand) < cap, cand, hi)
    lo = jnp.where(hi < ROUTE_TINY, -1.0, 0.5 * hi)
    for _ in range(ROUTE_REFINE_STEPS):
        q = 0.25 * (hi - lo)
        m1, m2, m3 = lo + q, lo + 2.0 * q, lo + 3.0 * q
        ok1, ok2, ok3 = [count_above(m) >= cap for m in (m1, m2, m3)]
        lo, hi = (jnp.where(ok3, m3, jnp.where(ok2, m2, jnp.where(ok1, m1, lo))),
                  jnp.where(ok3, hi, jnp.where(ok2, m3, jnp.where(ok1, m2, m1))))
    above = jnp.where(aff > hi, 1.0, 0.0)
    equal = jnp.where(aff > lo, 1.0, 0.0) - above
    need = cap - jnp.sum(above, axis=-1, keepdims=True)
    n_chunks = N // LANES
    assert n_chunks <= LANES
    _, tri_t = _tri(LANES)
    lane = lax.broadcasted_iota(jnp.int32, (1, LANES), 1)
    both = jnp.concatenate([above, equal], axis=0).astype(bf16)
    run = jnp.zeros((2 * E, 1), f32)
    start = jnp.zeros((E, 1), f32)
    ends = jnp.full((E, LANES), float(2 * N), f32)
    cl_scr[...] = jnp.zeros_like(cl_scr)
    for k in range(n_chunks):
        ck = _dot(both[:, k * LANES:(k + 1) * LANES], tri_t)
        count = (ck[:E] + run[:E]) + jnp.minimum(ck[E:] + run[E:], need)
        local = count - start
        for e in range(E):
            cl_scr[e, k:k + 1, :] = local[e:e + 1, :]
        start = count[:, LANES - 1:LANES]
        ends = jnp.where(lane == k, start, ends)
        run = run + ck[:, LANES - 1:LANES]

    slot = lax.broadcasted_iota(jnp.int32, (cap, 1), 0).astype(f32)
    for e in range(E):
        ends_e = ends[e:e + 1, :]
        before = ends_e <= slot
        chunk = jnp.sum(jnp.where(before, 1.0, 0.0), axis=-1, keepdims=True)
        chunk_start = jnp.max(jnp.where(before, ends_e, 0.0), axis=-1, keepdims=True)
        pick = jnp.where(lane.astype(f32) == chunk, 1.0, 0.0).astype(bf16)
        local = _dot(pick, cl_scr[e].astype(bf16))
        pos = jnp.sum(jnp.where(local <= slot - chunk_start, 1.0, 0.0), axis=-1, keepdims=True)
        idx_ref[0, e] = (chunk * float(LANES) + pos).astype(jnp.int32)


def _route(aff_t, cap):
    B, E, N = aff_t.shape
    return pl.pallas_call(
        functools.partial(_route_kernel, cap=cap),
        grid=(B,),
        in_specs=[pl.BlockSpec((1, E, N), lambda b: (b, 0, 0))],
        out_specs=pl.BlockSpec((1, E, cap, 1), lambda b: (b, 0, 0, 0)),
        out_shape=jax.ShapeDtypeStruct((B, E, cap, 1), jnp.int32),
        scratch_shapes=[pltpu.VMEM((E, LANES, LANES), f32)],
        compiler_params=pltpu.CompilerParams(dimension_semantics=("parallel",)),
        name="expert_choice_route",
    )(aff_t)


SCATTER_UNROLL = 8
EXPERT_VMEM_LIMIT_BYTES = 60 * 1024 * 1024


def _expert_kernel(idx_prev_ref, idx_ref, idx_next_ref, src_ref, affr_ref, wg_ref, wu_ref, wd_ref, acc_ref,
                   x_even, x_odd, g_even, g_odd, y_even, y_odd, *, cap, n_experts):
    e = pl.program_id(1)
    stride = _chunk_stride(cap)

    def gather(ids_ref, x_scr, g_scr):
        for i in range(cap):
            t = ids_ref[0, 0, i]
            x_scr[pl.ds(i, SUBLANES, stride=stride), :] = (
                src_ref[0, pl.ds(pl.multiple_of(t * SUBLANES, SUBLANES), SUBLANES), :])
            g_scr[pl.ds(i, 1), :] = affr_ref[0, pl.ds(t, 1), :]

    def ffn(x_scr, g_scr, y_scr):
        xg = jnp.concatenate([x_scr[j * stride:j * stride + cap, :].astype(bf16) for j in range(SUBLANES)], axis=1)
        lane = lax.broadcasted_iota(jnp.int32, (1, n_experts), 1)
        gate = jnp.sum(jnp.where(lane == e, g_scr[...], 0.0), axis=-1, keepdims=True)
        g = _dot(xg, wg_ref[0])
        u = _dot(xg, wu_ref[0])
        hmid = (_silu(g) * u).astype(bf16)
        y = _dot(hmid, wd_ref[0]) * gate
        for j in range(y.shape[1] // LANES):
            y_scr[j * stride:j * stride + cap, :] = y[:, j * LANES:(j + 1) * LANES]

    def scatter(ids_ref, y_scr):
        for i0 in range(0, cap, SCATTER_UNROLL):
            rows = [pl.multiple_of(ids_ref[0, 0, i0 + u] * SUBLANES, SUBLANES) for u in range(SCATTER_UNROLL)]
            vals = [acc_ref[0, pl.ds(rows[u], SUBLANES), :] + y_scr[pl.ds(i0 + u, SUBLANES, stride=stride), :]
                    for u in range(SCATTER_UNROLL)]
            for u in range(SCATTER_UNROLL):
                acc_ref[0, pl.ds(rows[u], SUBLANES), :] = vals[u]

    @pl.when(e == 0)
    def _():
        acc_ref[...] = jnp.zeros_like(acc_ref)
        y_odd[...] = jnp.zeros_like(y_odd)
        gather(idx_ref, x_even, g_even)

    @pl.when(e % 2 == 0)
    def _():
        gather(idx_next_ref, x_odd, g_odd)
        ffn(x_even, g_even, y_even)
        scatter(idx_prev_ref, y_odd)

    @pl.when(e % 2 == 1)
    def _():
        gather(idx_next_ref, x_even, g_even)
        ffn(x_odd, g_odd, y_odd)
        scatter(idx_prev_ref, y_even)

    @pl.when(e == n_experts - 1)
    def _():
        scatter(idx_ref, y_odd if (n_experts - 1) % 2 else y_even)


def _expert_mixture(idx, h2_slab, aff_rows, wg, wu, wd):
    B, E, cap = idx.shape
    D, FF = wg.shape[1], wg.shape[2]
    rows = (D // LANES) * _chunk_stride(cap)
    wmap = lambda b, e: (e, 0, 0)
    bmap = lambda b, e: (b, 0, 0)
    ids = idx.reshape(B * E, 1, cap)
    ids_spec = lambda m: pl.BlockSpec((1, 1, cap), m, memory_space=pltpu.SMEM)
    return pl.pallas_call(
        functools.partial(_expert_kernel, cap=cap, n_experts=E),
        grid=(B, E),
        in_specs=[ids_spec(lambda b, e: (b * E + jnp.maximum(e - 1, 0), 0, 0)),
                  ids_spec(lambda b, e: (b * E + e, 0, 0)),
                  ids_spec(lambda b, e: (b * E + jnp.minimum(e + 1, E - 1), 0, 0)),
                  pl.BlockSpec((1,) + h2_slab.shape[1:], bmap, pipeline_mode=pl.Buffered(1)),
                  pl.BlockSpec((1,) + aff_rows.shape[1:], bmap, pipeline_mode=pl.Buffered(1)),
                  pl.BlockSpec((1, D, FF), wmap),
                  pl.BlockSpec((1, D, FF), wmap),
                  pl.BlockSpec((1, FF, D), wmap)],
        out_specs=pl.BlockSpec((1,) + h2_slab.shape[1:], bmap, pipeline_mode=pl.Buffered(1)),
        out_shape=jax.ShapeDtypeStruct(h2_slab.shape, f32),
        scratch_shapes=[pltpu.VMEM((rows, LANES), f32), pltpu.VMEM((rows, LANES), f32),
                        pltpu.VMEM((cap, E), f32), pltpu.VMEM((cap, E), f32),
                        pltpu.VMEM((rows, LANES), f32), pltpu.VMEM((rows, LANES), f32)],
        compiler_params=pltpu.CompilerParams(
            dimension_semantics=("parallel", "arbitrary"), vmem_limit_bytes=EXPERT_VMEM_LIMIT_BYTES),
        name="expert_mixture",
    )(ids, ids, ids, h2_slab, aff_rows, wg, wu, wd)


def _final_kernel(x1_ref, g2_ref, acc_ref, o_ref, *, tm):
    ffn = jnp.concatenate([acc_ref[0, pl.ds(j, tm, stride=SUBLANES), :] for j in range(SUBLANES)], axis=-1)
    o_ref[0] = x1_ref[0] + g2_ref[0] * ffn


def _final_residual(x1, g2, acc, *, tm):
    B, N, D = x1.shape
    row_map = lambda b, i: (b, i, 0)
    return pl.pallas_call(
        functools.partial(_final_kernel, tm=tm),
        grid=(B, N // tm),
        in_specs=[pl.BlockSpec((1, tm, D), row_map),
                  pl.BlockSpec((1, 1, D), lambda b, i: (b, 0, 0)),
                  pl.BlockSpec((1, tm * SUBLANES, LANES), row_map)],
        out_specs=pl.BlockSpec((1, tm, D), row_map),
        out_shape=jax.ShapeDtypeStruct((B, N, D), f32),
        compiler_params=pltpu.CompilerParams(dimension_semantics=("parallel", "parallel")),
        name="final_residual",
    )(x1, g2, acc)


def _rope_tables(n):
    rows = n // GRID_W
    row, col = jnp.meshgrid(jnp.arange(rows), jnp.arange(GRID_W), indexing='ij')
    n_freq = HEAD_DIM // 4
    freqs = ROPE_BASE ** (-jnp.arange(n_freq, dtype=f32) / n_freq)
    ang = jnp.concatenate([row.reshape(-1, 1).astype(f32) * freqs, col.reshape(-1, 1).astype(f32) * freqs], -1)
    cos, sin = jnp.cos(ang), jnp.sin(ang)
    reps = LANES // HEAD_DIM
    return (jnp.tile(jnp.concatenate([cos, cos], -1), (1, reps)),
            jnp.tile(jnp.concatenate([-sin, sin], -1), (1, reps)))


def _pack_w_in(w_in):
    D = w_in.shape[0]
    o = 0
    aq = w_in[:, o:o + ATT_WIDTH]; o += ATT_WIDTH
    ak = w_in[:, o:o + KV_WIDTH]; o += KV_WIDTH
    av = w_in[:, o:o + KV_WIDTH]; o += KV_WIDTH
    mq = w_in[:, o:o + M_WIDTH]; o += M_WIDTH
    mk = w_in[:, o:o + M_WIDTH]; o += M_WIDTH
    mv = w_in[:, o:o + M_WIDTH]; o += M_WIDTH
    mo = w_in[:, o:o + M_WIDTH]; o += M_WIDTH
    gates = _gate_order(w_in[:, o:o + N_GATES])

    def dup(a):
        a = a.reshape(D, N_KV_HEADS, 1, HEAD_DIM)
        return jnp.broadcast_to(a, (D, N_KV_HEADS, 2, HEAD_DIM)).reshape(D, K2_WIDTH)

    pad = jnp.zeros((D, LANES - N_GATES), w_in.dtype)
    return jnp.concatenate([aq, dup(ak), av, mq, mk, mv, mo, gates, pad], -1).astype(bf16)


def _gate_order(g):
    H = M_HEADS
    return jnp.concatenate([g[..., 0:H], g[..., 2 * H:3 * H], g[..., H:2 * H], g[..., 3 * H:4 * H]], -1)


def _layer(x, ctx, mod_x, mod_c, norm1_w, norm2_w, w_in, b_gates, conv_qk, q_norm_w, k_norm_w, sink,
           mlstm_norm_w, w_out, w_router, w_gate, w_up, w_down):
    B, N, D = x.shape
    L = ctx.shape[1]
    assert D == SUBLANES * LANES, "token rows are handled as one (8, 128) register tile"
    tm = min(ROW_TILE, N)
    sh1, sc1, g1, sh2, sc2, g2 = [m[:, None, :] for m in jnp.split(mod_x, 6, -1)]
    csh1, csc1 = [jnp.broadcast_to(m[None, None, :], (B, 1, D)) for m in jnp.split(mod_c, 6, -1)[:2]]

    w_all = _pack_w_in(w_in)
    qk_w = jnp.concatenate([jnp.tile(q_norm_w, N_HEADS), jnp.tile(k_norm_w, 2 * N_KV_HEADS)])[None, :]
    bg = jnp.concatenate([_gate_order(b_gates), jnp.zeros((LANES - N_GATES,), f32)])[None, :]
    nw1 = norm1_w[None, :]
    cos_t, sin_t = _rope_tables(N)
    ones_t, zeros_t = jnp.ones((L, LANES), f32), jnp.zeros((L, LANES), f32)

    aq, k2, vt, mq, _, mkt, mv, mo, _, li_rc, lf_rc = _in_projection(x, sh1, sc1, nw1, w_all, cos_t, sin_t, qk_w,
                                                                     conv_qk, bg, tm=tm)
    _, ck2, cvt, _, cmk, _, cmv, _, cgts, _, _ = _in_projection(ctx, csh1, csc1, nw1, w_all, ones_t, zeros_t, qk_w,
                                                                conv_qk, bg, tm=L)

    att = _attention(sink, aq, k2, vt, ck2, cvt)

    n_streams = 2 * M_HEADS
    cn0, m0 = _ctx_states(cmk, cmv, cgts[..., :n_streams], cgts[..., n_streams:])
    m0 = jnp.broadcast_to(jnp.swapaxes(m0, 1, 2), (B, n_streams, CHUNK))
    gstats = _gate_prep(li_rc, lf_rc)
    hf, hb = _mlstm(mq, mkt, mv, gstats, cn0, m0)

    wr_pad = jnp.pad(w_router, ((0, 0), (0, LANES - N_EXPERTS)))
    wr_hi = wr_pad.astype(bf16)
    wr_split = jnp.concatenate([wr_hi, (wr_pad - wr_hi.astype(f32)).astype(bf16)], axis=1)
    x1, h2_slab, aff_t, aff_rows = _out_projection(x, att, hf, hb, mo, mlstm_norm_w[None, :], w_out.astype(bf16),
                                                   g1, norm2_w[None, :], sh2, sc2, wr_split, N_EXPERTS, tm=tm)

    cap = CAPACITY * N // N_EXPERTS
    idx = _route(aff_t, cap).reshape(B, N_EXPERTS, cap)
    acc = _expert_mixture(idx, h2_slab, aff_rows, w_gate.astype(bf16), w_up.astype(bf16), w_down.astype(bf16))
    return _final_residual(x1, g2, acc, tm=tm)


def kernel(x, c, ctx, c_ctx, w_mod, b_mod, norm1_w, norm2_w, w_in, b_gates, conv_qk, q_norm_w, k_norm_w, sink,
           mlstm_norm_w, w_out, w_router, w_gate, w_up, w_down):
    depth = w_mod.shape[0]
    assert depth == 1, "only the final-layer (no context update) form of the block is implemented"
    B = x.shape[0]
    pad_rows = (-(B + 1)) % SUBLANES
    c_all = jnp.concatenate([c, c_ctx[None, :], jnp.zeros((pad_rows, c.shape[1]), c.dtype)], 0)
    mod = _modulation(c_all, w_mod[0], b_mod[0])
    return _layer(x, ctx, mod[:B], mod[B], norm1_w[0], norm2_w[0], w_in[0], b_gates[0], conv_qk[0], q_norm_w[0],
                  k_norm_w[0], sink[0], mlstm_norm_w[0], w_out[0], w_router[0], w_gate[0], w_up[0], w_down[0])
```

```python
import functools

import jax
import jax.numpy as jnp
from jax import lax
from jax.experimental import pallas as pl
from jax.experimental.pallas import tpu as pltpu

f32 = jnp.float32
bf16 = jnp.bfloat16

GRID_W = 64
N_HEADS = 8
N_KV_HEADS = 2
HEAD_DIM = 64
WINDOW = 128
BLOCK = 128
ROPE_BASE = 10000.0
M_HEADS = 4
M_DIM = 128
CHUNK = 128
CONV_W = 3
ATT_WIDTH = N_HEADS * HEAD_DIM
KV_WIDTH = N_KV_HEADS * HEAD_DIM
M_WIDTH = M_HEADS * M_DIM
N_EXPERTS = 16
CAPACITY = 2
EPS = 1e-6
NEG = -1e30
LOG2E = 1.4426950408889634
Q_SCALE = HEAD_DIM ** -0.5 * LOG2E

LANES = 128
SUBLANES = 8
VMEM_LIMIT_BYTES = 56 * 1024 * 1024
ROW_TILE = 1024
MOD_COL_TILE = 512

K2_WIDTH = 2 * KV_WIDTH
C_Q = 0
C_K = C_Q + ATT_WIDTH
C_V = C_K + K2_WIDTH
C_MQK = C_V + KV_WIDTH
C_MV = C_MQK + 2 * M_WIDTH
C_MO = C_MV + M_WIDTH
C_G = C_MO + M_WIDTH
W_COLS = C_G + LANES
N_GATES = 4 * M_HEADS


def _dot(a, b):
    return jnp.dot(a, b, preferred_element_type=f32)


def _dot_nt(a, b):
    return lax.dot_general(a, b, (((1,), (1,)), ((), ())), preferred_element_type=f32)


def _dot_tn(a, b):
    return lax.dot_general(a, b, (((0,), (0,)), ((), ())), preferred_element_type=f32)


def _ones_cols(v):
    return jnp.concatenate([v, jnp.ones((v.shape[0], LANES), f32).astype(bf16)], axis=1)


def _split3(x):
    h = x.astype(bf16)
    r = x - h.astype(f32)
    m = r.astype(bf16)
    l = (r - m.astype(f32)).astype(bf16)
    return h, m, l


def _log_sigmoid(x):
    return jnp.minimum(x, 0.0) - jnp.log1p(jnp.exp(-jnp.abs(x)))


def _silu(x):
    return x / (1.0 + jnp.exp(-x))


def _mod_kernel(c_ref, w_ref, b_ref, o_ref):
    h, m, l = _split3(_silu(c_ref[...]))
    w = w_ref[...]
    wh = w.astype(bf16)
    wl = (w - wh.astype(f32)).astype(bf16)
    acc = _dot(h, wh) + _dot(m, wh) + _dot(h, wl) + _dot(l, wh) + _dot(m, wl)
    o_ref[...] = acc + b_ref[...]


def _modulation(c_all, w_mod, b_mod):
    rows, d = c_all.shape
    cols = w_mod.shape[1]
    tn = MOD_COL_TILE
    return pl.pallas_call(
        _mod_kernel,
        grid=(cols // tn,),
        in_specs=[pl.BlockSpec((rows, d), lambda j: (0, 0)),
                  pl.BlockSpec((d, tn), lambda j: (0, j)),
                  pl.BlockSpec((1, tn), lambda j: (0, j))],
        out_specs=pl.BlockSpec((rows, tn), lambda j: (0, j)),
        out_shape=jax.ShapeDtypeStruct((rows, cols), f32),
        name="adaln_mod",
    )(c_all, w_mod, b_mod.reshape(1, cols))


def _inproj_kernel(xp_ref, x_ref, xn_ref, shift_ref, scale_ref, nw_ref, w_ref, wvt_ref, cos_ref, sin_ref, qkw_ref,
                   conv_ref, bg_ref,
                   aq_ref, k2_ref, vt_ref, mq_ref, mk_ref, mkt_ref, mv_ref, mo_ref, g_ref, li_ref, lf_ref,
                   conv_scr, *, tm, nt):
    i = pl.program_id(1)
    nw = nw_ref[...]
    sc = 1.0 + scale_ref[0]
    sh = shift_ref[0]

    def prep(xv):
        ms = jnp.mean(xv * xv, axis=-1, keepdims=True)
        return (xv * lax.rsqrt(ms + EPS) * nw) * sc + sh

    hm = prep(x_ref[0])
    lhs = hm.astype(bf16)
    lhs_halo = jnp.concatenate([prep(xp_ref[0]), hm, prep(xn_ref[0])], axis=0).astype(bf16)

    lane = lax.broadcasted_iota(jnp.int32, (1, LANES), 1)
    lo = lane < HEAD_DIM
    first_half = (lane % HEAD_DIM) < (HEAD_DIM // 2)
    cos = cos_ref[...]
    sin = sin_ref[...]
    n_qk = (ATT_WIDTH + K2_WIDTH) // LANES
    vqk = _dot(lhs, w_ref[:, C_Q:C_Q + ATT_WIDTH + K2_WIDTH])
    for grp_i in range(n_qk):
        c0 = grp_i * LANES
        v = vqk[:, c0:c0 + LANES]
        sq = v * v
        s_all = jnp.sum(sq, axis=-1, keepdims=True)
        s_lo = jnp.sum(jnp.where(lo, sq, 0.0), axis=-1, keepdims=True)
        ms = jnp.where(lo, s_lo, s_all - s_lo) * (1.0 / HEAD_DIM)
        nv = v * lax.rsqrt(ms + EPS) * qkw_ref[:, c0:c0 + LANES]
        swapped = jnp.where(first_half, pltpu.roll(nv, LANES - HEAD_DIM // 2, 1), pltpu.roll(nv, HEAD_DIM // 2, 1))
        r = nv * cos + swapped * sin
        if c0 < ATT_WIDTH:
            aq_ref[0, :, c0:c0 + LANES] = (r * Q_SCALE).astype(bf16)
        else:
            k2_ref[0, :, c0 - ATT_WIDTH:c0 - ATT_WIDTH + LANES] = r.astype(bf16)

    vt_ref[0] = _dot_nt(wvt_ref[...], lhs).astype(bf16)
    mv_ref[0] = _dot(lhs, w_ref[:, C_MV:C_MV + M_WIDTH]).astype(bf16)
    mo_ref[0] = _dot(lhs, w_ref[:, C_MO:C_MO + M_WIDTH]).astype(bf16)

    gates = _dot(lhs, w_ref[:, C_G:C_G + LANES]) + bg_ref[...]
    n_streams = 2 * M_HEADS
    is_forget = (lane >= n_streams) & (lane < N_GATES)
    gates = jnp.where(is_forget, _log_sigmoid(gates), gates)
    g_ref[0] = gates[:, :N_GATES]
    for c in range(tm // CHUNK):
        t_c = gates[c * CHUNK:(c + 1) * CHUNK, :].T
        li_ref[0, c * n_streams:(c + 1) * n_streams, :] = t_c[0:n_streams]
        lf_ref[0, c * n_streams:(c + 1) * n_streams, :] = t_c[n_streams:2 * n_streams]

    conv_scr[...] = _dot(lhs_halo, w_ref[:, C_MQK:C_MQK + 2 * M_WIDTH])
    row = lax.broadcasted_iota(jnp.int32, (tm, 1), 0)
    prev = conv_scr[SUBLANES - 1:SUBLANES - 1 + tm, :]
    prev = jnp.where((row == 0) & (i == 0), 0.0, prev)
    nxt = conv_scr[SUBLANES + 1:SUBLANES + 1 + tm, :]
    nxt = jnp.where((row == tm - 1) & (i == nt - 1), 0.0, nxt)
    cur = conv_scr[SUBLANES:SUBLANES + tm, :]
    u = prev * conv_ref[0:1, :] + cur * conv_ref[1:2, :] + nxt * conv_ref[2:3, :]
    u = _silu(u)
    mq_ref[0] = (u[:, :M_WIDTH] * (M_DIM ** -0.5)).astype(bf16)
    mk = u[:, M_WIDTH:]
    mk_ref[0] = mk.astype(bf16)
    mkt_ref[0] = mk.T.astype(bf16)


def _in_projection(x, shift, scale, norm_w, w_all, cos_t, sin_t, qk_w, conv_qk, bg, *, tm):
    B, N, D = x.shape
    nt = N // tm
    hb = tm // SUBLANES
    nblk8 = N // SUBLANES
    kern = functools.partial(_inproj_kernel, tm=tm, nt=nt)
    gate_rows = (tm // CHUNK) * 2 * M_HEADS
    row_map = lambda b, i: (b, i, 0)
    const2 = lambda b, i: (0, 0)
    outs = pl.pallas_call(
        kern,
        grid=(B, nt),
        in_specs=[
            pl.BlockSpec((1, SUBLANES, D), lambda b, i: (b, jnp.maximum(i * hb - 1, 0), 0)),
            pl.BlockSpec((1, tm, D), row_map),
            pl.BlockSpec((1, SUBLANES, D), lambda b, i: (b, jnp.minimum((i + 1) * hb, nblk8 - 1), 0)),
            pl.BlockSpec((1, 1, D), lambda b, i: (b, 0, 0)),
            pl.BlockSpec((1, 1, D), lambda b, i: (b, 0, 0)),
            pl.BlockSpec((1, D), const2),
            pl.BlockSpec((D, W_COLS), const2),
            pl.BlockSpec((KV_WIDTH, D), const2),
            pl.BlockSpec((tm, LANES), lambda b, i: (i, 0)),
            pl.BlockSpec((tm, LANES), lambda b, i: (i, 0)),
            pl.BlockSpec((1, ATT_WIDTH + K2_WIDTH), const2),
            pl.BlockSpec((CONV_W, 2 * M_WIDTH), const2),
            pl.BlockSpec((1, LANES), const2),
        ],
        out_specs=[
            pl.BlockSpec((1, tm, ATT_WIDTH), row_map),
            pl.BlockSpec((1, tm, K2_WIDTH), row_map),
            pl.BlockSpec((1, KV_WIDTH, tm), lambda b, i: (b, 0, i)),
            pl.BlockSpec((1, tm, M_WIDTH), row_map),
            pl.BlockSpec((1, tm, M_WIDTH), row_map),
            pl.BlockSpec((1, M_WIDTH, tm), lambda b, i: (b, 0, i)),
            pl.BlockSpec((1, tm, M_WIDTH), row_map),
            pl.BlockSpec((1, tm, M_WIDTH), row_map),
            pl.BlockSpec((1, tm, N_GATES), row_map),
            pl.BlockSpec((1, gate_rows, CHUNK), row_map),
            pl.BlockSpec((1, gate_rows, CHUNK), row_map),
        ],
        out_shape=[
            jax.ShapeDtypeStruct((B, N, ATT_WIDTH), bf16),
            jax.ShapeDtypeStruct((B, N, K2_WIDTH), bf16),
            jax.ShapeDtypeStruct((B, KV_WIDTH, N), bf16),
            jax.ShapeDtypeStruct((B, N, M_WIDTH), bf16),
            jax.ShapeDtypeStruct((B, N, M_WIDTH), bf16),
            jax.ShapeDtypeStruct((B, M_WIDTH, N), bf16),
            jax.ShapeDtypeStruct((B, N, M_WIDTH), bf16),
            jax.ShapeDtypeStruct((B, N, M_WIDTH), bf16),
            jax.ShapeDtypeStruct((B, N, N_GATES), f32),
            jax.ShapeDtypeStruct((B, nt * gate_rows, CHUNK), f32),
            jax.ShapeDtypeStruct((B, nt * gate_rows, CHUNK), f32),
        ],
        scratch_shapes=[pltpu.VMEM((tm + 2 * SUBLANES, 2 * M_WIDTH), f32)],
        compiler_params=pltpu.CompilerParams(
            dimension_semantics=("parallel", "parallel"), vmem_limit_bytes=VMEM_LIMIT_BYTES),
        name="in_projection",
    )(x, x, x, shift, scale, norm_w, w_all, w_all[:, C_V:C_V + KV_WIDTH].T, cos_t, sin_t, qk_w, conv_qk, bg)
    return outs


def _attn_kernel(sink_ref, q_ref, kp_ref, kc_ref, kn_ref, vp_ref, vc_ref, vn_ref, ck_ref, cv_ref, o_ref, *, nb):
    n = pl.program_id(1)
    span = BLOCK + 2 * WINDOW
    grp = N_HEADS // N_KV_HEADS
    lane = lax.broadcasted_iota(jnp.int32, (1, LANES), 1)
    lo = lane < HEAD_DIM
    kj = lax.broadcasted_iota(jnp.int32, (span, BLOCK), 0)
    qi = lax.broadcasted_iota(jnp.int32, (span, BLOCK), 1)
    rel = kj - qi
    valid = (rel >= 0) & (rel <= 2 * WINDOW)
    valid = valid & ((kj >= WINDOW) | (n > 0)) & ((kj < WINDOW + BLOCK) | (n < nb - 1))
    bias = jnp.where(valid, 0.0, NEG)
    bias4 = jnp.concatenate([bias] * grp, axis=1)
    col = lax.broadcasted_iota(jnp.int32, (1, grp * BLOCK), 1)
    n_ctx = ck_ref.shape[1]
    ones_rows = jnp.ones((HEAD_DIM, span + n_ctx), f32).astype(bf16)
    zero = jnp.zeros((), bf16)
    scores = []
    for kh in range(N_KV_HEADS):
        ks = slice(kh * LANES, (kh + 1) * LANES)
        keys = jnp.concatenate([kp_ref[0, :, ks], kc_ref[0, :, ks], kn_ref[0, :, ks], ck_ref[0, :, ks]], axis=0)
        q0 = kh * grp * HEAD_DIM
        q2 = jnp.concatenate([q_ref[0, :, q0:q0 + LANES], q_ref[0, :, q0 + LANES:q0 + 2 * LANES]], axis=0)
        q4 = jnp.concatenate([jnp.where(lo, q2, zero), jnp.where(lo, zero, q2)], axis=0)
        scores.append(_dot_nt(keys, q4))
    probs = []
    for kh in range(N_KV_HEADS):
        heads = (0, 2, 1, 3)
        sk = sink_ref[kh * grp + heads[0]]
        for c, g in enumerate(heads[1:], start=1):
            sk = jnp.where(col >= c * BLOCK, sink_ref[kh * grp + g], sk)
        sk = sk * LOG2E
        s_t = scores[kh]
        s_loc = s_t[:span] + bias4
        s_ctx = s_t[span:]
        m = jnp.maximum(jnp.maximum(jnp.max(s_loc, axis=0, keepdims=True),
                                    jnp.max(s_ctx, axis=0, keepdims=True)), sk)
        p_t = jnp.concatenate([jnp.exp2(s_loc - m), jnp.exp2(s_ctx - m)], axis=0).astype(bf16)
        probs.append((p_t, jnp.exp2(sk - m)))
    for kh in range(N_KV_HEADS):
        hs = slice(kh * HEAD_DIM, (kh + 1) * HEAD_DIM)
        vals_t = jnp.concatenate([vp_ref[0, hs, :], vc_ref[0, hs, :], vn_ref[0, hs, :], cv_ref[0, hs, :]], axis=1)
        vals_t = jnp.concatenate([vals_t, ones_rows], axis=0)
        q0 = kh * grp * HEAD_DIM
        p_t, p_sink = probs[kh]
        o_t = _dot(vals_t, p_t)
        out_t = o_t[:HEAD_DIM] / (o_t[HEAD_DIM:] + p_sink)
        for pair in range(2):
            even = out_t[:, pair * BLOCK:(pair + 1) * BLOCK]
            odd = out_t[:, (2 + pair) * BLOCK:(3 + pair) * BLOCK]
            both = jnp.concatenate([even, odd], axis=0)
            o_ref[0, :, q0 + pair * LANES:q0 + (pair + 1) * LANES] = both.T.astype(o_ref.dtype)


def _attention(sink, q, k2, vt, ck2, cvt):
    B, N, _ = q.shape
    L = ck2.shape[1]
    nb = N // BLOCK
    kern = functools.partial(_attn_kernel, nb=nb)
    prev_blk = lambda n: jnp.maximum(n - 1, 0)
    next_blk = lambda n: jnp.minimum(n + 1, nb - 1)
    same = lambda n: n
    cur_map = lambda b, n, s: (b, n, 0)
    ctx_map = lambda b, n, s: (b, 0, 0)
    k_spec = lambda f: pl.BlockSpec((1, BLOCK, K2_WIDTH), lambda b, n, s: (b, f(n), 0))
    v_spec = lambda f: pl.BlockSpec((1, KV_WIDTH, BLOCK), lambda b, n, s: (b, 0, f(n)))
    return pl.pallas_call(
        kern,
        grid_spec=pltpu.PrefetchScalarGridSpec(
            num_scalar_prefetch=1,
            grid=(B, nb),
            in_specs=[pl.BlockSpec((1, BLOCK, ATT_WIDTH), cur_map),
                      k_spec(prev_blk), k_spec(same), k_spec(next_blk),
                      v_spec(prev_blk), v_spec(same), v_spec(next_blk),
                      pl.BlockSpec((1, L, K2_WIDTH), ctx_map),
                      pl.BlockSpec((1, KV_WIDTH, L), ctx_map)],
            out_specs=pl.BlockSpec((1, BLOCK, ATT_WIDTH), cur_map),
        ),
        out_shape=jax.ShapeDtypeStruct((B, N, ATT_WIDTH), bf16),
        compiler_params=pltpu.CompilerParams(dimension_semantics=("parallel", "parallel")),
        name="window_attention",
    )(sink, q, k2, k2, k2, vt, vt, vt, ck2, cvt)


def _tri_cumsum_cols(tri, x):
    h, m, l = _split3(x)
    return _dot(tri, h) + _dot(tri, m) + _dot(tri, l)


def _tri_cumsum_rows(x, tri_t):
    h, m, l = _split3(x)
    return _dot(h, tri_t) + _dot(m, tri_t) + _dot(l, tri_t)


def _tri(t):
    r = lax.broadcasted_iota(jnp.int32, (t, t), 0)
    c = lax.broadcasted_iota(jnp.int32, (t, t), 1)
    return jnp.where(c <= r, 1.0, 0.0).astype(bf16), jnp.where(r <= c, 1.0, 0.0).astype(bf16)


def _ctx_state_kernel(k_ref, v_ref, li_ref, lf_ref, c_ref, m_ref):
    L = k_ref.shape[1]
    tri, _ = _tri(L)
    li = li_ref[0]
    lf = lf_ref[0]
    lane = lax.broadcasted_iota(jnp.int32, (1, 2 * M_HEADS), 1)
    fwd = lane < M_HEADS
    cs = _tri_cumsum_cols(tri, lf)
    tot = cs[L - 1:L, :]
    b = jnp.where(fwd, cs, tot - cs + lf)
    w = tot - b + li
    m_new = jnp.maximum(tot, jnp.max(w, axis=0, keepdims=True))
    ws = jnp.exp(w - m_new)
    m_ref[0] = m_new
    for c in range(2 * M_HEADS):
        hs = slice((c % M_HEADS) * M_DIM, (c % M_HEADS + 1) * M_DIM)
        ks = k_ref[0, :, hs].astype(f32) * ws[:, c:c + 1]
        c_ref[0, c] = _dot_tn(ks.astype(bf16), _ones_cols(v_ref[0, :, hs]))


def _ctx_states(cmk, cmv, li_c, lf_c):
    B, L, _ = cmk.shape
    S = 2 * M_HEADS
    bmap = lambda b: (b, 0, 0)
    return pl.pallas_call(
        _ctx_state_kernel,
        grid=(B,),
        in_specs=[pl.BlockSpec((1, L, M_WIDTH), bmap), pl.BlockSpec((1, L, M_WIDTH), bmap),
                  pl.BlockSpec((1, L, S), bmap), pl.BlockSpec((1, L, S), bmap)],
        out_specs=[pl.BlockSpec((1, S, M_DIM, M_DIM + LANES), lambda b: (b, 0, 0, 0)),
                   pl.BlockSpec((1, 1, S), bmap)],
        out_shape=[jax.ShapeDtypeStruct((B, S, M_DIM, M_DIM + LANES), f32),
                   jax.ShapeDtypeStruct((B, 1, S), f32)],
        compiler_params=pltpu.CompilerParams(dimension_semantics=("parallel",)),
        name="mlstm_ctx_state",
    )(cmk, cmv, li_c, lf_c)


N_GATE_STATS = 6


def _gate_prep_kernel(li_ref, lf_ref, o_ref):
    S = 2 * M_HEADS
    T = CHUNK
    li = li_ref[0]
    lf = lf_ref[0]
    rows = li.shape[0]
    _, tri_t = _tri(T)
    fwd = (lax.broadcasted_iota(jnp.int32, (rows, 1), 0) % S) < M_HEADS
    lane = lax.broadcasted_iota(jnp.int32, (1, T), 1)
    cs = _tri_cumsum_rows(lf, tri_t)
    tot = cs[:, T - 1:T]
    b = jnp.where(fwd, cs, tot - cs + lf)
    e = li - b
    pm = e
    shift = 1
    while shift < T:
        from_left = jnp.where(lane >= shift, pltpu.roll(pm, shift, 1), -jnp.inf)
        from_right = jnp.where(lane < T - shift, pltpu.roll(pm, T - shift, 1), -jnp.inf)
        pm = jnp.maximum(pm, jnp.where(fwd, from_left, from_right))
        shift *= 2
    w = tot - b + li
    o_ref[0, 0] = e
    o_ref[0, 1] = pm
    o_ref[0, 2] = b
    o_ref[0, 3] = w
    o_ref[0, 4] = jnp.broadcast_to(tot, (rows, T))
    o_ref[0, 5] = jnp.broadcast_to(jnp.max(w, axis=-1, keepdims=True), (rows, T))


def _gate_prep(li_rc, lf_rc):
    B, rows, T = li_rc.shape
    bmap = lambda b: (b, 0, 0)
    return pl.pallas_call(
        _gate_prep_kernel,
        grid=(B,),
        in_specs=[pl.BlockSpec((1, rows, T), bmap), pl.BlockSpec((1, rows, T), bmap)],
        out_specs=pl.BlockSpec((1, N_GATE_STATS, rows, T), lambda b: (b, 0, 0, 0)),
        out_shape=jax.ShapeDtypeStruct((B, N_GATE_STATS, rows, T), f32),
        compiler_params=pltpu.CompilerParams(dimension_semantics=("parallel",)),
        name="mlstm_gate_prep",
    )(li_rc, lf_rc)


def _mlstm_kernel(qf_ref, ktf_ref, vf_ref, qb_ref, ktb_ref, vb_ref, gf_ref, gb_ref,
                  cn0_ref, m0_ref, hf_ref, hb_ref, cn_scr, m_scr):
    j = pl.program_id(1)
    S = 2 * M_HEADS
    T = CHUNK

    @pl.when(j == 0)
    def _():
        cn_scr[...] = cn0_ref[0]
        m_scr[...] = m0_ref[0]

    fwd_r = lax.broadcasted_iota(jnp.int32, (S, 1), 0) < M_HEADS
    stat = lambda k: jnp.where(fwd_r, gf_ref[0, k], gb_ref[0, k])
    e_r, pm, b_r, w_r, tot, w_max = [stat(k) for k in range(N_GATE_STATS)]

    m_prev = m_scr[...]
    g_hi = jnp.maximum(m_prev, pm).astype(bf16)
    g_used = g_hi.astype(f32)
    en_r = jnp.exp(-(b_r + g_used))
    en_hi = en_r.astype(bf16)
    en_lo = (en_r - en_hi.astype(f32)).astype(bf16)
    m_new = jnp.maximum(tot + m_prev, w_max)
    a_r = jnp.exp(tot + m_prev - m_new)
    ws_r = jnp.exp(w_r - m_new)
    m_scr[...] = m_new

    ti = lax.broadcasted_iota(jnp.int32, (T, T), 0)
    si = lax.broadcasted_iota(jnp.int32, (T, T), 1)
    eye = ti == si
    en_hi = en_hi.astype(f32)
    en_lo = en_lo.astype(f32)
    ones_tt = jnp.ones((T, T), f32).astype(bf16)
    zeros_tt = jnp.zeros((T, T), f32).astype(bf16)
    ones_2t = jnp.ones((2 * T, T), f32).astype(bf16)

    def refs(c):
        is_fwd = c < M_HEADS
        hs = slice((c % M_HEADS) * M_DIM, (c % M_HEADS + 1) * M_DIM)
        return ((qf_ref if is_fwd else qb_ref), (ktf_ref if is_fwd else ktb_ref), (vf_ref if is_fwd else vb_ref),
                (hf_ref if is_fwd else hb_ref), hs, is_fwd)

    stage1 = []
    for c in range(S):
        q_ref, kt_ref, _, _, hs, _ = refs(c)
        row = lambda a: a[c:c + 1, :]
        diag = lambda a: jnp.where(eye, row(a), 0.0).astype(bf16)
        q = q_ref[0, :, hs]
        kt = kt_ref[0, hs, :]
        qk_g = _dot(jnp.concatenate([q, diag(g_used)], axis=1),
                    jnp.concatenate([jnp.concatenate([kt, zeros_tt], axis=1),
                                     jnp.concatenate([zeros_tt, ones_tt], axis=1)], axis=0))
        en_rep = _dot(jnp.concatenate([diag(en_hi), diag(en_lo)], axis=1), ones_2t)
        stage1.append((qk_g, en_rep, q))
    stage2 = []
    for c in range(S):
        _, _, v_ref, _, hs, is_fwd = refs(c)
        row = lambda a: a[c:c + 1, :]
        qk_g, _, q = stage1[c]
        g_rep = qk_g[:, T:]
        within = (si <= ti) if is_fwd else (si >= ti)
        p_mat = jnp.exp(jnp.where(within, row(e_r) - g_rep, -jnp.inf))
        wts = p_mat * qk_g[:, :T]
        decay = jnp.exp(row(m_prev) - g_rep)
        lhs = jnp.concatenate([wts.astype(bf16), (decay * q.astype(f32)).astype(bf16)], axis=1)
        rhs = jnp.concatenate([_ones_cols(v_ref[0, :, hs]), cn_scr[c].astype(bf16)], axis=0)
        stage2.append(_dot(lhs, rhs))
    for c in range(S):
        _, _, _, h_ref, hs, _ = refs(c)
        _, en_rep, _ = stage1[c]
        nd = stage2[c]
        h = nd[:, :M_DIM] / jnp.maximum(jnp.abs(nd[:, M_DIM:]), en_rep)
        h_ref[0, :, hs] = h.astype(h_ref.dtype)
    for c in range(S):
        _, kt_ref, v_ref, _, hs, _ = refs(c)
        row = lambda a: a[c:c + 1, :]
        kst = (kt_ref[0, hs, :].astype(f32) * row(ws_r)).astype(bf16)
        a_c = row(a_r)
        cn_scr[c] = jnp.concatenate([a_c, a_c], axis=1) * cn_scr[c] + _dot(kst, _ones_cols(v_ref[0, :, hs]))


def _mlstm(mq, mkt, mv, gstats, cn0, m0):
    B, N, _ = mq.shape
    nc = N // CHUNK
    S = 2 * M_HEADS
    fmap = lambda b, j: (b, j, 0)
    bmap = lambda b, j: (b, nc - 1 - j, 0)
    frow = lambda b, j: (b, 0, j)
    brow = lambda b, j: (b, 0, nc - 1 - j)
    seq = lambda m: pl.BlockSpec((1, CHUNK, M_WIDTH), m)
    seq_t = lambda m: pl.BlockSpec((1, M_WIDTH, CHUNK), m)
    gspec = lambda m: pl.BlockSpec((1, N_GATE_STATS, S, CHUNK), m)
    return pl.pallas_call(
        _mlstm_kernel,
        grid=(B, nc),
        in_specs=[seq(fmap), seq_t(frow), seq(fmap), seq(bmap), seq_t(brow), seq(bmap),
                  gspec(lambda b, j: (b, 0, j, 0)), gspec(lambda b, j: (b, 0, nc - 1 - j, 0)),
                  pl.BlockSpec((1, S, M_DIM, M_DIM + LANES), lambda b, j: (b, 0, 0, 0)),
                  pl.BlockSpec((1, S, CHUNK), lambda b, j: (b, 0, 0))],
        out_specs=[seq(fmap), seq(bmap)],
        out_shape=[jax.ShapeDtypeStruct((B, N, M_WIDTH), bf16), jax.ShapeDtypeStruct((B, N, M_WIDTH), bf16)],
        scratch_shapes=[pltpu.VMEM((S, M_DIM, M_DIM + LANES), f32), pltpu.VMEM((S, CHUNK), f32)],
        compiler_params=pltpu.CompilerParams(dimension_semantics=("parallel", "arbitrary")),
        name="mlstm_scan",
    )(mq, mkt, mv, mq, mkt, mv, gstats, gstats, cn0, m0)


def _outproj_kernel(x_ref, att_ref, hf_ref, hb_ref, mo_ref, mnw_ref, wo_ref, g1_ref, n2w_ref, sh2_ref, sc2_ref,
                    wr_ref, x1_ref, h2_ref, aff_ref, affr_ref, *, tm):
    parts = []
    for h in range(M_HEADS):
        hs = slice(h * M_DIM, (h + 1) * M_DIM)
        s = hf_ref[0, :, hs].astype(f32) + hb_ref[0, :, hs].astype(f32)
        ms = jnp.mean(s * s, axis=-1, keepdims=True)
        hn = s * lax.rsqrt(ms + EPS) * mnw_ref[:, hs]
        gate = 1.0 / (1.0 + jnp.exp(-mo_ref[0, :, hs].astype(f32)))
        parts.append((hn * gate).astype(bf16))
    ml = jnp.concatenate(parts, axis=-1)
    proj = _dot(att_ref[0], wo_ref[0:ATT_WIDTH, :]) + _dot(ml, wo_ref[ATT_WIDTH:, :])
    x1 = x_ref[0] + g1_ref[0] * proj
    x1_ref[0] = x1
    ms = jnp.mean(x1 * x1, axis=-1, keepdims=True)
    h2 = (x1 * lax.rsqrt(ms + EPS) * n2w_ref[...]) * (1.0 + sc2_ref[0]) + sh2_ref[0]
    for j in range(SUBLANES):
        h2_ref[0, pl.ds(j, tm, stride=SUBLANES), :] = h2[:, j * LANES:(j + 1) * LANES]
    hh = h2.astype(bf16)
    hl = (h2 - hh.astype(f32)).astype(bf16)
    n_exp = aff_ref.shape[1]
    both = _dot(hh, wr_ref[...])
    logits_rows = both[:, :LANES] + both[:, LANES:] + _dot(hl, wr_ref[:, :LANES])
    logits = logits_rows.T[:n_exp]
    mx = jnp.max(logits, axis=0, keepdims=True)
    e = jnp.exp(logits - mx)
    aff = e / jnp.sum(e, axis=0, keepdims=True)
    aff_ref[0] = aff
    affr_ref[0] = aff.T


def _out_projection(x, att, hf, hb, mo, mnw, w_out, g1, n2w, sh2, sc2, wr_split, n_experts, *, tm):
    B, N, D = x.shape
    E = n_experts
    row_map = lambda b, i: (b, i, 0)
    const2 = lambda b, i: (0, 0)
    bvec = lambda b, i: (b, 0, 0)
    return pl.pallas_call(
        functools.partial(_outproj_kernel, tm=tm),
        grid=(B, N // tm),
        in_specs=[pl.BlockSpec((1, tm, D), row_map),
                  pl.BlockSpec((1, tm, ATT_WIDTH), row_map),
                  pl.BlockSpec((1, tm, M_WIDTH), row_map),
                  pl.BlockSpec((1, tm, M_WIDTH), row_map),
                  pl.BlockSpec((1, tm, M_WIDTH), row_map),
                  pl.BlockSpec((1, M_WIDTH), const2),
                  pl.BlockSpec((ATT_WIDTH + M_WIDTH, D), const2),
                  pl.BlockSpec((1, 1, D), bvec),
                  pl.BlockSpec((1, D), const2),
                  pl.BlockSpec((1, 1, D), bvec),
                  pl.BlockSpec((1, 1, D), bvec),
                  pl.BlockSpec((D, 2 * LANES), const2)],
        out_specs=[pl.BlockSpec((1, tm, D), row_map),
                   pl.BlockSpec((1, tm * SUBLANES, LANES), row_map),
                   pl.BlockSpec((1, E, tm), lambda b, i: (b, 0, i)),
                   pl.BlockSpec((1, tm, E), row_map)],
        out_shape=[jax.ShapeDtypeStruct((B, N, D), f32),
                   jax.ShapeDtypeStruct((B, N * SUBLANES, LANES), f32),
                   jax.ShapeDtypeStruct((B, E, N), f32),
                   jax.ShapeDtypeStruct((B, N, E), f32)],
        compiler_params=pltpu.CompilerParams(
            dimension_semantics=("parallel", "parallel"), vmem_limit_bytes=VMEM_LIMIT_BYTES),
        name="out_projection",
    )(x, att, hf, hb, mo, mnw, w_out, g1, n2w, sh2, sc2, wr_split)


def _chunk_stride(cap):
    return cap + SUBLANES


ROUTE_EXPONENT_BITS = (64, 32, 16, 8, 4, 2, 1)
ROUTE_REFINE_STEPS = 25
ROUTE_TINY = 1e-30


def _route_kernel(aff_ref, idx_ref, cl_scr, *, cap):
    E, N = aff_ref.shape[1], aff_ref.shape[2]
    aff = aff_ref[0]

    def count_above(t):
        return jnp.sum(jnp.where(aff > t, 1.0, 0.0), axis=-1, keepdims=True)

    hi = jnp.full((E, 1), 2.0, f32)
    for bit in ROUTE_EXPONENT_BITS:
        cand = hi * (2.0 ** -bit)
        hi = jnp.where(count_above(cand) < cap, cand, hi)
    lo = jnp.where(hi < ROUTE_TINY, -1.0, 0.5 * hi)
    for _ in range(ROUTE_REFINE_STEPS):
        q = 0.25 * (hi - lo)
        m1, m2, m3 = lo + q, lo + 2.0 * q, lo + 3.0 * q
        ok1, ok2, ok3 = [count_above(m) >= cap for m in (m1, m2, m3)]
        lo, hi = (jnp.where(ok3, m3, jnp.where(ok2, m2, jnp.where(ok1, m1, lo))),
                  jnp.where(ok3, hi, jnp.where(ok2, m3, jnp.where(ok1, m2, m1))))
    above = jnp.where(aff > hi, 1.0, 0.0)
    equal = jnp.where(aff > lo, 1.0, 0.0) - above
    need = cap - jnp.sum(above, axis=-1, keepdims=True)
    n_chunks = N // LANES
    assert n_chunks <= LANES
    _, tri_t = _tri(LANES)
    lane = lax.broadcasted_iota(jnp.int32, (1, LANES), 1)
    both = jnp.concatenate([above, equal], axis=0).astype(bf16)
    run = jnp.zeros((2 * E, 1), f32)
    start = jnp.zeros((E, 1), f32)
    ends = jnp.full((E, LANES), float(2 * N), f32)
    cl_scr[...] = jnp.zeros_like(cl_scr)
    for k in range(n_chunks):
        ck = _dot(both[:, k * LANES:(k + 1) * LANES], tri_t)
        count = (ck[:E] + run[:E]) + jnp.minimum(ck[E:] + run[E:], need)
        local = count - start
        for e in range(E):
            cl_scr[e, k:k + 1, :] = local[e:e + 1, :]
        start = count[:, LANES - 1:LANES]
        ends = jnp.where(lane == k, start, ends)
        run = run + ck[:, LANES - 1:LANES]

    slot = lax.broadcasted_iota(jnp.int32, (cap, 1), 0).astype(f32)
    for e in range(E):
        ends_e = ends[e:e + 1, :]
        before = ends_e <= slot
        chunk = jnp.sum(jnp.where(before, 1.0, 0.0), axis=-1, keepdims=True)
        chunk_start = jnp.max(jnp.where(before, ends_e, 0.0), axis=-1, keepdims=True)
        pick = jnp.where(lane.astype(f32) == chunk, 1.0, 0.0).astype(bf16)
        local = _dot(pick, cl_scr[e].astype(bf16))
        pos = jnp.sum(jnp.where(local <= slot - chunk_start, 1.0, 0.0), axis=-1, keepdims=True)
        idx_ref[0, e] = (chunk * float(LANES) + pos).astype(jnp.int32)


def _route(aff_t, cap):
    B, E, N = aff_t.shape
    return pl.pallas_call(
        functools.partial(_route_kernel, cap=cap),
        grid=(B,),
        in_specs=[pl.BlockSpec((1, E, N), lambda b: (b, 0, 0))],
        out_specs=pl.BlockSpec((1, E, cap, 1), lambda b: (b, 0, 0, 0)),
        out_shape=jax.ShapeDtypeStruct((B, E, cap, 1), jnp.int32),
        scratch_shapes=[pltpu.VMEM((E, LANES, LANES), f32)],
        compiler_params=pltpu.CompilerParams(dimension_semantics=("parallel",)),
        name="expert_choice_route",
    )(aff_t)


SCATTER_UNROLL = 8
EXPERT_VMEM_LIMIT_BYTES = 60 * 1024 * 1024


def _expert_kernel(idx_prev_ref, idx_ref, idx_next_ref, src_ref, affr_ref, wg_ref, wu_ref, wd_ref, acc_ref,
                   x_even, x_odd, g_even, g_odd, y_even, y_odd, *, cap, n_experts):
    e = pl.program_id(1)
    stride = _chunk_stride(cap)

    def gather(ids_ref, x_scr, g_scr):
        for i in range(cap):
            t = ids_ref[0, 0, i]
            x_scr[pl.ds(i, SUBLANES, stride=stride), :] = (
                src_ref[0, pl.ds(pl.multiple_of(t * SUBLANES, SUBLANES), SUBLANES), :])
            g_scr[pl.ds(i, 1), :] = affr_ref[0, pl.ds(t, 1), :]

    def ffn(x_scr, g_scr, y_scr):
        xg = jnp.concatenate([x_scr[j * stride:j * stride + cap, :].astype(bf16) for j in range(SUBLANES)], axis=1)
        lane = lax.broadcasted_iota(jnp.int32, (1, n_experts), 1)
        gate = jnp.sum(jnp.where(lane == e, g_scr[...], 0.0), axis=-1, keepdims=True)
        g = _dot(xg, wg_ref[0])
        u = _dot(xg, wu_ref[0])
        hmid = (_silu(g) * u).astype(bf16)
        y = _dot(hmid, wd_ref[0]) * gate
        for j in range(y.shape[1] // LANES):
            y_scr[j * stride:j * stride + cap, :] = y[:, j * LANES:(j + 1) * LANES]

    def scatter(ids_ref, y_scr):
        for i0 in range(0, cap, SCATTER_UNROLL):
            rows = [pl.multiple_of(ids_ref[0, 0, i0 + u] * SUBLANES, SUBLANES) for u in range(SCATTER_UNROLL)]
            vals = [acc_ref[0, pl.ds(rows[u], SUBLANES), :] + y_scr[pl.ds(i0 + u, SUBLANES, stride=stride), :]
                    for u in range(SCATTER_UNROLL)]
            for u in range(SCATTER_UNROLL):
                acc_ref[0, pl.ds(rows[u], SUBLANES), :] = vals[u]

    @pl.when(e == 0)
    def _():
        acc_ref[...] = jnp.zeros_like(acc_ref)
        y_odd[...] = jnp.zeros_like(y_odd)
        gather(idx_ref, x_even, g_even)

    @pl.when(e % 2 == 0)
    def _():
        gather(idx_next_ref, x_odd, g_odd)
        ffn(x_even, g_even, y_even)
        scatter(idx_prev_ref, y_odd)

    @pl.when(e % 2 == 1)
    def _():
        gather(idx_next_ref, x_even, g_even)
        ffn(x_odd, g_odd, y_odd)
        scatter(idx_prev_ref, y_even)

    @pl.when(e == n_experts - 1)
    def _():
        scatter(idx_ref, y_odd if (n_experts - 1) % 2 else y_even)


def _expert_mixture(idx, h2_slab, aff_rows, wg, wu, wd):
    B, E, cap = idx.shape
    D, FF = wg.shape[1], wg.shape[2]
    rows = (D // LANES) * _chunk_stride(cap)
    wmap = lambda b, e: (e, 0, 0)
    bmap = lambda b, e: (b, 0, 0)
    ids = idx.reshape(B * E, 1, cap)
    ids_spec = lambda m: pl.BlockSpec((1, 1, cap), m, memory_space=pltpu.SMEM)
    return pl.pallas_call(
        functools.partial(_expert_kernel, cap=cap, n_experts=E),
        grid=(B, E),
        in_specs=[ids_spec(lambda b, e: (b * E + jnp.maximum(e - 1, 0), 0, 0)),
                  ids_spec(lambda b, e: (b * E + e, 0, 0)),
                  ids_spec(lambda b, e: (b * E + jnp.minimum(e + 1, E - 1), 0, 0)),
                  pl.BlockSpec((1,) + h2_slab.shape[1:], bmap, pipeline_mode=pl.Buffered(1)),
                  pl.BlockSpec((1,) + aff_rows.shape[1:], bmap, pipeline_mode=pl.Buffered(1)),
                  pl.BlockSpec((1, D, FF), wmap),
                  pl.BlockSpec((1, D, FF), wmap),
                  pl.BlockSpec((1, FF, D), wmap)],
        out_specs=pl.BlockSpec((1,) + h2_slab.shape[1:], bmap, pipeline_mode=pl.Buffered(1)),
        out_shape=jax.ShapeDtypeStruct(h2_slab.shape, f32),
        scratch_shapes=[pltpu.VMEM((rows, LANES), f32), pltpu.VMEM((rows, LANES), f32),
                        pltpu.VMEM((cap, E), f32), pltpu.VMEM((cap, E), f32),
                        pltpu.VMEM((rows, LANES), f32), pltpu.VMEM((rows, LANES), f32)],
        compiler_params=pltpu.CompilerParams(
            dimension_semantics=("parallel", "arbitrary"), vmem_limit_bytes=EXPERT_VMEM_LIMIT_BYTES),
        name="expert_mixture",
    )(ids, ids, ids, h2_slab, aff_rows, wg, wu, wd)


def _final_kernel(x1_ref, g2_ref, acc_ref, o_ref, *, tm):
    ffn = jnp.concatenate([acc_ref[0, pl.ds(j, tm, stride=SUBLANES), :] for j in range(SUBLANES)], axis=-1)
    o_ref[0] = x1_ref[0] + g2_ref[0] * ffn


def _final_residual(x1, g2, acc, *, tm):
    B, N, D = x1.shape
    row_map = lambda b, i: (b, i, 0)
    return pl.pallas_call(
        functools.partial(_final_kernel, tm=tm),
        grid=(B, N // tm),
        in_specs=[pl.BlockSpec((1, tm, D), row_map),
                  pl.BlockSpec((1, 1, D), lambda b, i: (b, 0, 0)),
                  pl.BlockSpec((1, tm * SUBLANES, LANES), row_map)],
        out_specs=pl.BlockSpec((1, tm, D), row_map),
        out_shape=jax.ShapeDtypeStruct((B, N, D), f32),
        compiler_params=pltpu.CompilerParams(dimension_semantics=("parallel", "parallel")),
        name="final_residual",
    )(x1, g2, acc)


def _rope_tables(n):
    rows = n // GRID_W
    row, col = jnp.meshgrid(jnp.arange(rows), jnp.arange(GRID_W), indexing='ij')
    n_freq = HEAD_DIM // 4
    freqs = ROPE_BASE ** (-jnp.arange(n_freq, dtype=f32) / n_freq)
    ang = jnp.concatenate([row.reshape(-1, 1).astype(f32) * freqs, col.reshape(-1, 1).astype(f32) * freqs], -1)
    cos, sin = jnp.cos(ang), jnp.sin(ang)
    reps = LANES // HEAD_DIM
    return (jnp.tile(jnp.concatenate([cos, cos], -1), (1, reps)),
            jnp.tile(jnp.concatenate([-sin, sin], -1), (1, reps)))


def _pack_w_in(w_in):
    D = w_in.shape[0]
    o = 0
    aq = w_in[:, o:o + ATT_WIDTH]; o += ATT_WIDTH
    ak = w_in[:, o:o + KV_WIDTH]; o += KV_WIDTH
    av = w_in[:, o:o + KV_WIDTH]; o += KV_WIDTH
    mq = w_in[:, o:o + M_WIDTH]; o += M_WIDTH
    mk = w_in[:, o:o + M_WIDTH]; o += M_WIDTH
    mv = w_in[:, o:o + M_WIDTH]; o += M_WIDTH
    mo = w_in[:, o:o + M_WIDTH]; o += M_WIDTH
    gates = _gate_order(w_in[:, o:o + N_GATES])

    def dup(a):
        a = a.reshape(D, N_KV_HEADS, 1, HEAD_DIM)
        return jnp.broadcast_to(a, (D, N_KV_HEADS, 2, HEAD_DIM)).reshape(D, K2_WIDTH)

    pad = jnp.zeros((D, LANES - N_GATES), w_in.dtype)
    return jnp.concatenate([aq, dup(ak), av, mq, mk, mv, mo, gates, pad], -1).astype(bf16)


def _gate_order(g):
    H = M_HEADS
    return jnp.concatenate([g[..., 0:H], g[..., 2 * H:3 * H], g[..., H:2 * H], g[..., 3 * H:4 * H]], -1)


def _layer(x, ctx, mod_x, mod_c, norm1_w, norm2_w, w_in, b_gates, conv_qk, q_norm_w, k_norm_w, sink,
           mlstm_norm_w, w_out, w_router, w_gate, w_up, w_down):
    B, N, D = x.shape
    L = ctx.shape[1]
    assert D == SUBLANES * LANES, "token rows are handled as one (8, 128) register tile"
    tm = min(ROW_TILE, N)
    sh1, sc1, g1, sh2, sc2, g2 = [m[:, None, :] for m in jnp.split(mod_x, 6, -1)]
    csh1, csc1 = [jnp.broadcast_to(m[None, None, :], (B, 1, D)) for m in jnp.split(mod_c, 6, -1)[:2]]

    w_all = _pack_w_in(w_in)
    qk_w = jnp.concatenate([jnp.tile(q_norm_w, N_HEADS), jnp.tile(k_norm_w, 2 * N_KV_HEADS)])[None, :]
    bg = jnp.concatenate([_gate_order(b_gates), jnp.zeros((LANES - N_GATES,), f32)])[None, :]
    nw1 = norm1_w[None, :]
    cos_t, sin_t = _rope_tables(N)
    ones_t, zeros_t = jnp.ones((L, LANES), f32), jnp.zeros((L, LANES), f32)

    aq, k2, vt, mq, _, mkt, mv, mo, _, li_rc, lf_rc = _in_projection(x, sh1, sc1, nw1, w_all, cos_t, sin_t, qk_w,
                                                                     conv_qk, bg, tm=tm)
    _, ck2, cvt, _, cmk, _, cmv, _, cgts, _, _ = _in_projection(ctx, csh1, csc1, nw1, w_all, ones_t, zeros_t, qk_w,
                                                                conv_qk, bg, tm=L)

    att = _attention(sink, aq, k2, vt, ck2, cvt)

    n_streams = 2 * M_HEADS
    cn0, m0 = _ctx_states(cmk, cmv, cgts[..., :n_streams], cgts[..., n_streams:])
    m0 = jnp.broadcast_to(jnp.swapaxes(m0, 1, 2), (B, n_streams, CHUNK))
    gstats = _gate_prep(li_rc, lf_rc)
    hf, hb = _mlstm(mq, mkt, mv, gstats, cn0, m0)

    wr_pad = jnp.pad(w_router, ((0, 0), (0, LANES - N_EXPERTS)))
    wr_hi = wr_pad.astype(bf16)
    wr_split = jnp.concatenate([wr_hi, (wr_pad - wr_hi.astype(f32)).astype(bf16)], axis=1)
    x1, h2_slab, aff_t, aff_rows = _out_projection(x, att, hf, hb, mo, mlstm_norm_w[None, :], w_out.astype(bf16),
                                                   g1, norm2_w[None, :], sh2, sc2, wr_split, N_EXPERTS, tm=tm)

    cap = CAPACITY * N // N_EXPERTS
    idx = _route(aff_t, cap).reshape(B, N_EXPERTS, cap)
    acc = _expert_mixture(idx, h2_slab, aff_rows, w_gate.astype(bf16), w_up.astype(bf16), w_down.astype(bf16))
    return _final_residual(x1, g2, acc, tm=tm)


def kernel(x, c, ctx, c_ctx, w_mod, b_mod, norm1_w, norm2_w, w_in, b_gates, conv_qk, q_norm_w, k_norm_w, sink,
           mlstm_norm_w, w_out, w_router, w_gate, w_up, w_down):
    depth = w_mod.shape[0]
    assert depth == 1, "only the final-layer (no context update) form of the block is implemented"
    B = x.shape[0]
    pad_rows = (-(B + 1)) % SUBLANES
    c_all = jnp.concatenate([c, c_ctx[None, :], jnp.zeros((pad_rows, c.shape[1]), c.dtype)], 0)
    mod = _modulation(c_all, w_mod[0], b_mod[0])
    return _layer(x, ctx, mod[:B], mod[B], norm1_w[0], norm2_w[0], w_in[0], b_gates[0], conv_qk[0], q_norm_w[0],
                  k_norm_w[0], sink[0], mlstm_norm_w[0], w_out[0], w_router[0], w_gate[0], w_up[0], w_down[0])
```

```python
import functools

import jax
import jax.numpy as jnp
from jax import lax
from jax.experimental import pallas as pl
from jax.experimental.pallas import tpu as pltpu

f32 = jnp.float32
bf16 = jnp.bfloat16

GRID_W = 64
N_HEADS = 8
N_KV_HEADS = 2
HEAD_DIM = 64
WINDOW = 128
BLOCK = 128
ROPE_BASE = 10000.0
M_HEADS = 4
M_DIM = 128
CHUNK = 128
CONV_W = 3
ATT_WIDTH = N_HEADS * HEAD_DIM
KV_WIDTH = N_KV_HEADS * HEAD_DIM
M_WIDTH = M_HEADS * M_DIM
N_EXPERTS = 16
CAPACITY = 2
EPS = 1e-6
NEG = -1e30
LOG2E = 1.4426950408889634
Q_SCALE = HEAD_DIM ** -0.5 * LOG2E

LANES = 128
SUBLANES = 8
VMEM_LIMIT_BYTES = 56 * 1024 * 1024
ROW_TILE = 1024
MOD_COL_TILE = 512

K2_WIDTH = 2 * KV_WIDTH
C_Q = 0
C_K = C_Q + ATT_WIDTH
C_V = C_K + K2_WIDTH
C_MQK = C_V + KV_WIDTH
C_MV = C_MQK + 2 * M_WIDTH
C_MO = C_MV + M_WIDTH
C_G = C_MO + M_WIDTH
W_COLS = C_G + LANES
N_GATES = 4 * M_HEADS


def _dot(a, b):
    return jnp.dot(a, b, preferred_element_type=f32)


def _dot_nt(a, b):
    return lax.dot_general(a, b, (((1,), (1,)), ((), ())), preferred_element_type=f32)


def _dot_tn(a, b):
    return lax.dot_general(a, b, (((0,), (0,)), ((), ())), preferred_element_type=f32)


def _ones_cols(v):
    return jnp.concatenate([v, jnp.ones((v.shape[0], LANES), f32).astype(bf16)], axis=1)


def _split3(x):
    h = x.astype(bf16)
    r = x - h.astype(f32)
    m = r.astype(bf16)
    l = (r - m.astype(f32)).astype(bf16)
    return h, m, l


def _log_sigmoid(x):
    return jnp.minimum(x, 0.0) - jnp.log1p(jnp.exp(-jnp.abs(x)))


def _silu(x):
    return x / (1.0 + jnp.exp(-x))


def _mod_kernel(c_ref, w_ref, b_ref, o_ref):
    h, m, l = _split3(_silu(c_ref[...]))
    w = w_ref[...]
    wh = w.astype(bf16)
    wl = (w - wh.astype(f32)).astype(bf16)
    acc = _dot(h, wh) + _dot(m, wh) + _dot(h, wl) + _dot(l, wh) + _dot(m, wl)
    o_ref[...] = acc + b_ref[...]


def _modulation(c_all, w_mod, b_mod):
    rows, d = c_all.shape
    cols = w_mod.shape[1]
    tn = MOD_COL_TILE
    return pl.pallas_call(
        _mod_kernel,
        grid=(cols // tn,),
        in_specs=[pl.BlockSpec((rows, d), lambda j: (0, 0)),
                  pl.BlockSpec((d, tn), lambda j: (0, j)),
                  pl.BlockSpec((1, tn), lambda j: (0, j))],
        out_specs=pl.BlockSpec((rows, tn), lambda j: (0, j)),
        out_shape=jax.ShapeDtypeStruct((rows, cols), f32),
        name="adaln_mod",
    )(c_all, w_mod, b_mod.reshape(1, cols))


def _inproj_kernel(xp_ref, x_ref, xn_ref, shift_ref, scale_ref, nw_ref, w_ref, wvt_ref, cos_ref, sin_ref, qkw_ref,
                   conv_ref, bg_ref,
                   aq_ref, k2_ref, vt_ref, mq_ref, mk_ref, mkt_ref, mv_ref, mo_ref, g_ref, li_ref, lf_ref,
                   conv_scr, *, tm, nt):
    i = pl.program_id(1)
    nw = nw_ref[...]
    sc = 1.0 + scale_ref[0]
    sh = shift_ref[0]

    def prep(xv):
        ms = jnp.mean(xv * xv, axis=-1, keepdims=True)
        return (xv * lax.rsqrt(ms + EPS) * nw) * sc + sh

    hm = prep(x_ref[0])
    lhs = hm.astype(bf16)
    lhs_halo = jnp.concatenate([prep(xp_ref[0]), hm, prep(xn_ref[0])], axis=0).astype(bf16)

    lane = lax.broadcasted_iota(jnp.int32, (1, LANES), 1)
    lo = lane < HEAD_DIM
    first_half = (lane % HEAD_DIM) < (HEAD_DIM // 2)
    cos = cos_ref[...]
    sin = sin_ref[...]
    n_qk = (ATT_WIDTH + K2_WIDTH) // LANES
    vqk = _dot(lhs, w_ref[:, C_Q:C_Q + ATT_WIDTH + K2_WIDTH])
    for grp_i in range(n_qk):
        c0 = grp_i * LANES
        v = vqk[:, c0:c0 + LANES]
        sq = v * v
        s_all = jnp.sum(sq, axis=-1, keepdims=True)
        s_lo = jnp.sum(jnp.where(lo, sq, 0.0), axis=-1, keepdims=True)
        ms = jnp.where(lo, s_lo, s_all - s_lo) * (1.0 / HEAD_DIM)
        nv = v * lax.rsqrt(ms + EPS) * qkw_ref[:, c0:c0 + LANES]
        swapped = jnp.where(first_half, pltpu.roll(nv, LANES - HEAD_DIM // 2, 1), pltpu.roll(nv, HEAD_DIM // 2, 1))
        r = nv * cos + swapped * sin
        if c0 < ATT_WIDTH:
            aq_ref[0, :, c0:c0 + LANES] = (r * Q_SCALE).astype(bf16)
        else:
            k2_ref[0, :, c0 - ATT_WIDTH:c0 - ATT_WIDTH + LANES] = r.astype(bf16)

    vt_ref[0] = _dot_nt(wvt_ref[...], lhs).astype(bf16)
    mv_ref[0] = _dot(lhs, w_ref[:, C_MV:C_MV + M_WIDTH]).astype(bf16)
    mo_ref[0] = _dot(lhs, w_ref[:, C_MO:C_MO + M_WIDTH]).astype(bf16)

    gates = _dot(lhs, w_ref[:, C_G:C_G + LANES]) + bg_ref[...]
    n_streams = 2 * M_HEADS
    is_forget = (lane >= n_streams) & (lane < N_GATES)
    gates = jnp.where(is_forget, _log_sigmoid(gates), gates)
    g_ref[0] = gates[:, :N_GATES]
    for c in range(tm // CHUNK):
        t_c = gates[c * CHUNK:(c + 1) * CHUNK, :].T
        li_ref[0, c * n_streams:(c + 1) * n_streams, :] = t_c[0:n_streams]
        lf_ref[0, c * n_streams:(c + 1) * n_streams, :] = t_c[n_streams:2 * n_streams]

    conv_scr[...] = _dot(lhs_halo, w_ref[:, C_MQK:C_MQK + 2 * M_WIDTH])
    row = lax.broadcasted_iota(jnp.int32, (tm, 1), 0)
    prev = conv_scr[SUBLANES - 1:SUBLANES - 1 + tm, :]
    prev = jnp.where((row == 0) & (i == 0), 0.0, prev)
    nxt = conv_scr[SUBLANES + 1:SUBLANES + 1 + tm, :]
    nxt = jnp.where((row == tm - 1) & (i == nt - 1), 0.0, nxt)
    cur = conv_scr[SUBLANES:SUBLANES + tm, :]
    u = prev * conv_ref[0:1, :] + cur * conv_ref[1:2, :] + nxt * conv_ref[2:3, :]
    u = _silu(u)
    mq_ref[0] = (u[:, :M_WIDTH] * (M_DIM ** -0.5)).astype(bf16)
    mk = u[:, M_WIDTH:]
    mk_ref[0] = mk.astype(bf16)
    mkt_ref[0] = mk.T.astype(bf16)


def _in_projection(x, shift, scale, norm_w, w_all, cos_t, sin_t, qk_w, conv_qk, bg, *, tm):
    B, N, D = x.shape
    nt = N // tm
    hb = tm // SUBLANES
    nblk8 = N // SUBLANES
    kern = functools.partial(_inproj_kernel, tm=tm, nt=nt)
    gate_rows = (tm // CHUNK) * 2 * M_HEADS
    row_map = lambda b, i: (b, i, 0)
    const2 = lambda b, i: (0, 0)
    outs = pl.pallas_call(
        kern,
        grid=(B, nt),
        in_specs=[
            pl.BlockSpec((1, SUBLANES, D), lambda b, i: (b, jnp.maximum(i * hb - 1, 0), 0)),
            pl.BlockSpec((1, tm, D), row_map),
            pl.BlockSpec((1, SUBLANES, D), lambda b, i: (b, jnp.minimum((i + 1) * hb, nblk8 - 1), 0)),
            pl.BlockSpec((1, 1, D), lambda b, i: (b, 0, 0)),
            pl.BlockSpec((1, 1, D), lambda b, i: (b, 0, 0)),
            pl.BlockSpec((1, D), const2),
            pl.BlockSpec((D, W_COLS), const2),
            pl.BlockSpec((KV_WIDTH, D), const2),
            pl.BlockSpec((tm, LANES), lambda b, i: (i, 0)),
            pl.BlockSpec((tm, LANES), lambda b, i: (i, 0)),
            pl.BlockSpec((1, ATT_WIDTH + K2_WIDTH), const2),
            pl.BlockSpec((CONV_W, 2 * M_WIDTH), const2),
            pl.BlockSpec((1, LANES), const2),
        ],
        out_specs=[
            pl.BlockSpec((1, tm, ATT_WIDTH), row_map),
            pl.BlockSpec((1, tm, K2_WIDTH), row_map),
            pl.BlockSpec((1, KV_WIDTH, tm), lambda b, i: (b, 0, i)),
            pl.BlockSpec((1, tm, M_WIDTH), row_map),
            pl.BlockSpec((1, tm, M_WIDTH), row_map),
            pl.BlockSpec((1, M_WIDTH, tm), lambda b, i: (b, 0, i)),
            pl.BlockSpec((1, tm, M_WIDTH), row_map),
            pl.BlockSpec((1, tm, M_WIDTH), row_map),
            pl.BlockSpec((1, tm, N_GATES), row_map),
            pl.BlockSpec((1, gate_rows, CHUNK), row_map),
            pl.BlockSpec((1, gate_rows, CHUNK), row_map),
        ],
        out_shape=[
            jax.ShapeDtypeStruct((B, N, ATT_WIDTH), bf16),
            jax.ShapeDtypeStruct((B, N, K2_WIDTH), bf16),
            jax.ShapeDtypeStruct((B, KV_WIDTH, N), bf16),
            jax.ShapeDtypeStruct((B, N, M_WIDTH), bf16),
            jax.ShapeDtypeStruct((B, N, M_WIDTH), bf16),
            jax.ShapeDtypeStruct((B, M_WIDTH, N), bf16),
            jax.ShapeDtypeStruct((B, N, M_WIDTH), bf16),
            jax.ShapeDtypeStruct((B, N, M_WIDTH), bf16),
            jax.ShapeDtypeStruct((B, N, N_GATES), f32),
            jax.ShapeDtypeStruct((B, nt * gate_rows, CHUNK), f32),
            jax.ShapeDtypeStruct((B, nt * gate_rows, CHUNK), f32),
        ],
        scratch_shapes=[pltpu.VMEM((tm + 2 * SUBLANES, 2 * M_WIDTH), f32)],
        compiler_params=pltpu.CompilerParams(
            dimension_semantics=("parallel", "parallel"), vmem_limit_bytes=VMEM_LIMIT_BYTES),
        name="in_projection",
    )(x, x, x, shift, scale, norm_w, w_all, w_all[:, C_V:C_V + KV_WIDTH].T, cos_t, sin_t, qk_w, conv_qk, bg)
    return outs


def _attn_kernel(sink_ref, q_ref, kp_ref, kc_ref, kn_ref, vp_ref, vc_ref, vn_ref, ck_ref, cv_ref, o_ref, *, nb):
    n = pl.program_id(1)
    span = BLOCK + 2 * WINDOW
    grp = N_HEADS // N_KV_HEADS
    lane = lax.broadcasted_iota(jnp.int32, (1, LANES), 1)
    lo = lane < HEAD_DIM
    kj = lax.broadcasted_iota(jnp.int32, (span, BLOCK), 0)
    qi = lax.broadcasted_iota(jnp.int32, (span, BLOCK), 1)
    rel = kj - qi
    valid = (rel >= 0) & (rel <= 2 * WINDOW)
    valid = valid & ((kj >= WINDOW) | (n > 0)) & ((kj < WINDOW + BLOCK) | (n < nb - 1))
    bias = jnp.where(valid, 0.0, NEG)
    bias4 = jnp.concatenate([bias] * grp, axis=1)
    col = lax.broadcasted_iota(jnp.int32, (1, grp * BLOCK), 1)
    n_ctx = ck_ref.shape[1]
    ones_rows = jnp.ones((HEAD_DIM, span + n_ctx), f32).astype(bf16)
    zero = jnp.zeros((), bf16)
    scores = []
    for kh in range(N_KV_HEADS):
        ks = slice(kh * LANES, (kh + 1) * LANES)
        keys = jnp.concatenate([kp_ref[0, :, ks], kc_ref[0, :, ks], kn_ref[0, :, ks], ck_ref[0, :, ks]], axis=0)
        q0 = kh * grp * HEAD_DIM
        q2 = jnp.concatenate([q_ref[0, :, q0:q0 + LANES], q_ref[0, :, q0 + LANES:q0 + 2 * LANES]], axis=0)
        q4 = jnp.concatenate([jnp.where(lo, q2, zero), jnp.where(lo, zero, q2)], axis=0)
        scores.append(_dot_nt(keys, q4))
    probs = []
    for kh in range(N_KV_HEADS):
        heads = (0, 2, 1, 3)
        sk = sink_ref[kh * grp + heads[0]]
        for c, g in enumerate(heads[1:], start=1):
            sk = jnp.where(col >= c * BLOCK, sink_ref[kh * grp + g], sk)
        sk = sk * LOG2E
        s_t = scores[kh]
        s_loc = s_t[:span] + bias4
        s_ctx = s_t[span:]
        m = jnp.maximum(jnp.maximum(jnp.max(s_loc, axis=0, keepdims=True),
                                    jnp.max(s_ctx, axis=0, keepdims=True)), sk)
        p_t = jnp.concatenate([jnp.exp2(s_loc - m), jnp.exp2(s_ctx - m)], axis=0).astype(bf16)
        probs.append((p_t, jnp.exp2(sk - m)))
    for kh in range(N_KV_HEADS):
        hs = slice(kh * HEAD_DIM, (kh + 1) * HEAD_DIM)
        vals_t = jnp.concatenate([vp_ref[0, hs, :], vc_ref[0, hs, :], vn_ref[0, hs, :], cv_ref[0, hs, :]], axis=1)
        vals_t = jnp.concatenate([vals_t, ones_rows], axis=0)
        q0 = kh * grp * HEAD_DIM
        p_t, p_sink = probs[kh]
        o_t = _dot(vals_t, p_t)
        out_t = o_t[:HEAD_DIM] / (o_t[HEAD_DIM:] + p_sink)
        for pair in range(2):
            even = out_t[:, pair * BLOCK:(pair + 1) * BLOCK]
            odd = out_t[:, (2 + pair) * BLOCK:(3 + pair) * BLOCK]
            both = jnp.concatenate([even, odd], axis=0)
            o_ref[0, :, q0 + pair * LANES:q0 + (pair + 1) * LANES] = both.T.astype(o_ref.dtype)


def _attention(sink, q, k2, vt, ck2, cvt):
    B, N, _ = q.shape
    L = ck2.shape[1]
    nb = N // BLOCK
    kern = functools.partial(_attn_kernel, nb=nb)
    prev_blk = lambda n: jnp.maximum(n - 1, 0)
    next_blk = lambda n: jnp.minimum(n + 1, nb - 1)
    same = lambda n: n
    cur_map = lambda b, n, s: (b, n, 0)
    ctx_map = lambda b, n, s: (b, 0, 0)
    k_spec = lambda f: pl.BlockSpec((1, BLOCK, K2_WIDTH), lambda b, n, s: (b, f(n), 0))
    v_spec = lambda f: pl.BlockSpec((1, KV_WIDTH, BLOCK), lambda b, n, s: (b, 0, f(n)))
    return pl.pallas_call(
        kern,
        grid_spec=pltpu.PrefetchScalarGridSpec(
            num_scalar_prefetch=1,
            grid=(B, nb),
            in_specs=[pl.BlockSpec((1, BLOCK, ATT_WIDTH), cur_map),
                      k_spec(prev_blk), k_spec(same), k_spec(next_blk),
                      v_spec(prev_blk), v_spec(same), v_spec(next_blk),
                      pl.BlockSpec((1, L, K2_WIDTH), ctx_map),
                      pl.BlockSpec((1, KV_WIDTH, L), ctx_map)],
            out_specs=pl.BlockSpec((1, BLOCK, ATT_WIDTH), cur_map),
        ),
        out_shape=jax.ShapeDtypeStruct((B, N, ATT_WIDTH), bf16),
        compiler_params=pltpu.CompilerParams(dimension_semantics=("parallel", "parallel")),
        name="window_attention",
    )(sink, q, k2, k2, k2, vt, vt, vt, ck2, cvt)


def _tri_cumsum_cols(tri, x):
    h, m, l = _split3(x)
    return _dot(tri, h) + _dot(tri, m) + _dot(tri, l)


def _tri_cumsum_rows(x, tri_t):
    h, m, l = _split3(x)
    return _dot(h, tri_t) + _dot(m, tri_t) + _dot(l, tri_t)


def _tri(t):
    r = lax.broadcasted_iota(jnp.int32, (t, t), 0)
    c = lax.broadcasted_iota(jnp.int32, (t, t), 1)
    return jnp.where(c <= r, 1.0, 0.0).astype(bf16), jnp.where(r <= c, 1.0, 0.0).astype(bf16)


def _ctx_state_kernel(k_ref, v_ref, li_ref, lf_ref, c_ref, m_ref):
    L = k_ref.shape[1]
    tri, _ = _tri(L)
    li = li_ref[0]
    lf = lf_ref[0]
    lane = lax.broadcasted_iota(jnp.int32, (1, 2 * M_HEADS), 1)
    fwd = lane < M_HEADS
    cs = _tri_cumsum_cols(tri, lf)
    tot = cs[L - 1:L, :]
    b = jnp.where(fwd, cs, tot - cs + lf)
    w = tot - b + li
    m_new = jnp.maximum(tot, jnp.max(w, axis=0, keepdims=True))
    ws = jnp.exp(w - m_new)
    m_ref[0] = m_new
    for c in range(2 * M_HEADS):
        hs = slice((c % M_HEADS) * M_DIM, (c % M_HEADS + 1) * M_DIM)
        ks = k_ref[0, :, hs].astype(f32) * ws[:, c:c + 1]
        c_ref[0, c] = _dot_tn(ks.astype(bf16), _ones_cols(v_ref[0, :, hs]))


def _ctx_states(cmk, cmv, li_c, lf_c):
    B, L, _ = cmk.shape
    S = 2 * M_HEADS
    bmap = lambda b: (b, 0, 0)
    return pl.pallas_call(
        _ctx_state_kernel,
        grid=(B,),
        in_specs=[pl.BlockSpec((1, L, M_WIDTH), bmap), pl.BlockSpec((1, L, M_WIDTH), bmap),
                  pl.BlockSpec((1, L, S), bmap), pl.BlockSpec((1, L, S), bmap)],
        out_specs=[pl.BlockSpec((1, S, M_DIM, M_DIM + LANES), lambda b: (b, 0, 0, 0)),
                   pl.BlockSpec((1, 1, S), bmap)],
        out_shape=[jax.ShapeDtypeStruct((B, S, M_DIM, M_DIM + LANES), f32),
                   jax.ShapeDtypeStruct((B, 1, S), f32)],
        compiler_params=pltpu.CompilerParams(dimension_semantics=("parallel",)),
        name="mlstm_ctx_state",
    )(cmk, cmv, li_c, lf_c)


N_GATE_STATS = 6


def _gate_prep_kernel(li_ref, lf_ref, o_ref):
    S = 2 * M_HEADS
    T = CHUNK
    li = li_ref[0]
    lf = lf_ref[0]
    rows = li.shape[0]
    _, tri_t = _tri(T)
    fwd = (lax.broadcasted_iota(jnp.int32, (rows, 1), 0) % S) < M_HEADS
    lane = lax.broadcasted_iota(jnp.int32, (1, T), 1)
    cs = _tri_cumsum_rows(lf, tri_t)
    tot = cs[:, T - 1:T]
    b = jnp.where(fwd, cs, tot - cs + lf)
    e = li - b
    pm = e
    shift = 1
    while shift < T:
        from_left = jnp.where(lane >= shift, pltpu.roll(pm, shift, 1), -jnp.inf)
        from_right = jnp.where(lane < T - shift, pltpu.roll(pm, T - shift, 1), -jnp.inf)
        pm = jnp.maximum(pm, jnp.where(fwd, from_left, from_right))
        shift *= 2
    w = tot - b + li
    o_ref[0, 0] = e
    o_ref[0, 1] = pm
    o_ref[0, 2] = b
    o_ref[0, 3] = w
    o_ref[0, 4] = jnp.broadcast_to(tot, (rows, T))
    o_ref[0, 5] = jnp.broadcast_to(jnp.max(w, axis=-1, keepdims=True), (rows, T))


def _gate_prep(li_rc, lf_rc):
    B, rows, T = li_rc.shape
    bmap = lambda b: (b, 0, 0)
    return pl.pallas_call(
        _gate_prep_kernel,
        grid=(B,),
        in_specs=[pl.BlockSpec((1, rows, T), bmap), pl.BlockSpec((1, rows, T), bmap)],
        out_specs=pl.BlockSpec((1, N_GATE_STATS, rows, T), lambda b: (b, 0, 0, 0)),
        out_shape=jax.ShapeDtypeStruct((B, N_GATE_STATS, rows, T), f32),
        compiler_params=pltpu.CompilerParams(dimension_semantics=("parallel",)),
        name="mlstm_gate_prep",
    )(li_rc, lf_rc)


def _mlstm_kernel(qf_ref, ktf_ref, vf_ref, qb_ref, ktb_ref, vb_ref, gf_ref, gb_ref,
                  cn0_ref, m0_ref, hf_ref, hb_ref, cn_scr, m_scr):
    j = pl.program_id(1)
    S = 2 * M_HEADS
    T = CHUNK

    @pl.when(j == 0)
    def _():
        cn_scr[...] = cn0_ref[0]
        m_scr[...] = m0_ref[0]

    fwd_r = lax.broadcasted_iota(jnp.int32, (S, 1), 0) < M_HEADS
    stat = lambda k: jnp.where(fwd_r, gf_ref[0, k], gb_ref[0, k])
    e_r, pm, b_r, w_r, tot, w_max = [stat(k) for k in range(N_GATE_STATS)]

    m_prev = m_scr[...]
    g_hi = jnp.maximum(m_prev, pm).astype(bf16)
    g_used = g_hi.astype(f32)
    en_r = jnp.exp(-(b_r + g_used))
    en_hi = en_r.astype(bf16)
    en_lo = (en_r - en_hi.astype(f32)).astype(bf16)
    m_new = jnp.maximum(tot + m_prev, w_max)
    a_r = jnp.exp(tot + m_prev - m_new)
    ws_r = jnp.exp(w_r - m_new)
    m_scr[...] = m_new

    ti = lax.broadcasted_iota(jnp.int32, (T, T), 0)
    si = lax.broadcasted_iota(jnp.int32, (T, T), 1)
    eye = ti == si
    en_hi = en_hi.astype(f32)
    en_lo = en_lo.astype(f32)
    ones_tt = jnp.ones((T, T), f32).astype(bf16)
    zeros_tt = jnp.zeros((T, T), f32).astype(bf16)
    ones_2t = jnp.ones((2 * T, T), f32).astype(bf16)

    def refs(c):
        is_fwd = c < M_HEADS
        hs = slice((c % M_HEADS) * M_DIM, (c % M_HEADS + 1) * M_DIM)
        return ((qf_ref if is_fwd else qb_ref), (ktf_ref if is_fwd else ktb_ref), (vf_ref if is_fwd else vb_ref),
                (hf_ref if is_fwd else hb_ref), hs, is_fwd)

    stage1 = []
    for c in range(S):
        q_ref, kt_ref, _, _, hs, _ = refs(c)
        row = lambda a: a[c:c + 1, :]
        diag = lambda a: jnp.where(eye, row(a), 0.0).astype(bf16)
        q = q_ref[0, :, hs]
        kt = kt_ref[0, hs, :]
        qk_g = _dot(jnp.concatenate([q, diag(g_used)], axis=1),
                    jnp.concatenate([jnp.concatenate([kt, zeros_tt], axis=1),
                                     jnp.concatenate([zeros_tt, ones_tt], axis=1)], axis=0))
        en_rep = _dot(jnp.concatenate([diag(en_hi), diag(en_lo)], axis=1), ones_2t)
        stage1.append((qk_g, en_rep, q))
    stage2 = []
    for c in range(S):
        _, _, v_ref, _, hs, is_fwd = refs(c)
        row = lambda a: a[c:c + 1, :]
        qk_g, _, q = stage1[c]
        g_rep = qk_g[:, T:]
        within = (si <= ti) if is_fwd else (si >= ti)
        p_mat = jnp.exp(jnp.where(within, row(e_r) - g_rep, -jnp.inf))
        wts = p_mat * qk_g[:, :T]
        decay = jnp.exp(row(m_prev) - g_rep)
        lhs = jnp.concatenate([wts.astype(bf16), (decay * q.astype(f32)).astype(bf16)], axis=1)
        rhs = jnp.concatenate([_ones_cols(v_ref[0, :, hs]), cn_scr[c].astype(bf16)], axis=0)
        stage2.append(_dot(lhs, rhs))
    for c in range(S):
        _, _, _, h_ref, hs, _ = refs(c)
        _, en_rep, _ = stage1[c]
        nd = stage2[c]
        h = nd[:, :M_DIM] / jnp.maximum(jnp.abs(nd[:, M_DIM:]), en_rep)
        h_ref[0, :, hs] = h.astype(h_ref.dtype)
    for c in range(S):
        _, kt_ref, v_ref, _, hs, _ = refs(c)
        row = lambda a: a[c:c + 1, :]
        kst = (kt_ref[0, hs, :].astype(f32) * row(ws_r)).astype(bf16)
        a_c = row(a_r)
        cn_scr[c] = jnp.concatenate([a_c, a_c], axis=1) * cn_scr[c] + _dot(kst, _ones_cols(v_ref[0, :, hs]))


def _mlstm(mq, mkt, mv, gstats, cn0, m0):
    B, N, _ = mq.shape
    nc = N // CHUNK
    S = 2 * M_HEADS
    fmap = lambda b, j: (b, j, 0)
    bmap = lambda b, j: (b, nc - 1 - j, 0)
    frow = lambda b, j: (b, 0, j)
    brow = lambda b, j: (b, 0, nc - 1 - j)
    seq = lambda m: pl.BlockSpec((1, CHUNK, M_WIDTH), m)
    seq_t = lambda m: pl.BlockSpec((1, M_WIDTH, CHUNK), m)
    gspec = lambda m: pl.BlockSpec((1, N_GATE_STATS, S, CHUNK), m)
    return pl.pallas_call(
        _mlstm_kernel,
        grid=(B, nc),
        in_specs=[seq(fmap), seq_t(frow), seq(fmap), seq(bmap), seq_t(brow), seq(bmap),
                  gspec(lambda b, j: (b, 0, j, 0)), gspec(lambda b, j: (b, 0, nc - 1 - j, 0)),
                  pl.BlockSpec((1, S, M_DIM, M_DIM + LANES), lambda b, j: (b, 0, 0, 0)),
                  pl.BlockSpec((1, S, CHUNK), lambda b, j: (b, 0, 0))],
        out_specs=[seq(fmap), seq(bmap)],
        out_shape=[jax.ShapeDtypeStruct((B, N, M_WIDTH), bf16), jax.ShapeDtypeStruct((B, N, M_WIDTH), bf16)],
        scratch_shapes=[pltpu.VMEM((S, M_DIM, M_DIM + LANES), f32), pltpu.VMEM((S, CHUNK), f32)],
        compiler_params=pltpu.CompilerParams(dimension_semantics=("parallel", "arbitrary")),
        name="mlstm_scan",
    )(mq, mkt, mv, mq, mkt, mv, gstats, gstats, cn0, m0)


def _outproj_kernel(x_ref, att_ref, hf_ref, hb_ref, mo_ref, mnw_ref, wo_ref, g1_ref, n2w_ref, sh2_ref, sc2_ref,
                    wr_ref, x1_ref, h2_ref, aff_ref, affr_ref, *, tm):
    parts = []
    for h in range(M_HEADS):
        hs = slice(h * M_DIM, (h + 1) * M_DIM)
        s = hf_ref[0, :, hs].astype(f32) + hb_ref[0, :, hs].astype(f32)
        ms = jnp.mean(s * s, axis=-1, keepdims=True)
        hn = s * lax.rsqrt(ms + EPS) * mnw_ref[:, hs]
        gate = 1.0 / (1.0 + jnp.exp(-mo_ref[0, :, hs].astype(f32)))
        parts.append((hn * gate).astype(bf16))
    ml = jnp.concatenate(parts, axis=-1)
    proj = _dot(att_ref[0], wo_ref[0:ATT_WIDTH, :]) + _dot(ml, wo_ref[ATT_WIDTH:, :])
    x1 = x_ref[0] + g1_ref[0] * proj
    x1_ref[0] = x1
    ms = jnp.mean(x1 * x1, axis=-1, keepdims=True)
    h2 = (x1 * lax.rsqrt(ms + EPS) * n2w_ref[...]) * (1.0 + sc2_ref[0]) + sh2_ref[0]
    for j in range(SUBLANES):
        h2_ref[0, pl.ds(j, tm, stride=SUBLANES), :] = h2[:, j * LANES:(j + 1) * LANES]
    hh = h2.astype(bf16)
    hl = (h2 - hh.astype(f32)).astype(bf16)
    n_exp = aff_ref.shape[1]
    both = _dot(hh, wr_ref[...])
    logits_rows = both[:, :LANES] + both[:, LANES:] + _dot(hl, wr_ref[:, :LANES])
    logits = logits_rows.T[:n_exp]
    mx = jnp.max(logits, axis=0, keepdims=True)
    e = jnp.exp(logits - mx)
    aff = e / jnp.sum(e, axis=0, keepdims=True)
    aff_ref[0] = aff
    affr_ref[0] = aff.T


def _out_projection(x, att, hf, hb, mo, mnw, w_out, g1, n2w, sh2, sc2, wr_split, n_experts, *, tm):
    B, N, D = x.shape
    E = n_experts
    row_map = lambda b, i: (b, i, 0)
    const2 = lambda b, i: (0, 0)
    bvec = lambda b, i: (b, 0, 0)
    return pl.pallas_call(
        functools.partial(_outproj_kernel, tm=tm),
        grid=(B, N // tm),
        in_specs=[pl.BlockSpec((1, tm, D), row_map),
                  pl.BlockSpec((1, tm, ATT_WIDTH), row_map),
                  pl.BlockSpec((1, tm, M_WIDTH), row_map),
                  pl.BlockSpec((1, tm, M_WIDTH), row_map),
                  pl.BlockSpec((1, tm, M_WIDTH), row_map),
                  pl.BlockSpec((1, M_WIDTH), const2),
                  pl.BlockSpec((ATT_WIDTH + M_WIDTH, D), const2),
                  pl.BlockSpec((1, 1, D), bvec),
                  pl.BlockSpec((1, D), const2),
                  pl.BlockSpec((1, 1, D), bvec),
                  pl.BlockSpec((1, 1, D), bvec),
                  pl.BlockSpec((D, 2 * LANES), const2)],
        out_specs=[pl.BlockSpec((1, tm, D), row_map),
                   pl.BlockSpec((1, tm * SUBLANES, LANES), row_map),
                   pl.BlockSpec((1, E, tm), lambda b, i: (b, 0, i)),
                   pl.BlockSpec((1, tm, E), row_map)],
        out_shape=[jax.ShapeDtypeStruct((B, N, D), f32),
                   jax.ShapeDtypeStruct((B, N * SUBLANES, LANES), f32),
                   jax.ShapeDtypeStruct((B, E, N), f32),
                   jax.ShapeDtypeStruct((B, N, E), f32)],
        compiler_params=pltpu.CompilerParams(
            dimension_semantics=("parallel", "parallel"), vmem_limit_bytes=VMEM_LIMIT_BYTES),
        name="out_projection",
    )(x, att, hf, hb, mo, mnw, w_out, g1, n2w, sh2, sc2, wr_split)


def _chunk_stride(cap):
    return cap + SUBLANES


ROUTE_EXPONENT_BITS = (64, 32, 16, 8, 4, 2, 1)
ROUTE_REFINE_STEPS = 25
ROUTE_TINY = 1e-30


def _route_kernel(aff_ref, idx_ref, cl_scr, *, cap):
    E, N = aff_ref.shape[1], aff_ref.shape[2]
    aff = aff_ref[0]

    def count_above(t):
        return jnp.sum(jnp.where(aff > t, 1.0, 0.0), axis=-1, keepdims=True)

    hi = jnp.full((E, 1), 2.0, f32)
    for bit in ROUTE_EXPONENT_BITS:
        cand = hi * (2.0 ** -bit)
        hi = jnp.where(count_above(cand) < cap, cand, hi)
    lo = jnp.where(hi < ROUTE_TINY, -1.0, 0.5 * hi)
    for _ in range(ROUTE_REFINE_STEPS):
        q = 0.25 * (hi - lo)
        m1, m2, m3 = lo + q, lo + 2.0 * q, lo + 3.0 * q
        ok1, ok2, ok3 = [count_above(m) >= cap for m in (m1, m2, m3)]
        lo, hi = (jnp.where(ok3, m3, jnp.where(ok2, m2, jnp.where(ok1, m1, lo))),
                  jnp.where(ok3, hi, jnp.where(ok2, m3, jnp.where(ok1, m2, m1))))
    above = jnp.where(aff > hi, 1.0, 0.0)
    equal = jnp.where(aff > lo, 1.0, 0.0) - above
    need = cap - jnp.sum(above, axis=-1, keepdims=True)
    n_chunks = N // LANES
    assert n_chunks <= LANES
    _, tri_t = _tri(LANES)
    lane = lax.broadcasted_iota(jnp.int32, (1, LANES), 1)
    both = jnp.concatenate([above, equal], axis=0).astype(bf16)
    run = jnp.zeros((2 * E, 1), f32)
    start = jnp.zeros((E, 1), f32)
    ends = jnp.full((E, LANES), float(2 * N), f32)
    cl_scr[...] = jnp.zeros_like(cl_scr)
    for k in range(n_chunks):
        ck = _dot(both[:, k * LANES:(k + 1) * LANES], tri_t)
        count = (ck[:E] + run[:E]) + jnp.minimum(ck[E:] + run[E:], need)
        local = count - start
        for e in range(E):
            cl_scr[e, k:k + 1, :] = local[e:e + 1, :]
        start = count[:, LANES - 1:LANES]
        ends = jnp.where(lane == k, start, ends)
        run = run + ck[:, LANES - 1:LANES]

    slot = lax.broadcasted_iota(jnp.int32, (cap, 1), 0).astype(f32)
    for e in range(E):
        ends_e = ends[e:e + 1, :]
        before = ends_e <= slot
        chunk = jnp.sum(jnp.where(before, 1.0, 0.0), axis=-1, keepdims=True)
        chunk_start = jnp.max(jnp.where(before, ends_e, 0.0), axis=-1, keepdims=True)
        pick = jnp.where(lane.astype(f32) == chunk, 1.0, 0.0).astype(bf16)
        local = _dot(pick, cl_scr[e].astype(bf16))
        pos = jnp.sum(jnp.where(local <= slot - chunk_start, 1.0, 0.0), axis=-1, keepdims=True)
        idx_ref[0, e] = (chunk * float(LANES) + pos).astype(jnp.int32)


def _route(aff_t, cap):
    B, E, N = aff_t.shape
    return pl.pallas_call(
        functools.partial(_route_kernel, cap=cap),
        grid=(B,),
        in_specs=[pl.BlockSpec((1, E, N), lambda b: (b, 0, 0))],
        out_specs=pl.BlockSpec((1, E, cap, 1), lambda b: (b, 0, 0, 0)),
        out_shape=jax.ShapeDtypeStruct((B, E, cap, 1), jnp.int32),
        scratch_shapes=[pltpu.VMEM((E, LANES, LANES), f32)],
        compiler_params=pltpu.CompilerParams(dimension_semantics=("parallel",)),
        name="expert_choice_route",
    )(aff_t)


SCATTER_UNROLL = 8
EXPERT_VMEM_LIMIT_BYTES = 60 * 1024 * 1024


def _expert_kernel(idx_prev_ref, idx_ref, idx_next_ref, src_ref, affr_ref, wg_ref, wu_ref, wd_ref, acc_ref,
                   x_even, x_odd, g_even, g_odd, y_even, y_odd, *, cap, n_experts):
    e = pl.program_id(1)
    stride = _chunk_stride(cap)

    def gather(ids_ref, x_scr, g_scr):
        for i in range(cap):
            t = ids_ref[0, 0, i]
            x_scr[pl.ds(i, SUBLANES, stride=stride), :] = (
                src_ref[0, pl.ds(pl.multiple_of(t * SUBLANES, SUBLANES), SUBLANES), :])
            g_scr[pl.ds(i, 1), :] = affr_ref[0, pl.ds(t, 1), :]

    def ffn(x_scr, g_scr, y_scr):
        xg = jnp.concatenate([x_scr[j * stride:j * stride + cap, :].astype(bf16) for j in range(SUBLANES)], axis=1)
        lane = lax.broadcasted_iota(jnp.int32, (1, n_experts), 1)
        gate = jnp.sum(jnp.where(lane == e, g_scr[...], 0.0), axis=-1, keepdims=True)
        half = wg_ref.shape[2] // 2
        y = None
        for c0 in (0, half):
            g = _dot(xg, wg_ref[0, :, c0:c0 + half])
            u = _dot(xg, wu_ref[0, :, c0:c0 + half])
            part = _dot((_silu(g) * u).astype(bf16), wd_ref[0, c0:c0 + half, :])
            y = part if y is None else y + part
        y = y * gate
        for j in range(y.shape[1] // LANES):
            y_scr[j * stride:j * stride + cap, :] = y[:, j * LANES:(j + 1) * LANES]

    def scatter(ids_ref, y_scr):
        for i0 in range(0, cap, SCATTER_UNROLL):
            rows = [pl.multiple_of(ids_ref[0, 0, i0 + u] * SUBLANES, SUBLANES) for u in range(SCATTER_UNROLL)]
            vals = [acc_ref[0, pl.ds(rows[u], SUBLANES), :] + y_scr[pl.ds(i0 + u, SUBLANES, stride=stride), :]
                    for u in range(SCATTER_UNROLL)]
            for u in range(SCATTER_UNROLL):
                acc_ref[0, pl.ds(rows[u], SUBLANES), :] = vals[u]

    @pl.when(e == 0)
    def _():
        acc_ref[...] = jnp.zeros_like(acc_ref)
        y_odd[...] = jnp.zeros_like(y_odd)
        gather(idx_ref, x_even, g_even)

    @pl.when(e % 2 == 0)
    def _():
        gather(idx_next_ref, x_odd, g_odd)
        ffn(x_even, g_even, y_even)
        scatter(idx_prev_ref, y_odd)

    @pl.when(e % 2 == 1)
    def _():
        gather(idx_next_ref, x_even, g_even)
        ffn(x_odd, g_odd, y_odd)
        scatter(idx_prev_ref, y_even)

    @pl.when(e == n_experts - 1)
    def _():
        scatter(idx_ref, y_odd if (n_experts - 1) % 2 else y_even)


def _expert_mixture(idx, h2_slab, aff_rows, wg, wu, wd):
    B, E, cap = idx.shape
    D, FF = wg.shape[1], wg.shape[2]
    rows = (D // LANES) * _chunk_stride(cap)
    wmap = lambda b, e: (e, 0, 0)
    bmap = lambda b, e: (b, 0, 0)
    ids = idx.reshape(B * E, 1, cap)
    ids_spec = lambda m: pl.BlockSpec((1, 1, cap), m, memory_space=pltpu.SMEM)
    return pl.pallas_call(
        functools.partial(_expert_kernel, cap=cap, n_experts=E),
        grid=(B, E),
        in_specs=[ids_spec(lambda b, e: (b * E + jnp.maximum(e - 1, 0), 0, 0)),
                  ids_spec(lambda b, e: (b * E + e, 0, 0)),
                  ids_spec(lambda b, e: (b * E + jnp.minimum(e + 1, E - 1), 0, 0)),
                  pl.BlockSpec((1,) + h2_slab.shape[1:], bmap, pipeline_mode=pl.Buffered(1)),
                  pl.BlockSpec((1,) + aff_rows.shape[1:], bmap, pipeline_mode=pl.Buffered(1)),
                  pl.BlockSpec((1, D, FF), wmap),
                  pl.BlockSpec((1, D, FF), wmap),
                  pl.BlockSpec((1, FF, D), wmap)],
        out_specs=pl.BlockSpec((1,) + h2_slab.shape[1:], bmap, pipeline_mode=pl.Buffered(1)),
        out_shape=jax.ShapeDtypeStruct(h2_slab.shape, f32),
        scratch_shapes=[pltpu.VMEM((rows, LANES), f32), pltpu.VMEM((rows, LANES), f32),
                        pltpu.VMEM((cap, E), f32), pltpu.VMEM((cap, E), f32),
                        pltpu.VMEM((rows, LANES), f32), pltpu.VMEM((rows, LANES), f32)],
        compiler_params=pltpu.CompilerParams(
            dimension_semantics=("parallel", "arbitrary"), vmem_limit_bytes=EXPERT_VMEM_LIMIT_BYTES),
        name="expert_mixture",
    )(ids, ids, ids, h2_slab, aff_rows, wg, wu, wd)


def _final_kernel(x1_ref, g2_ref, acc_ref, o_ref, *, tm):
    ffn = jnp.concatenate([acc_ref[0, pl.ds(j, tm, stride=SUBLANES), :] for j in range(SUBLANES)], axis=-1)
    o_ref[0] = x1_ref[0] + g2_ref[0] * ffn


def _final_residual(x1, g2, acc, *, tm):
    B, N, D = x1.shape
    row_map = lambda b, i: (b, i, 0)
    return pl.pallas_call(
        functools.partial(_final_kernel, tm=tm),
        grid=(B, N // tm),
        in_specs=[pl.BlockSpec((1, tm, D), row_map),
                  pl.BlockSpec((1, 1, D), lambda b, i: (b, 0, 0)),
                  pl.BlockSpec((1, tm * SUBLANES, LANES), row_map)],
        out_specs=pl.BlockSpec((1, tm, D), row_map),
        out_shape=jax.ShapeDtypeStruct((B, N, D), f32),
        compiler_params=pltpu.CompilerParams(dimension_semantics=("parallel", "parallel")),
        name="final_residual",
    )(x1, g2, acc)


def _rope_tables(n):
    rows = n // GRID_W
    row, col = jnp.meshgrid(jnp.arange(rows), jnp.arange(GRID_W), indexing='ij')
    n_freq = HEAD_DIM // 4
    freqs = ROPE_BASE ** (-jnp.arange(n_freq, dtype=f32) / n_freq)
    ang = jnp.concatenate([row.reshape(-1, 1).astype(f32) * freqs, col.reshape(-1, 1).astype(f32) * freqs], -1)
    cos, sin = jnp.cos(ang), jnp.sin(ang)
    reps = LANES // HEAD_DIM
    return (jnp.tile(jnp.concatenate([cos, cos], -1), (1, reps)),
            jnp.tile(jnp.concatenate([-sin, sin], -1), (1, reps)))


def _pack_w_in(w_in):
    D = w_in.shape[0]
    o = 0
    aq = w_in[:, o:o + ATT_WIDTH]; o += ATT_WIDTH
    ak = w_in[:, o:o + KV_WIDTH]; o += KV_WIDTH
    av = w_in[:, o:o + KV_WIDTH]; o += KV_WIDTH
    mq = w_in[:, o:o + M_WIDTH]; o += M_WIDTH
    mk = w_in[:, o:o + M_WIDTH]; o += M_WIDTH
    mv = w_in[:, o:o + M_WIDTH]; o += M_WIDTH
    mo = w_in[:, o:o + M_WIDTH]; o += M_WIDTH
    gates = _gate_order(w_in[:, o:o + N_GATES])

    def dup(a):
        a = a.reshape(D, N_KV_HEADS, 1, HEAD_DIM)
        return jnp.broadcast_to(a, (D, N_KV_HEADS, 2, HEAD_DIM)).reshape(D, K2_WIDTH)

    pad = jnp.zeros((D, LANES - N_GATES), w_in.dtype)
    return jnp.concatenate([aq, dup(ak), av, mq, mk, mv, mo, gates, pad], -1).astype(bf16)


def _gate_order(g):
    H = M_HEADS
    return jnp.concatenate([g[..., 0:H], g[..., 2 * H:3 * H], g[..., H:2 * H], g[..., 3 * H:4 * H]], -1)


def _layer(x, ctx, mod_x, mod_c, norm1_w, norm2_w, w_in, b_gates, conv_qk, q_norm_w, k_norm_w, sink,
           mlstm_norm_w, w_out, w_router, w_gate, w_up, w_down):
    B, N, D = x.shape
    L = ctx.shape[1]
    assert D == SUBLANES * LANES, "token rows are handled as one (8, 128) register tile"
    tm = min(ROW_TILE, N)
    sh1, sc1, g1, sh2, sc2, g2 = [m[:, None, :] for m in jnp.split(mod_x, 6, -1)]
    csh1, csc1 = [jnp.broadcast_to(m[None, None, :], (B, 1, D)) for m in jnp.split(mod_c, 6, -1)[:2]]

    w_all = _pack_w_in(w_in)
    qk_w = jnp.concatenate([jnp.tile(q_norm_w, N_HEADS), jnp.tile(k_norm_w, 2 * N_KV_HEADS)])[None, :]
    bg = jnp.concatenate([_gate_order(b_gates), jnp.zeros((LANES - N_GATES,), f32)])[None, :]
    nw1 = norm1_w[None, :]
    cos_t, sin_t = _rope_tables(N)
    ones_t, zeros_t = jnp.ones((L, LANES), f32), jnp.zeros((L, LANES), f32)

    aq, k2, vt, mq, _, mkt, mv, mo, _, li_rc, lf_rc = _in_projection(x, sh1, sc1, nw1, w_all, cos_t, sin_t, qk_w,
                                                                     conv_qk, bg, tm=tm)
    _, ck2, cvt, _, cmk, _, cmv, _, cgts, _, _ = _in_projection(ctx, csh1, csc1, nw1, w_all, ones_t, zeros_t, qk_w,
                                                                conv_qk, bg, tm=L)

    att = _attention(sink, aq, k2, vt, ck2, cvt)

    n_streams = 2 * M_HEADS
    cn0, m0 = _ctx_states(cmk, cmv, cgts[..., :n_streams], cgts[..., n_streams:])
    m0 = jnp.broadcast_to(jnp.swapaxes(m0, 1, 2), (B, n_streams, CHUNK))
    gstats = _gate_prep(li_rc, lf_rc)
    hf, hb = _mlstm(mq, mkt, mv, gstats, cn0, m0)

    wr_pad = jnp.pad(w_router, ((0, 0), (0, LANES - N_EXPERTS)))
    wr_hi = wr_pad.astype(bf16)
    wr_split = jnp.concatenate([wr_hi, (wr_pad - wr_hi.astype(f32)).astype(bf16)], axis=1)
    x1, h2_slab, aff_t, aff_rows = _out_projection(x, att, hf, hb, mo, mlstm_norm_w[None, :], w_out.astype(bf16),
                                                   g1, norm2_w[None, :], sh2, sc2, wr_split, N_EXPERTS, tm=tm)

    cap = CAPACITY * N // N_EXPERTS
    idx = _route(aff_t, cap).reshape(B, N_EXPERTS, cap)
    acc = _expert_mixture(idx, h2_slab, aff_rows, w_gate.astype(bf16), w_up.astype(bf16), w_down.astype(bf16))
    return _final_residual(x1, g2, acc, tm=tm)


def kernel(x, c, ctx, c_ctx, w_mod, b_mod, norm1_w, norm2_w, w_in, b_gates, conv_qk, q_norm_w, k_norm_w, sink,
           mlstm_norm_w, w_out, w_router, w_gate, w_up, w_down):
    depth = w_mod.shape[0]
    assert depth == 1, "only the final-layer (no context update) form of the block is implemented"
    B = x.shape[0]
    pad_rows = (-(B + 1)) % SUBLANES
    c_all = jnp.concatenate([c, c_ctx[None, :], jnp.zeros((pad_rows, c.shape[1]), c.dtype)], 0)
    mod = _modulation(c_all, w_mod[0], b_mod[0])
    return _layer(x, ctx, mod[:B], mod[B], norm1_w[0], norm2_w[0], w_in[0], b_gates[0], conv_qk[0], q_norm_w[0],
                  k_norm_w[0], sink[0], mlstm_norm_w[0], w_out[0], w_router[0], w_gate[0], w_up[0], w_down[0])
```
